```python
import math
import jax
import jax.numpy as jnp
from jax import lax
import numpy as np

D_MODEL = 1024
BATCH = 2
SEQ = 8192
DEPTH = 1
DEC_BATCH = 8
DEC_SEQ = 16
PAST_LEN = 4096

CHUNK = 64
N_META = 16
EPS = 1e-6
POOL_GROUPS = 4
POOL_WINDOWS = (2, 4, 8, 16)
POOL_GROUP_DIM = D_MODEL // POOL_GROUPS
POOL_WIDTH = POOL_GROUPS * POOL_GROUP_DIM
POOL_STATE = max(POOL_WINDOWS) - 1
N_HEADS = 8
N_KV_HEADS = 2
HEAD_DIM = D_MODEL // N_HEADS
KV_REP = N_HEADS // N_KV_HEADS
Q_WIDTH = N_HEADS * HEAD_DIM
KV_WIDTH = N_KV_HEADS * HEAD_DIM
N_IDX_HEADS = 4
IDX_DIM = 64
IDX_TOPK = 256
IDX_W_SCALE = (N_IDX_HEADS * IDX_DIM) ** -0.5
ATTN_SCALE = HEAD_DIM ** -0.5
QUERY_BLOCK = 128
ROPE_THETA = 500000.0
ROT_FRACTION = 4
IN_WIDTHS = (POOL_WIDTH, Q_WIDTH, KV_WIDTH, KV_WIDTH, N_IDX_HEADS * IDX_DIM, IDX_DIM, N_IDX_HEADS, D_MODEL, D_MODEL)
SPLIT_POINTS = tuple(int(s) for s in np.cumsum(IN_WIDTHS)[:-1])
IN_WIDTH = sum(IN_WIDTHS)
N_EXPERT_GROUPS = 4
EXPERTS_PER_GROUP = 8
N_EXPERTS = N_EXPERT_GROUPS * EXPERTS_PER_GROUP
EXPERT_TOPK = 2
D_EXPERT = 128

kernel_name = 'pool_dsa_hiermoe_streaming_step'


def rms_norm(x, g):
    xf = x.astype(jnp.float32)
    y = xf * lax.rsqrt(jnp.mean(xf * xf, axis=-1, keepdims=True) + EPS) * g.astype(jnp.float32)
    return y.astype(x.dtype)


def chunk_index(p):
    return jnp.where(p < N_META, 0, (p - N_META) // CHUNK + 1)


def rope(x, pos):
    d = x.shape[-1]
    rot = d // ROT_FRACTION
    half = rot // 2
    inv = jnp.exp(-math.log(ROPE_THETA) * jnp.arange(half, dtype=jnp.float32) * (2.0 / rot))
    ang = pos.astype(jnp.float32)[:, None] * inv[None, :]
    cos = jnp.cos(ang)[:, None, :]
    sin = jnp.sin(ang)[:, None, :]
    xf = x.astype(jnp.float32)
    x1 = xf[..., :half]
    x2 = xf[..., half:rot]
    return jnp.concatenate([x1 * cos - x2 * sin, x2 * cos + x1 * sin, xf[..., rot:]], axis=-1).astype(x.dtype)


def in_project(h, w_in, pos):
    B, S, _ = h.shape
    z_pool, q, k, v, qi, ki, wi, ga, gb = jnp.split(h @ w_in, SPLIT_POINTS, axis=-1)
    q = rope(q.reshape(B, S, N_HEADS, HEAD_DIM), pos)
    k = rope(k.reshape(B, S, N_KV_HEADS, HEAD_DIM), pos)
    v = v.reshape(B, S, N_KV_HEADS, HEAD_DIM)
    qi = rope(qi.reshape(B, S, N_IDX_HEADS, IDX_DIM), pos)
    ki = rope(ki[:, :, None, :], pos)[:, :, 0, :]
    wi = wi * IDX_W_SCALE
    return z_pool, q, k, v, qi, ki, wi, ga, gb


def pool_branch(z, n_out, w_pool, pool_scale):
    B, N, _ = z.shape
    zg = z.astype(jnp.float32).reshape(B, N, POOL_GROUPS, POOL_GROUP_DIM)
    cs = jnp.pad(jnp.cumsum(zg, axis=1), ((0, 0), (1, 0), (0, 0), (0, 0)))
    t = jnp.arange(N - n_out, N)
    outs = []
    for g, win in enumerate(POOL_WINDOWS):
        lo = jnp.maximum(t + 1 - win, 0)
        cnt = (t + 1 - lo).astype(jnp.float32)
        mean = (cs[:, t + 1, g] - cs[:, lo, g]) / cnt[None, :, None]
        outs.append(mean - zg[:, N - n_out:, g])
    d = jnp.stack(outs, axis=2)
    y = jnp.einsum('bngc,gcd->bngd', d, w_pool.astype(jnp.float32)).reshape(B, n_out, POOL_WIDTH)
    return (y * pool_scale.astype(jnp.float32)).astype(z.dtype)


def sparse_attention(q, qi, wi, q_chunk, k_all, v_all, ki_all, k_chunk, top_k):
    B, Q = q.shape[:2]
    sc = jnp.einsum('bqhd,bsd->bqhs', qi, ki_all, preferred_element_type=jnp.float32)
    score = jnp.einsum('bqhs,bqh->bqs', jax.nn.relu(sc), wi.astype(jnp.float32))
    admissible = k_chunk[None, :] <= q_chunk[:, None]
    score = jnp.where(admissible[None], score, -jnp.inf)
    _, idx = lax.top_k(score, top_k)
    valid = k_chunk[idx] <= q_chunk[None, :, None]
    k_sel = jax.vmap(lambda kb, ib: kb[ib])(k_all, idx)
    v_sel = jax.vmap(lambda vb, ib: vb[ib])(v_all, idx)
    qg = q.reshape(B, Q, N_KV_HEADS, KV_REP, HEAD_DIM)
    logits = jnp.einsum('bqgrd,bqkgd->bqgrk', qg, k_sel, preferred_element_type=jnp.float32) * ATTN_SCALE
    logits = jnp.where(valid[:, :, None, None, :], logits, -jnp.inf)
    p = jax.nn.softmax(logits, axis=-1)
    o = jnp.einsum('bqgrk,bqkgd->bqgrd', p.astype(v_sel.dtype), v_sel)
    return o.reshape(B, Q, Q_WIDTH)


def merge_branches(a, b, ga, gb, w_out):
    return (jax.nn.sigmoid(ga) * a + jax.nn.sigmoid(gb) * b) @ w_out


def token_mix_prompt(h, pos, w_in, w_pool, pool_scale, w_attn_out, w_out, top_k):
    B, N, _ = h.shape
    z_pool, q, k, v, qi, ki, wi, ga, gb = in_project(h, w_in, pos)
    a = pool_branch(z_pool, N, w_pool, pool_scale)
    k_chunk = chunk_index(pos)
    n_blk = -(-N // QUERY_BLOCK)
    pad = n_blk * QUERY_BLOCK - N

    def to_blocks(t):
        t = jnp.pad(t, [(0, 0), (0, pad)] + [(0, 0)] * (t.ndim - 2))
        return t.reshape((B, n_blk, QUERY_BLOCK) + t.shape[2:]).swapaxes(0, 1)

    q_chunk_blocks = chunk_index(jnp.arange(n_blk * QUERY_BLOCK)).reshape(n_blk, QUERY_BLOCK)

    def one_block(args):
        qb, qib, wib, qcb = args
        return sparse_attention(qb, qib, wib, qcb, k, v, ki, k_chunk, top_k)

    o = lax.map(one_block, (to_blocks(q), to_blocks(qi), to_blocks(wi), q_chunk_blocks))
    o = o.swapaxes(0, 1).reshape(B, n_blk * QUERY_BLOCK, Q_WIDTH)[:, :N]
    y = merge_branches(a, o @ w_attn_out, ga, gb, w_out)
    return y, k, v, ki, z_pool[:, N - POOL_STATE:]


def token_mix_sample(h, pos, cache_k, cache_v, cache_ki, state_pool, w_in, w_pool, pool_scale, w_attn_out, w_out, top_k):
    B, S, _ = h.shape
    z_pool, q, k, v, qi, ki, wi, ga, gb = in_project(h, w_in, pos)
    z_cat = jnp.concatenate([state_pool.astype(z_pool.dtype), z_pool], axis=1)
    a = pool_branch(z_cat, S, w_pool, pool_scale)
    k_all = jnp.concatenate([cache_k.astype(k.dtype), k], axis=1)
    v_all = jnp.concatenate([cache_v.astype(v.dtype), v], axis=1)
    ki_all = jnp.concatenate([cache_ki.astype(ki.dtype), ki], axis=1)
    k_chunk = chunk_index(jnp.arange(k_all.shape[1]))
    o = sparse_attention(q, qi, wi, chunk_index(pos), k_all, v_all, ki_all, k_chunk, top_k)
    y = merge_branches(a, o @ w_attn_out, ga, gb, w_out)
    return y, k, v, ki, z_cat[:, -POOL_STATE:]


def hier_moe(h, w_rg, b_rg, w_re, b_re, w_g, w_u, w_d):
    B, N, _ = h.shape
    g_logits = jnp.einsum('bnd,dg->bng', h, w_rg).astype(jnp.float32) + b_rg.astype(jnp.float32)
    g_sel = jnp.argmax(g_logits, axis=-1)
    g_prob = jnp.take_along_axis(jax.nn.softmax(g_logits, axis=-1), g_sel[..., None], axis=-1)
    e_logits = (jnp.einsum('bnd,de->bne', h, w_re).astype(jnp.float32) + b_re.astype(jnp.float32))
    e_logits = e_logits.reshape(B, N, N_EXPERT_GROUPS, EXPERTS_PER_GROUP)
    e_logits = jnp.take_along_axis(e_logits, g_sel[..., None, None], axis=2)[:, :, 0]
    top_v, top_i = lax.top_k(e_logits, EXPERT_TOPK)
    top_w = jax.nn.softmax(top_v, axis=-1) * g_prob
    within = jnp.einsum('bnk,bnke->bne', top_w, jax.nn.one_hot(top_i, EXPERTS_PER_GROUP, dtype=jnp.float32))
    comb = (jax.nn.one_hot(g_sel, N_EXPERT_GROUPS, dtype=jnp.float32)[..., None] * within[:, :, None, :])
    comb = comb.reshape(B, N, N_EXPERTS)
    act = jax.nn.silu(jnp.einsum('bnd,edf->bnef', h, w_g)) * jnp.einsum('bnd,edf->bnef', h, w_u)
    out = jnp.einsum('bnef,efd->bnd', act * comb.astype(act.dtype)[..., None], w_d)
    return out.astype(h.dtype)


def setup_inputs(seed: int = 0) -> dict:
    key = jax.random.key(seed)
    ks = jax.random.split(key, 22)

    def nrm(k, shape, scale=1.0):
        return jax.random.normal(k, shape, jnp.float32) * scale

    return {
        'x_prompt': nrm(ks[0], (BATCH, SEQ, D_MODEL)),
        'x_sample': nrm(ks[1], (DEC_BATCH, DEC_SEQ, D_MODEL)),
        'cache_k': nrm(ks[2], (DEPTH, DEC_BATCH, N_META + PAST_LEN, N_KV_HEADS, HEAD_DIM)),
        'cache_v': nrm(ks[3], (DEPTH, DEC_BATCH, N_META + PAST_LEN, N_KV_HEADS, HEAD_DIM)),
        'cache_idx_k': nrm(ks[4], (DEPTH, DEC_BATCH, N_META + PAST_LEN, IDX_DIM)),
        'state_pool': nrm(ks[5], (DEPTH, DEC_BATCH, POOL_STATE, POOL_WIDTH)),
        'meta_tokens': nrm(ks[6], (N_META, D_MODEL)),
        'norm_mix_g': 1.0 + nrm(ks[7], (DEPTH, D_MODEL), 0.01),
        'norm_ffn_g': 1.0 + nrm(ks[8], (DEPTH, D_MODEL), 0.01),
        'norm_final_g': 1.0 + nrm(ks[9], (D_MODEL,), 0.01),
        'w_in': nrm(ks[10], (DEPTH, D_MODEL, IN_WIDTH), D_MODEL ** -0.5),
        'w_pool': nrm(ks[11], (DEPTH, POOL_GROUPS, POOL_GROUP_DIM, POOL_GROUP_DIM), POOL_GROUP_DIM ** -0.5),
        'pool_scale': 1.0 + nrm(ks[12], (DEPTH, POOL_WIDTH), 0.01),
        'w_attn_out': nrm(ks[13], (DEPTH, Q_WIDTH, D_MODEL), Q_WIDTH ** -0.5),
        'w_out': nrm(ks[14], (DEPTH, D_MODEL, D_MODEL), D_MODEL ** -0.5),
        'w_router_group': nrm(ks[15], (DEPTH, D_MODEL, N_EXPERT_GROUPS), D_MODEL ** -0.5),
        'b_router_group': nrm(ks[16], (DEPTH, N_EXPERT_GROUPS), 0.01),
        'w_router_expert': nrm(ks[17], (DEPTH, D_MODEL, N_EXPERTS), D_MODEL ** -0.5),
        'b_router_expert': nrm(ks[18], (DEPTH, N_EXPERTS), 0.01),
        'w_expert_gate': nrm(ks[19], (DEPTH, N_EXPERTS, D_MODEL, D_EXPERT), D_MODEL ** -0.5),
        'w_expert_up': nrm(ks[20], (DEPTH, N_EXPERTS, D_MODEL, D_EXPERT), D_MODEL ** -0.5),
        'w_expert_down': nrm(ks[21], (DEPTH, N_EXPERTS, D_EXPERT, D_MODEL), D_EXPERT ** -0.5),
    }


def reference(x_prompt, x_sample, cache_k, cache_v, cache_idx_k, state_pool, meta_tokens,
              norm_mix_g, norm_ffn_g, norm_final_g, w_in, w_pool, pool_scale, w_attn_out, w_out,
              w_router_group, b_router_group, w_router_expert, b_router_expert,
              w_expert_gate, w_expert_up, w_expert_down):
    B, seq = x_prompt.shape[:2]
    dec = x_sample.shape[1]
    past = cache_k.shape[2] - N_META
    top_k_prompt = min(IDX_TOPK, seq // 4)
    top_k_sample = min(IDX_TOPK, (past + dec) // 4)
    xp = jnp.concatenate([jnp.broadcast_to(meta_tokens.astype(x_prompt.dtype)[None], (B, N_META, D_MODEL)), x_prompt], axis=1)
    xs = x_sample
    pos_p = jnp.arange(N_META + seq)
    pos_s = N_META + past + jnp.arange(dec)
    kp_l, vp_l, kip_l, pp_l, ks_l, vs_l, kis_l, ps_l = [], [], [], [], [], [], [], []
    for l in range(DEPTH):
        yp, kp, vp, kip, pp = token_mix_prompt(rms_norm(xp, norm_mix_g[l]), pos_p, w_in[l], w_pool[l], pool_scale[l],
                                               w_attn_out[l], w_out[l], top_k_prompt)
        xp = xp + yp
        xp = xp + hier_moe(rms_norm(xp, norm_ffn_g[l]), w_router_group[l], b_router_group[l], w_router_expert[l],
                           b_router_expert[l], w_expert_gate[l], w_expert_up[l], w_expert_down[l])
        ys, ks_, vs_, kis, ps = token_mix_sample(rms_norm(xs, norm_mix_g[l]), pos_s, cache_k[l], cache_v[l], cache_idx_k[l],
                                                 state_pool[l], w_in[l], w_pool[l], pool_scale[l], w_attn_out[l], w_out[l],
                                                 top_k_sample)
        xs = xs + ys
        xs = xs + hier_moe(rms_norm(xs, norm_ffn_g[l]), w_router_group[l], b_router_group[l], w_router_expert[l],
                           b_router_expert[l], w_expert_gate[l], w_expert_up[l], w_expert_down[l])
        kp_l.append(kp); vp_l.append(vp); kip_l.append(kip); pp_l.append(pp)
        ks_l.append(ks_); vs_l.append(vs_); kis_l.append(kis); ps_l.append(ps)
    y_prompt = rms_norm(xp[:, N_META:], norm_final_g)
    y_sample = rms_norm(xs, norm_final_g)
    k_prompt = jnp.stack(kp_l)
    v_prompt = jnp.stack(vp_l)
    idx_k_prompt = jnp.stack(kip_l)
    pool_prompt = jnp.stack(pp_l)
    k_sample = jnp.stack(ks_l)
    v_sample = jnp.stack(vs_l)
    idx_k_sample = jnp.stack(kis_l)
    pool_sample = jnp.stack(ps_l)
    return (y_prompt, y_sample, k_prompt, v_prompt, idx_k_prompt, pool_prompt, k_sample, v_sample, idx_k_sample, pool_sample)
```

```python
import functools
import math

import numpy as np
import jax
import jax.numpy as jnp
from jax import lax
from jax.experimental import pallas as pl
from jax.experimental.pallas import tpu as pltpu

CHUNK = 64
N_META = 16
EPS = 1e-6
POOL_WINDOWS = (2, 4, 8, 16)
POOL_STATE = max(POOL_WINDOWS) - 1
N_HEADS = 8
N_KV_HEADS = 2
HEAD_DIM = 128
KV_REP = N_HEADS // N_KV_HEADS
N_IDX_HEADS = 4
IDX_DIM = 64
IDX_TOPK = 256
IDX_W_SCALE = (N_IDX_HEADS * IDX_DIM) ** -0.5
ATTN_SCALE = HEAD_DIM ** -0.5
ROPE_THETA = 500000.0
ROT_FRACTION = 4
N_EXPERT_GROUPS = 4
EXPERTS_PER_GROUP = 8
N_EXPERTS = N_EXPERT_GROUPS * EXPERTS_PER_GROUP
D_EXPERT = 128

LANES = 128
VMEM_LIMIT_BYTES = 56 * 1024 * 1024

QK_HALF = HEAD_DIM // ROT_FRACTION // 2
IX_HALF = IDX_DIM // ROT_FRACTION // 2
TAB_COS_QK = 0
TAB_SIN_QK = QK_HALF
TAB_COS_IX = 2 * QK_HALF
TAB_SIN_IX = 2 * QK_HALF + IX_HALF

INT_MIN = -(2 ** 31)
NEG_BIG = -1e30

F32 = jnp.float32
BF16 = jnp.bfloat16
I32 = jnp.int32


def _round_up(x, m):
    return -(-x // m) * m


def _rms(x, g):
    return x * lax.rsqrt(jnp.mean(x * x, axis=-1, keepdims=True) + EPS) * g


def _dot(a, b):
    return jnp.dot(a, b, preferred_element_type=F32)


def _dot_nt(a, b):
    return lax.dot_general(a, b, (((1,), (1,)), ((), ())), preferred_element_type=F32)


def _rope(y, c, a, b, shift_a, shift_b):
    outs = []
    for j in range(y.shape[1] // LANES):
        ys = y[:, j * LANES:(j + 1) * LANES]
        outs.append(ys * c + pltpu.roll(ys, shift_a, 1) * a + pltpu.roll(ys, shift_b, 1) * b)
    return outs[0] if len(outs) == 1 else jnp.concatenate(outs, axis=1)


def _inproj_kernel(x_ref, g_ref, tab_ref, wz_ref, wq_ref, wkv_ref, wqi_ref, wsm_ref, wgab_ref,
                   z_ref, q_ref, k_ref, v_ref, kb_ref, vb_ref, qis_ref, sm_ref, kid_ref, sga_ref, sgb_ref):
    h = _rms(x_ref[...], g_ref[...]).astype(BF16)
    tm = h.shape[0]
    d = x_ref.shape[1]
    lane = lax.broadcasted_iota(I32, (tm, LANES), 1)
    t = tab_ref[...]

    def rl(s):
        return pltpu.roll(t, s % LANES, 1) if s % LANES else t

    c_qk = jnp.where(lane < QK_HALF, rl(-TAB_COS_QK), jnp.where(lane < 2 * QK_HALF, rl(QK_HALF - TAB_COS_QK), 1.0))
    a_qk = jnp.where(lane < QK_HALF, -rl(-TAB_SIN_QK), 0.0)
    b_qk = jnp.where((lane >= QK_HALF) & (lane < 2 * QK_HALF), rl(QK_HALF - TAB_SIN_QK), 0.0)
    l64 = lane & (IDX_DIM - 1)
    hi = lane >= IDX_DIM
    cos_lo = jnp.where(hi, rl(IDX_DIM - TAB_COS_IX), rl(-TAB_COS_IX))
    cos_hi = jnp.where(hi, rl(IDX_DIM + IX_HALF - TAB_COS_IX), rl(IX_HALF - TAB_COS_IX))
    sin_lo = jnp.where(hi, rl(IDX_DIM - TAB_SIN_IX), rl(-TAB_SIN_IX))
    sin_hi = jnp.where(hi, rl(IDX_DIM + IX_HALF - TAB_SIN_IX), rl(IX_HALF - TAB_SIN_IX))
    c_ix = jnp.where(l64 < IX_HALF, cos_lo, jnp.where(l64 < 2 * IX_HALF, cos_hi, 1.0))
    a_ix = jnp.where(l64 < IX_HALF, -sin_lo, 0.0)
    b_ix = jnp.where((l64 >= IX_HALF) & (l64 < 2 * IX_HALF), sin_hi, 0.0)

    z_ref[...] = _dot(h, wz_ref[...])

    q = _rope(_dot(h, wq_ref[...]), c_qk, a_qk, b_qk, LANES - QK_HALF, QK_HALF)
    q_ref[...] = q.astype(BF16)

    kv = _dot(h, wkv_ref[...])
    kvw = kv.shape[1] // 2
    k = _rope(kv[:, :kvw], c_qk, a_qk, b_qk, LANES - QK_HALF, QK_HALF)
    v = kv[:, kvw:]
    k_ref[...] = k
    v_ref[...] = v
    kb_ref[...] = k.astype(BF16)
    vb_ref[...] = v.astype(BF16)

    qi = _rope(_dot(h, wqi_ref[...]), c_ix, a_ix, b_ix, LANES - IX_HALF, IX_HALF)
    pieces = []
    for hd in range(N_IDX_HEADS):
        col = qi[:, (hd // 2) * LANES:(hd // 2 + 1) * LANES]
        keep = hi if hd % 2 else jnp.logical_not(hi)
        pieces.append(jnp.where(keep, col, 0.0))
    qis_ref[...] = jnp.concatenate(pieces, axis=1).astype(BF16)

    c_sm = jnp.where(hi, IDX_W_SCALE, c_ix)
    a_sm = jnp.where(hi, 0.0, a_ix)
    b_sm = jnp.where(hi, 0.0, b_ix)
    sm = _rope(_dot(h, wsm_ref[...]), c_sm, a_sm, b_sm, LANES - IX_HALF, IX_HALF)
    sm_ref[...] = sm
    kid_ref[...] = jnp.where(hi, pltpu.roll(sm, IDX_DIM, 1), sm).astype(BF16)

    gab = _dot(h, wgab_ref[...])
    sga_ref[...] = jax.nn.sigmoid(gab[:, :d])
    sgb_ref[...] = jax.nn.sigmoid(gab[:, d:])


def _inproj(x2d, g, tab, w, *, tm, tab_tiles):
    rows, d = x2d.shape
    nt = rows // tm
    qw = w["wq"].shape[1]
    kvw = w["wkv"].shape[1] // 2
    row_spec = lambda n: pl.BlockSpec((tm, n), lambda i: (i, 0))
    const = lambda a: pl.BlockSpec(a.shape, lambda i: (0,) * a.ndim)
    outs = [
        (d, F32),
        (qw, BF16),
        (kvw, F32), (kvw, F32),
        (kvw, BF16), (kvw, BF16),
        (N_IDX_HEADS * LANES, BF16),
        (LANES, F32),
        (LANES, BF16),
        (d, F32), (d, F32),
    ]
    return pl.pallas_call(
        _inproj_kernel,
        grid=(nt,),
        in_specs=[row_spec(d), const(g),
                  pl.BlockSpec((tm, LANES), lambda i: (i % tab_tiles, 0)),
                  const(w["wz"]), const(w["wq"]), const(w["wkv"]), const(w["wqi"]), const(w["wsm"]), const(w["wgab"])],
        out_specs=[row_spec(n) for n, _ in outs],
        out_shape=[jax.ShapeDtypeStruct((rows, n), dt) for n, dt in outs],
        compiler_params=pltpu.CompilerParams(dimension_semantics=("arbitrary",), vmem_limit_bytes=VMEM_LIMIT_BYTES),
        name="inproj",
    )(x2d, g, tab, w["wz"], w["wq"], w["wkv"], w["wqi"], w["wsm"], w["wgab"])


def _attn_kernel(nch_ref, qis_ref, sm_ref, kid_ref, q_ref, k_ref, v_ref, tri_ref, o_ref,
                 keys_scr, m_scr, l_scr, acc_scr, *, tq, tk, topk, causal, n_keys):
    i = pl.program_id(1)
    nch = nch_ref[i]
    row = i * tq + lax.broadcasted_iota(I32, (tq, 1), 0)
    if causal:
        lim = N_META + CHUNK * (jnp.right_shift(row - N_META, int(math.log2(CHUNK))) + 1)
        lim = jnp.minimum(lim, n_keys)
    else:
        lim = jnp.full((tq, 1), n_keys, I32)

    qstack = jnp.concatenate([qis_ref[:, hd * LANES:(hd + 1) * LANES] for hd in range(N_IDX_HEADS)], axis=0)
    wstack = jnp.concatenate([sm_ref[:, IDX_DIM + hd:IDX_DIM + hd + 1] for hd in range(N_IDX_HEADS)], axis=0)

    def score_body(c, carry):
        off = pl.multiple_of(c * tk, tk)
        sc = _dot_nt(qstack, kid_ref[pl.ds(off, tk), :])
        sc = jnp.maximum(sc, 0.0) * wstack
        s = sc[0:tq]
        for hd in range(1, N_IDX_HEADS):
            s = s + sc[hd * tq:(hd + 1) * tq]
        s = jnp.where(s == 0.0, 0.0, s)
        bits = pltpu.bitcast(s, I32)
        key = bits ^ (jnp.right_shift(bits, 31) & 0x7FFFFFFF)
        kidx = off + lax.broadcasted_iota(I32, (tq, tk), 1)
        keys_scr[:, pl.ds(off, tk)] = jnp.where(kidx < lim, key, INT_MIN)
        return carry

    lax.fori_loop(0, nch, score_body, 0)

    def count_ge(cand, strict):
        def body(c, acc):
            off = pl.multiple_of(c * tk, tk)
            kb = keys_scr[:, pl.ds(off, tk)]
            hit = jnp.where((kb > cand) if strict else (kb >= cand), 1.0, 0.0)
            for j in range(tk // LANES):
                acc = acc + hit[:, j * LANES:(j + 1) * LANES]
            return acc
        acc = lax.fori_loop(0, nch, body, jnp.zeros((tq, LANES), F32))
        return jnp.sum(acc, axis=1, keepdims=True)

    def bit_body(t, cur):
        cand = cur + lax.shift_left(jnp.int32(1), 31 - t)
        return jnp.where(count_ge(cand, False) >= topk, cand, cur)

    thr = lax.fori_loop(0, 32, bit_body, jnp.full((tq, 1), INT_MIN, I32))
    n_tie = jnp.where(thr == INT_MIN, 0.0, topk - count_ge(thr, True))

    m_scr[...] = jnp.full(m_scr.shape, NEG_BIG, F32)
    l_scr[...] = jnp.zeros(l_scr.shape, F32)
    acc_scr[...] = jnp.zeros(acc_scr.shape, F32)

    def flash_body(c, seen):
        off = pl.multiple_of(c * tk, tk)
        kb = keys_scr[:, pl.ds(off, tk)]
        eq = kb == thr
        eqf = jnp.where(eq, 1.0, 0.0)
        rank = seen + _dot(eqf.astype(BF16), tri_ref[...])
        sel = (kb > thr) | (eq & (rank < n_tie))
        maskf = jnp.where(sel, 0.0, NEG_BIG)
        mask4 = jnp.concatenate([maskf] * KV_REP, axis=0)
        for g in range(N_KV_HEADS):
            qg = jnp.concatenate([q_ref[:, (g * KV_REP + r) * HEAD_DIM:(g * KV_REP + r + 1) * HEAD_DIM]
                                  for r in range(KV_REP)], axis=0)
            kc = k_ref[pl.ds(off, tk), g * HEAD_DIM:(g + 1) * HEAD_DIM]
            vc = v_ref[pl.ds(off, tk), g * HEAD_DIM:(g + 1) * HEAD_DIM]
            s = _dot_nt(qg, kc) * ATTN_SCALE + mask4
            m_prev = m_scr[g]
            m_new = jnp.maximum(m_prev, jnp.max(s, axis=1, keepdims=True))
            alpha = jnp.exp(m_prev - m_new)
            p = jnp.exp(s - m_new)
            l_scr[g] = alpha * l_scr[g] + jnp.sum(p, axis=1, keepdims=True)
            acc_scr[g] = alpha * acc_scr[g] + _dot(p.astype(BF16), vc)
            m_scr[g] = m_new
        return seen + jnp.sum(eqf, axis=1, keepdims=True)

    lax.fori_loop(0, nch, flash_body, jnp.zeros((tq, 1), F32))

    for g in range(N_KV_HEADS):
        og = acc_scr[g] / l_scr[g]
        for r in range(KV_REP):
            hd = g * KV_REP + r
            o_ref[:, hd * HEAD_DIM:(hd + 1) * HEAD_DIM] = og[r * tq:(r + 1) * tq].astype(o_ref.dtype)


def _attention(qis, sm, kid, q, kb, vb, *, tq, tk, topk, causal, n_keys):
    bsz, sq, _ = q.shape
    sk = kb.shape[1]
    nq = sq // tq
    if causal:
        ends = np.minimum(N_META + CHUNK * ((np.arange(nq) * tq + tq - 1 - N_META) // CHUNK + 1), n_keys)
    else:
        ends = np.full((nq,), n_keys)
    nch = jnp.asarray(-(-ends // tk), I32)
    tri = jnp.asarray(np.triu(np.ones((tk, tk), np.float32), 1), BF16)
    qrow = lambda n: pl.BlockSpec((None, tq, n), lambda b, i, s: (b, i, 0))
    krow = lambda n: pl.BlockSpec((None, sk, n), lambda b, i, s: (b, 0, 0))
    kern = functools.partial(_attn_kernel, tq=tq, tk=tk, topk=topk, causal=causal, n_keys=n_keys)
    return pl.pallas_call(
        kern,
        grid_spec=pltpu.PrefetchScalarGridSpec(
            num_scalar_prefetch=1,
            grid=(bsz, nq),
            in_specs=[qrow(qis.shape[2]), qrow(sm.shape[2]), krow(kid.shape[2]), qrow(q.shape[2]),
                      krow(kb.shape[2]), krow(vb.shape[2]),
                      pl.BlockSpec((tk, tk), lambda b, i, s: (0, 0))],
            out_specs=qrow(q.shape[2]),
            scratch_shapes=[pltpu.VMEM((tq, sk), I32),
                            pltpu.VMEM((N_KV_HEADS, KV_REP * tq, 1), F32),
                            pltpu.VMEM((N_KV_HEADS, KV_REP * tq, 1), F32),
                            pltpu.VMEM((N_KV_HEADS, KV_REP * tq, HEAD_DIM), F32)]),
        out_shape=jax.ShapeDtypeStruct(q.shape, BF16),
        compiler_params=pltpu.CompilerParams(dimension_semantics=("arbitrary", "arbitrary"),
                                             vmem_limit_bytes=VMEM_LIMIT_BYTES),
        name="attn",
    )(nch, qis, sm, kid, q, kb, vb, tri)


def _mix_kernel(tb_ref, o_ref, z_ref, halo_ref, sga_ref, sgb_ref, x_ref, wpool_ref, ps_ref, wao_ref, wout_ref,
                x1_ref, zc_scr, *, tm):
    hs = halo_ref.shape[0]
    zc_scr[0:hs, :] = halo_ref[...]
    zc_scr[hs:hs + tm, :] = z_ref[...]
    avail = tb_ref[pl.program_id(0)] + lax.broadcasted_iota(I32, (tm, 1), 0) + 1
    gw = z_ref.shape[1] // len(POOL_WINDOWS)
    ys = []
    for g, win in enumerate(POOL_WINDOWS):
        cols = slice(g * gw, (g + 1) * gw)
        zt = zc_scr[hs:hs + tm, cols]
        s = zt
        for j in range(1, win):
            s = s + zc_scr[hs - j:hs - j + tm, cols]
        cnt = jnp.minimum(avail, win).astype(F32)
        dlt = s / cnt - zt
        ys.append(_dot(dlt.astype(BF16), wpool_ref[g]))
    a = jnp.concatenate(ys, axis=1) * ps_ref[...]
    b = _dot(o_ref[...], wao_ref[...])
    mrg = sga_ref[...] * a + sgb_ref[...] * b
    x1_ref[...] = x_ref[...] + _dot(mrg.astype(BF16), wout_ref[...])


def _mix(tbase, o, z, halo, sga, sgb, x, w, *, tm):
    rows, d = x.shape
    nt = rows // tm
    hs = halo.shape[1]
    row = lambda n: pl.BlockSpec((tm, n), lambda i, s: (i, 0))
    const = lambda a: pl.BlockSpec(a.shape, lambda i, s: (0,) * a.ndim)
    return pl.pallas_call(
        functools.partial(_mix_kernel, tm=tm),
        grid_spec=pltpu.PrefetchScalarGridSpec(
            num_scalar_prefetch=1,
            grid=(nt,),
            in_specs=[row(d), row(d), pl.BlockSpec((None, hs, d), lambda i, s: (i, 0, 0)), row(d), row(d), row(d),
                      const(w["wpool"]), const(w["pscale"]), const(w["wao"]), const(w["wout"])],
            out_specs=row(d),
            scratch_shapes=[pltpu.VMEM((hs + tm, d), F32)]),
        out_shape=jax.ShapeDtypeStruct((rows, d), F32),
        compiler_params=pltpu.CompilerParams(dimension_semantics=("arbitrary",), vmem_limit_bytes=VMEM_LIMIT_BYTES),
        name="mix",
    )(tbase, o, z, halo, sga, sgb, x, w["wpool"], w["pscale"], w["wao"], w["wout"])


ROUTE_GROUP_LANE0 = N_EXPERTS


def _moe_kernel(x_ref, gf_ref, gl_ref, wrh_ref, wrl_ref, br_ref, wg_ref, wu_ref, wd_ref, y_ref,
                h_scr, route_scr, acc_scr, *, ec):
    j = pl.program_id(1)
    tm = x_ref.shape[0]

    @pl.when(j == 0)
    def _():
        hf = _rms(x_ref[...], gf_ref[...])
        h_hi = hf.astype(BF16)
        h_lo = (hf - h_hi.astype(F32)).astype(BF16)
        h_scr[...] = h_hi
        logits = _dot(h_hi, wrh_ref[...]) + _dot(h_lo, wrh_ref[...]) + _dot(h_hi, wrl_ref[...]) + br_ref[...]
        lane = lax.broadcasted_iota(I32, (tm, LANES), 1)
        lanef = lane.astype(F32)
        ninf = -jnp.inf
        isg = (lane >= ROUTE_GROUP_LANE0) & (lane < ROUTE_GROUP_LANE0 + N_EXPERT_GROUPS)
        gl = jnp.where(isg, logits, ninf)
        gmax = jnp.max(gl, axis=1, keepdims=True)
        gsel = jnp.min(jnp.where(gl == gmax, lanef, 1e9), axis=1, keepdims=True)
        gprob = 1.0 / jnp.sum(jnp.exp(gl - gmax), axis=1, keepdims=True)
        e0 = (gsel - ROUTE_GROUP_LANE0) * EXPERTS_PER_GROUP
        el = jnp.where((lanef >= e0) & (lanef < e0 + EXPERTS_PER_GROUP), logits, ninf)
        v1 = jnp.max(el, axis=1, keepdims=True)
        i1 = jnp.min(jnp.where(el == v1, lanef, 1e9), axis=1, keepdims=True)
        el2 = jnp.where(lanef == i1, ninf, el)
        v2 = jnp.max(el2, axis=1, keepdims=True)
        i2 = jnp.min(jnp.where(el2 == v2, lanef, 1e9), axis=1, keepdims=True)
        e = jnp.exp(v2 - v1)
        w1 = (1.0 / (1.0 + e)) * gprob
        w2 = (e / (1.0 + e)) * gprob
        for n, val in enumerate((i1, i2, w1, w2)):
            route_scr[n] = jnp.broadcast_to(val, (tm, LANES))
        acc_scr[...] = jnp.zeros(acc_scr.shape, F32)

    h = h_scr[...]
    i1, i2, w1, w2 = route_scr[0], route_scr[1], route_scr[2], route_scr[3]
    pair = 2 * D_EXPERT
    for p in range(ec // 2):
        gate = _dot(h, wg_ref[:, p * pair:(p + 1) * pair])
        up = _dot(h, wu_ref[:, p * pair:(p + 1) * pair])
        act = gate * jax.nn.sigmoid(gate) * up
        halves = []
        for q in range(2):
            ef = jnp.full((tm, LANES), j * ec + 2 * p + q, I32).astype(F32)
            comb = jnp.where(i1 == ef, w1, 0.0) + jnp.where(i2 == ef, w2, 0.0)
            halves.append(act[:, q * D_EXPERT:(q + 1) * D_EXPERT] * comb)
        act = jnp.concatenate(halves, axis=1).astype(BF16)
        acc_scr[...] += _dot(act, wd_ref[p * pair:(p + 1) * pair, :])

    @pl.when(j == pl.num_programs(1) - 1)
    def _():
        y_ref[...] = _rms(x_ref[...] + acc_scr[...], gl_ref[...])


def _moe(x, w, *, tm, ec=4):
    rows, d = x.shape
    nt = rows // tm
    ne = N_EXPERTS // ec
    cw = ec * D_EXPERT
    const = lambda a: pl.BlockSpec(a.shape, lambda i, j: (0,) * a.ndim)
    return pl.pallas_call(
        functools.partial(_moe_kernel, ec=ec),
        grid=(nt, ne),
        in_specs=[pl.BlockSpec((tm, d), lambda i, j: (i, 0)), const(w["gffn"]), const(w["gfinal"]),
                  const(w["wr_hi"]), const(w["wr_lo"]), const(w["br"]),
                  pl.BlockSpec((d, cw), lambda i, j: (0, j)), pl.BlockSpec((d, cw), lambda i, j: (0, j)),
                  pl.BlockSpec((cw, d), lambda i, j: (j, 0))],
        out_specs=pl.BlockSpec((tm, d), lambda i, j: (i, 0)),
        out_shape=jax.ShapeDtypeStruct((rows, d), F32),
        scratch_shapes=[pltpu.VMEM((tm, d), BF16), pltpu.VMEM((4, tm, LANES), F32), pltpu.VMEM((tm, d), F32)],
        compiler_params=pltpu.CompilerParams(dimension_semantics=("arbitrary", "arbitrary"),
                                             vmem_limit_bytes=VMEM_LIMIT_BYTES),
        name="moe",
    )(x, w["gffn"], w["gfinal"], w["wr_hi"], w["wr_lo"], w["br"], w["wg"], w["wu"], w["wd"])


def _rope_table(pos):
    posf = pos.astype(F32)[:, None]

    def cs(half):
        inv = jnp.exp(-math.log(ROPE_THETA) * jnp.arange(half, dtype=F32) * (1.0 / half))
        ang = posf * inv[None, :]
        return jnp.cos(ang), jnp.sin(ang)

    cq, sq = cs(QK_HALF)
    ci, si = cs(IX_HALF)
    used = 2 * QK_HALF + 2 * IX_HALF
    return jnp.concatenate([cq, sq, ci, si, jnp.zeros((pos.shape[0], LANES - used), F32)], axis=1)


def _prep_weights(l, norm_mix_g, norm_ffn_g, norm_final_g, w_in, w_pool, pool_scale, w_attn_out, w_out,
                  w_router_group, b_router_group, w_router_expert, b_router_expert,
                  w_expert_gate, w_expert_up, w_expert_down):
    d = w_in.shape[1]
    pw = w_pool.shape[1] * w_pool.shape[2]
    qw = N_HEADS * HEAD_DIM
    kvw = N_KV_HEADS * HEAD_DIM
    widths = (pw, qw, kvw, kvw, N_IDX_HEADS * IDX_DIM, IDX_DIM, N_IDX_HEADS, d, d)
    offs = np.concatenate([[0], np.cumsum(widths)])
    wi = w_in[l]
    piece = lambda a, b: wi[:, offs[a]:offs[b]].astype(BF16)
    wsm = jnp.concatenate([wi[:, offs[5]:offs[7]], jnp.zeros((d, LANES - IDX_DIM - N_IDX_HEADS), F32)], axis=1)
    wr = jnp.concatenate([w_router_expert[l], w_router_group[l],
                          jnp.zeros((d, LANES - N_EXPERTS - N_EXPERT_GROUPS), F32)], axis=1)
    wr_hi = wr.astype(BF16)
    br = jnp.concatenate([b_router_expert[l], b_router_group[l],
                          jnp.zeros((LANES - N_EXPERTS - N_EXPERT_GROUPS,), F32)])[None, :]
    flat_e = lambda a: a.transpose(1, 0, 2).reshape(d, N_EXPERTS * D_EXPERT).astype(BF16)
    return dict(
        gmix=norm_mix_g[l][None, :], gffn=norm_ffn_g[l][None, :], gfinal=norm_final_g[None, :],
        wz=piece(0, 1), wq=piece(1, 2), wkv=piece(2, 4), wqi=piece(4, 5), wsm=wsm.astype(BF16), wgab=piece(7, 9),
        wpool=w_pool[l].astype(BF16), pscale=pool_scale[l][None, :],
        wao=w_attn_out[l].astype(BF16), wout=w_out[l].astype(BF16),
        wr_hi=wr_hi, wr_lo=(wr - wr_hi.astype(F32)).astype(BF16), br=br,
        wg=flat_e(w_expert_gate[l]), wu=flat_e(w_expert_up[l]),
        wd=w_expert_down[l].reshape(N_EXPERTS * D_EXPERT, d).astype(BF16),
    )


def _dup_lanes(ki):
    return jnp.concatenate([ki, ki], axis=-1)


def kernel(x_prompt, x_sample, cache_k, cache_v, cache_idx_k, state_pool, meta_tokens, norm_mix_g, norm_ffn_g,
           norm_final_g, w_in, w_pool, pool_scale, w_attn_out, w_out, w_router_group, b_router_group,
           w_router_expert, b_router_expert, w_expert_gate, w_expert_up, w_expert_down):
    bsz, seq, d = x_prompt.shape
    dbsz, dec, _ = x_sample.shape
    assert w_in.shape[0] == 1, "single-layer step"
    l = 0
    past = cache_k.shape[2] - N_META
    n = N_META + seq
    topk_p = min(IDX_TOPK, seq // 4)
    topk_s = min(IDX_TOPK, (past + dec) // 4)
    kvw = N_KV_HEADS * HEAD_DIM

    tm_moe = 768 if n >= 768 else 256
    tm = tm_moe // 2
    tq, tk = 128, 256
    sp = _round_up(n, tm_moe)
    w = _prep_weights(l, norm_mix_g, norm_ffn_g, norm_final_g, w_in, w_pool, pool_scale, w_attn_out, w_out,
                      w_router_group, b_router_group, w_router_expert, b_router_expert,
                      w_expert_gate, w_expert_up, w_expert_down)

    xp = jnp.concatenate([jnp.broadcast_to(meta_tokens.astype(F32)[None], (bsz, N_META, d)), x_prompt,
                          jnp.zeros((bsz, sp - n, d), F32)], axis=1).reshape(bsz * sp, d)
    nt = sp // tm
    tab_p = _rope_table(jnp.arange(sp))
    z, q, k, v, kb, vb, qis, sm, kid, sga, sgb = _inproj(xp, w["gmix"], tab_p, w, tm=tm, tab_tiles=nt)
    b3 = lambda a: a.reshape(bsz, sp, a.shape[-1])
    o = _attention(b3(qis), b3(sm), b3(kid), b3(q), b3(kb), b3(vb),
                   tq=tq, tk=tk, topk=topk_p, causal=True, n_keys=n)
    hs = POOL_STATE + 1
    z4 = z.reshape(bsz, nt, tm, d)
    halo = jnp.concatenate([jnp.zeros((bsz, 1, hs, d), F32), z4[:, :-1, tm - hs:, :]], axis=1).reshape(bsz * nt, hs, d)
    tbase = jnp.asarray(np.tile(np.arange(nt) * tm, bsz), I32)
    x1 = _mix(tbase, o.reshape(bsz * sp, -1), z, halo, sga, sgb, xp, w, tm=tm)
    yp = _moe(x1, w, tm=tm_moe)
    y_prompt = yp.reshape(bsz, sp, d)[:, N_META:n]
    k_prompt = b3(k)[:, :n].reshape(1, bsz, n, N_KV_HEADS, HEAD_DIM)
    v_prompt = b3(v)[:, :n].reshape(1, bsz, n, N_KV_HEADS, HEAD_DIM)
    idx_k_prompt = b3(sm)[:, :n, :IDX_DIM][None]
    pool_prompt = b3(z)[:, n - POOL_STATE:n][None]

    rows_s = dbsz * dec
    tab_s = jnp.tile(_rope_table(N_META + past + jnp.arange(dec)), (dbsz, 1))
    xs = x_sample.reshape(rows_s, d)
    zs, qs, ks, vs, kbs, vbs, qiss, sms, kids, sgas, sgbs = _inproj(xs, w["gmix"], tab_s, w, tm=rows_s, tab_tiles=1)
    nk = N_META + past + dec
    sk = _round_up(nk, tk)
    s3 = lambda a: a.reshape(dbsz, dec, a.shape[-1])
    padk = lambda a: jnp.concatenate([a, jnp.zeros((dbsz, sk - nk, a.shape[-1]), a.dtype)], axis=1)
    k_all = padk(jnp.concatenate([cache_k[l].reshape(dbsz, N_META + past, kvw).astype(BF16), s3(kbs)], axis=1))
    v_all = padk(jnp.concatenate([cache_v[l].reshape(dbsz, N_META + past, kvw).astype(BF16), s3(vbs)], axis=1))
    ki_all = padk(jnp.concatenate([_dup_lanes(cache_idx_k[l]).astype(BF16), s3(kids)], axis=1))
    os_ = _attention(s3(qiss), s3(sms), ki_all, s3(qs), k_all, v_all,
                     tq=dec, tk=tk, topk=topk_s, causal=False, n_keys=nk)
    zcat = jnp.concatenate([state_pool[l].astype(F32), s3(zs)], axis=1)
    halo_s = jnp.concatenate([jnp.zeros((dbsz, 1, d), F32), zcat[:, :POOL_STATE]], axis=1)
    tbase_s = jnp.full((dbsz,), POOL_STATE, I32)
    x1s = _mix(tbase_s, os_.reshape(rows_s, -1), zs, halo_s, sgas, sgbs, xs, w, tm=dec)
    ys = _moe(x1s, w, tm=rows_s)
    y_sample = ys.reshape(dbsz, dec, d)
    k_sample = s3(ks).reshape(1, dbsz, dec, N_KV_HEADS, HEAD_DIM)
    v_sample = s3(vs).reshape(1, dbsz, dec, N_KV_HEADS, HEAD_DIM)
    idx_k_sample = s3(sms)[:, :, :IDX_DIM][None]
    pool_sample = zcat[:, zcat.shape[1] - POOL_STATE:][None]

    return (y_prompt, y_sample, k_prompt, v_prompt, idx_k_prompt, pool_prompt,
            k_sample, v_sample, idx_k_sample, pool_sample)
```

```python
import functools
import math

import numpy as np
import jax
import jax.numpy as jnp
from jax import lax
from jax.experimental import pallas as pl
from jax.experimental.pallas import tpu as pltpu

CHUNK = 64
N_META = 16
EPS = 1e-6
POOL_WINDOWS = (2, 4, 8, 16)
POOL_STATE = max(POOL_WINDOWS) - 1
N_HEADS = 8
N_KV_HEADS = 2
HEAD_DIM = 128
KV_REP = N_HEADS // N_KV_HEADS
N_IDX_HEADS = 4
IDX_DIM = 64
IDX_TOPK = 256
IDX_W_SCALE = (N_IDX_HEADS * IDX_DIM) ** -0.5
ATTN_SCALE = HEAD_DIM ** -0.5
Q_PRESCALE = ATTN_SCALE * math.log2(math.e)
ROPE_THETA = 500000.0
ROT_FRACTION = 4
N_EXPERT_GROUPS = 4
EXPERTS_PER_GROUP = 8
N_EXPERTS = N_EXPERT_GROUPS * EXPERTS_PER_GROUP
D_EXPERT = 128

LANES = 128
BF16_SUBLANES = 16
VMEM_LIMIT_BYTES = 56 * 1024 * 1024

QK_HALF = HEAD_DIM // ROT_FRACTION // 2
IX_HALF = IDX_DIM // ROT_FRACTION // 2
TAB_COS_QK = 0
TAB_SIN_QK = QK_HALF
TAB_COS_IX = 2 * QK_HALF
TAB_SIN_IX = 2 * QK_HALF + IX_HALF

TIE_BLOCK = 256
INT_MIN = -(2 ** 31)
NEG_BIG = -1e30

F32 = jnp.float32
BF16 = jnp.bfloat16
I32 = jnp.int32


def _round_up(x, m):
    return -(-x // m) * m


def _pick_tile(rows, cap):
    return max(t for t in range(BF16_SUBLANES, cap + 1, BF16_SUBLANES) if rows % t == 0)


def _rms(x, g):
    return x * lax.rsqrt(jnp.mean(x * x, axis=-1, keepdims=True) + EPS) * g


def _dot(a, b):
    return jnp.dot(a, b, preferred_element_type=F32)


def _dot_nt(a, b):
    return lax.dot_general(a, b, (((1,), (1,)), ((), ())), preferred_element_type=F32)


def _rope(y, c, a, b, shift_a, shift_b):
    outs = []
    for j in range(y.shape[1] // LANES):
        ys = y[:, j * LANES:(j + 1) * LANES]
        outs.append(ys * c + pltpu.roll(ys, shift_a, 1) * a + pltpu.roll(ys, shift_b, 1) * b)
    return outs[0] if len(outs) == 1 else jnp.concatenate(outs, axis=1)


def _inproj_kernel(x_ref, g_ref, tab_ref, wz_ref, wq_ref, wkv_ref, wqi_ref, wsm_ref, wgab_ref,
                   z_ref, q_ref, k_ref, v_ref, kb_ref, vb_ref, qis_ref, sm_ref, kid_ref, sga_ref, sgb_ref):
    h = _rms(x_ref[...], g_ref[...]).astype(BF16)
    tm = h.shape[0]
    d = x_ref.shape[1]
    lane = lax.broadcasted_iota(I32, (tm, LANES), 1)
    t = tab_ref[...]

    def rl(s):
        return pltpu.roll(t, s % LANES, 1) if s % LANES else t

    c_qk = jnp.where(lane < QK_HALF, rl(-TAB_COS_QK), jnp.where(lane < 2 * QK_HALF, rl(QK_HALF - TAB_COS_QK), 1.0))
    a_qk = jnp.where(lane < QK_HALF, -rl(-TAB_SIN_QK), 0.0)
    b_qk = jnp.where((lane >= QK_HALF) & (lane < 2 * QK_HALF), rl(QK_HALF - TAB_SIN_QK), 0.0)
    l64 = lane & (IDX_DIM - 1)
    hi = lane >= IDX_DIM
    cos_lo = jnp.where(hi, rl(IDX_DIM - TAB_COS_IX), rl(-TAB_COS_IX))
    cos_hi = jnp.where(hi, rl(IDX_DIM + IX_HALF - TAB_COS_IX), rl(IX_HALF - TAB_COS_IX))
    sin_lo = jnp.where(hi, rl(IDX_DIM - TAB_SIN_IX), rl(-TAB_SIN_IX))
    sin_hi = jnp.where(hi, rl(IDX_DIM + IX_HALF - TAB_SIN_IX), rl(IX_HALF - TAB_SIN_IX))
    c_ix = jnp.where(l64 < IX_HALF, cos_lo, jnp.where(l64 < 2 * IX_HALF, cos_hi, 1.0))
    a_ix = jnp.where(l64 < IX_HALF, -sin_lo, 0.0)
    b_ix = jnp.where((l64 >= IX_HALF) & (l64 < 2 * IX_HALF), sin_hi, 0.0)

    z_ref[...] = _dot(h, wz_ref[...])

    q = _rope(_dot(h, wq_ref[...]), c_qk, a_qk, b_qk, LANES - QK_HALF, QK_HALF)
    q_ref[...] = (q * Q_PRESCALE).astype(BF16)

    kv = _dot(h, wkv_ref[...])
    kvw = kv.shape[1] // 2
    k = _rope(kv[:, :kvw], c_qk, a_qk, b_qk, LANES - QK_HALF, QK_HALF)
    v = kv[:, kvw:]
    k_ref[...] = k
    v_ref[...] = v
    kb_ref[...] = k.astype(BF16)
    vb_ref[...] = v.astype(BF16)

    qi = _rope(_dot(h, wqi_ref[...]), c_ix, a_ix, b_ix, LANES - IX_HALF, IX_HALF)
    pieces = []
    for hd in range(N_IDX_HEADS):
        col = qi[:, (hd // 2) * LANES:(hd // 2 + 1) * LANES]
        keep = hi if hd % 2 else jnp.logical_not(hi)
        pieces.append(jnp.where(keep, col, 0.0))
    qis_ref[...] = jnp.concatenate(pieces, axis=1).astype(BF16)

    c_sm = jnp.where(hi, IDX_W_SCALE, c_ix)
    a_sm = jnp.where(hi, 0.0, a_ix)
    b_sm = jnp.where(hi, 0.0, b_ix)
    sm = _rope(_dot(h, wsm_ref[...]), c_sm, a_sm, b_sm, LANES - IX_HALF, IX_HALF)
    sm_ref[...] = sm
    kid_ref[...] = jnp.where(hi, pltpu.roll(sm, IDX_DIM, 1), sm).astype(BF16)

    gab = _dot(h, wgab_ref[...])
    sga_ref[...] = jax.nn.sigmoid(gab[:, :d])
    sgb_ref[...] = jax.nn.sigmoid(gab[:, d:])


def _inproj(x2d, g, tab, w, *, tm, tab_tiles):
    rows, d = x2d.shape
    nt = rows // tm
    qw = w["wq"].shape[1]
    kvw = w["wkv"].shape[1] // 2
    row_spec = lambda n: pl.BlockSpec((tm, n), lambda i: (i, 0))
    const = lambda a: pl.BlockSpec(a.shape, lambda i: (0,) * a.ndim)
    outs = [
        (d, F32),
        (qw, BF16),
        (kvw, F32), (kvw, F32),
        (kvw, BF16), (kvw, BF16),
        (N_IDX_HEADS * LANES, BF16),
        (LANES, F32),
        (LANES, BF16),
        (d, F32), (d, F32),
    ]
    return pl.pallas_call(
        _inproj_kernel,
        grid=(nt,),
        in_specs=[row_spec(d), const(g),
                  pl.BlockSpec((tm, LANES), lambda i: (i % tab_tiles, 0)),
                  const(w["wz"]), const(w["wq"]), const(w["wkv"]), const(w["wqi"]), const(w["wsm"]), const(w["wgab"])],
        out_specs=[row_spec(n) for n, _ in outs],
        out_shape=[jax.ShapeDtypeStruct((rows, n), dt) for n, dt in outs],
        compiler_params=pltpu.CompilerParams(dimension_semantics=("arbitrary",), vmem_limit_bytes=VMEM_LIMIT_BYTES),
        name="inproj",
    )(x2d, g, tab, w["wz"], w["wq"], w["wkv"], w["wqi"], w["wsm"], w["wgab"])


def _attn_kernel(nch_ref, qis_ref, sm_ref, kid_ref, q_ref, k_ref, v_ref, tri_ref, o_ref,
                 keys_scr, m_scr, l_scr, acc_scr, *, tq, tk, topk, causal, n_keys):
    i = pl.program_id(1)
    nch = nch_ref[i]
    sub = tri_ref.shape[0]
    row = i * tq + lax.broadcasted_iota(I32, (tq, 1), 0)
    if causal:
        lim = N_META + CHUNK * (jnp.right_shift(row - N_META, int(math.log2(CHUNK))) + 1)
        lim = jnp.minimum(lim, n_keys)
    else:
        lim = jnp.full((tq, 1), n_keys, I32)

    qstack = jnp.concatenate([qis_ref[:, hd * LANES:(hd + 1) * LANES] for hd in range(N_IDX_HEADS)], axis=0)
    wstack = jnp.concatenate([sm_ref[:, IDX_DIM + hd:IDX_DIM + hd + 1] for hd in range(N_IDX_HEADS)], axis=0)

    def lane_fold(x, op):
        acc = x[:, 0:LANES]
        for j in range(1, x.shape[1] // LANES):
            acc = op(acc, x[:, j * LANES:(j + 1) * LANES])
        return acc

    def sort_key(s):
        bits = pltpu.bitcast(s, I32)
        return bits ^ (jnp.right_shift(bits, 31) & 0x7FFFFFFF)

    def score_body(c, smax):
        for hf in range(tk // sub):
            off = pl.multiple_of(c * tk + hf * sub, sub)
            sc = _dot_nt(qstack, kid_ref[pl.ds(off, sub), :])
            sc = jnp.maximum(sc, 0.0) * wstack
            s = sc[0:tq]
            for hd in range(1, N_IDX_HEADS):
                s = s + sc[hd * tq:(hd + 1) * tq]
            s = jnp.where(s == 0.0, 0.0, s)
            ok = off + lax.broadcasted_iota(I32, (tq, sub), 1) < lim
            keys_scr[:, pl.ds(off, sub)] = jnp.where(ok, sort_key(s), INT_MIN)
            smax = jnp.maximum(smax, lane_fold(jnp.where(ok, s, -jnp.inf), jnp.maximum))
        return smax

    smax = lax.fori_loop(0, nch, score_body, jnp.full((tq, LANES), -jnp.inf, F32))
    kmax = sort_key(jnp.max(smax, axis=1, keepdims=True))

    def count_ge(cand):
        def body(c, acc):
            kb = keys_scr[:, pl.ds(pl.multiple_of(c * tk, tk), tk)]
            return acc + lane_fold(jnp.where(kb >= cand, 1.0, 0.0), jnp.add)
        acc = lax.fori_loop(0, nch, body, jnp.zeros((tq, LANES), F32))
        return jnp.sum(acc, axis=1, keepdims=True)

    take_all = lim <= topk
    lo0 = jnp.where(take_all, INT_MIN, INT_MIN + 1)
    hi0 = jnp.where(take_all, INT_MIN + 1, kmax + 1)

    def n_open(lo, hi):
        return jnp.max(jnp.where(hi == lo + 1, 0.0, 1.0))

    def search_cond(st):
        return (st[0] < 48) & (st[3] > 0.5)

    def search_body(st):
        it, lo, hi, _ = st
        mid = jnp.right_shift(lo, 1) + jnp.right_shift(hi, 1) + (lo & hi & 1)
        near = jnp.where(kmax > (1 << 26), kmax - (1 << 25), mid)
        pref = jnp.where(it == 0, 0, jnp.where(it == 1, 1, jnp.where(it == 2, near, mid)))
        cand = jnp.where((pref > lo) & (pref < hi), pref, mid)
        is_open = hi != lo + 1
        cnt = count_ge(cand)
        ge = cnt >= topk
        hit = cnt == topk
        lo_n = jnp.where(is_open & ge, cand, lo)
        hi_n = jnp.where(is_open & hit, cand + 1, jnp.where(is_open & jnp.logical_not(ge), cand, hi))
        return it + 1, lo_n, hi_n, n_open(lo_n, hi_n)

    _, thr, _, _ = lax.while_loop(search_cond, search_body, (jnp.int32(0), lo0, hi0, n_open(lo0, hi0)))
    n_tie = jnp.where(take_all, 0.0, topk - count_ge(thr + 1))

    m_scr[...] = jnp.full(m_scr.shape, NEG_BIG, F32)
    l_scr[...] = jnp.zeros(l_scr.shape, F32)
    acc_scr[...] = jnp.zeros(acc_scr.shape, F32)

    def flash_body(c, seen):
        off = pl.multiple_of(c * tk, tk)
        kb = keys_scr[:, pl.ds(off, tk)]
        eq = kb == thr
        eqf = jnp.where(eq, 1.0, 0.0)
        ranks = []
        for hf in range(tk // sub):
            e = eqf[:, hf * sub:(hf + 1) * sub]
            ranks.append(seen + _dot(e.astype(BF16), tri_ref[...]))
            seen = seen + jnp.sum(e, axis=1, keepdims=True)
        rank = jnp.concatenate(ranks, axis=1)
        sel = (kb > thr) | (eq & (rank < n_tie))
        maskf = jnp.where(sel, 0.0, NEG_BIG)
        mask4 = jnp.concatenate([maskf] * KV_REP, axis=0)
        for g in range(N_KV_HEADS):
            qg = jnp.concatenate([q_ref[:, (g * KV_REP + r) * HEAD_DIM:(g * KV_REP + r + 1) * HEAD_DIM]
                                  for r in range(KV_REP)], axis=0)
            kc = k_ref[pl.ds(off, tk), g * HEAD_DIM:(g + 1) * HEAD_DIM]
            vc = v_ref[pl.ds(off, tk), g * HEAD_DIM:(g + 1) * HEAD_DIM]
            s = _dot_nt(qg, kc) + mask4
            m_prev = m_scr[g]
            m_new = jnp.maximum(m_prev, jnp.max(s, axis=1, keepdims=True))
            alpha = jnp.exp2(m_prev - m_new)
            p = jnp.exp2(s - jnp.tile(m_new, (1, tk // LANES)))
            l_scr[g] = alpha * l_scr[g] + lane_fold(p, jnp.add)
            acc_scr[g] = alpha * acc_scr[g] + _dot(p.astype(BF16), vc)
            m_scr[g] = m_new
        return seen

    lax.fori_loop(0, nch, flash_body, jnp.zeros((tq, 1), F32))

    for g in range(N_KV_HEADS):
        og = acc_scr[g] / jnp.sum(l_scr[g], axis=1, keepdims=True)
        for r in range(KV_REP):
            hd = g * KV_REP + r
            o_ref[:, hd * HEAD_DIM:(hd + 1) * HEAD_DIM] = og[r * tq:(r + 1) * tq].astype(o_ref.dtype)


def _attention(qis, sm, kid, q, kb, vb, *, tq, tk, topk, causal, n_keys):
    bsz, sq, _ = q.shape
    sk = kb.shape[1]
    nq = sq // tq
    if causal:
        ends = np.minimum(N_META + CHUNK * ((np.arange(nq) * tq + tq - 1 - N_META) // CHUNK + 1), n_keys)
    else:
        ends = np.full((nq,), n_keys)
    nch = jnp.asarray(-(-ends // tk), I32)
    sub = TIE_BLOCK
    tri = jnp.asarray(np.triu(np.ones((sub, sub), np.float32), 1), BF16)
    qrow = lambda n: pl.BlockSpec((None, tq, n), lambda b, i, s: (b, i, 0))
    krow = lambda n: pl.BlockSpec((None, sk, n), lambda b, i, s: (b, 0, 0))
    kern = functools.partial(_attn_kernel, tq=tq, tk=tk, topk=topk, causal=causal, n_keys=n_keys)
    return pl.pallas_call(
        kern,
        grid_spec=pltpu.PrefetchScalarGridSpec(
            num_scalar_prefetch=1,
            grid=(bsz, nq),
            in_specs=[qrow(qis.shape[2]), qrow(sm.shape[2]), krow(kid.shape[2]), qrow(q.shape[2]),
                      krow(kb.shape[2]), krow(vb.shape[2]),
                      pl.BlockSpec((sub, sub), lambda b, i, s: (0, 0))],
            out_specs=qrow(q.shape[2]),
            scratch_shapes=[pltpu.VMEM((tq, sk), I32),
                            pltpu.VMEM((N_KV_HEADS, KV_REP * tq, LANES), F32),
                            pltpu.VMEM((N_KV_HEADS, KV_REP * tq, LANES), F32),
                            pltpu.VMEM((N_KV_HEADS, KV_REP * tq, HEAD_DIM), F32)]),
        out_shape=jax.ShapeDtypeStruct(q.shape, BF16),
        compiler_params=pltpu.CompilerParams(dimension_semantics=("arbitrary", "arbitrary"),
                                             vmem_limit_bytes=VMEM_LIMIT_BYTES),
        name="attn",
    )(nch, qis, sm, kid, q, kb, vb, tri)


def _mix_kernel(tb_ref, o_ref, z_ref, halo_ref, sga_ref, sgb_ref, x_ref, wpool_ref, ps_ref, wao_ref, wout_ref,
                x1_ref, zc_scr, *, tm):
    hs = halo_ref.shape[0]
    zc_scr[0:hs, :] = halo_ref[...]
    zc_scr[hs:hs + tm, :] = z_ref[...]
    avail = tb_ref[pl.program_id(0)] + lax.broadcasted_iota(I32, (tm, 1), 0) + 1
    gw = z_ref.shape[1] // len(POOL_WINDOWS)
    ys = []
    for g, win in enumerate(POOL_WINDOWS):
        cols = slice(g * gw, (g + 1) * gw)
        zt = zc_scr[hs:hs + tm, cols]
        s = zt
        for j in range(1, win):
            s = s + zc_scr[hs - j:hs - j + tm, cols]
        cnt = jnp.minimum(avail, win).astype(F32)
        dlt = s / cnt - zt
        ys.append(_dot(dlt.astype(BF16), wpool_ref[g]))
    a = jnp.concatenate(ys, axis=1) * ps_ref[...]
    b = _dot(o_ref[...], wao_ref[...])
    mrg = sga_ref[...] * a + sgb_ref[...] * b
    x1_ref[...] = x_ref[...] + _dot(mrg.astype(BF16), wout_ref[...])


def _mix(tbase, o, z, halo, sga, sgb, x, w, *, tm):
    rows, d = x.shape
    nt = rows // tm
    hs = halo.shape[1]
    row = lambda n: pl.BlockSpec((tm, n), lambda i, s: (i, 0))
    const = lambda a: pl.BlockSpec(a.shape, lambda i, s: (0,) * a.ndim)
    return pl.pallas_call(
        functools.partial(_mix_kernel, tm=tm),
        grid_spec=pltpu.PrefetchScalarGridSpec(
            num_scalar_prefetch=1,
            grid=(nt,),
            in_specs=[row(d), row(d), pl.BlockSpec((None, hs, d), lambda i, s: (i, 0, 0)), row(d), row(d), row(d),
                      const(w["wpool"]), const(w["pscale"]), const(w["wao"]), const(w["wout"])],
            out_specs=row(d),
            scratch_shapes=[pltpu.VMEM((hs + tm, d), F32)]),
        out_shape=jax.ShapeDtypeStruct((rows, d), F32),
        compiler_params=pltpu.CompilerParams(dimension_semantics=("arbitrary",), vmem_limit_bytes=VMEM_LIMIT_BYTES),
        name="mix",
    )(tbase, o, z, halo, sga, sgb, x, w["wpool"], w["pscale"], w["wao"], w["wout"])


ROUTE_GROUP_LANE0 = N_EXPERTS


def _moe_kernel(x_ref, gf_ref, gl_ref, wrh_ref, wrl_ref, br_ref, wg_ref, wu_ref, wd_ref, y_ref,
                h_scr, route_scr, acc_scr, *, ec):
    j = pl.program_id(1)
    tm = x_ref.shape[0]

    @pl.when(j == 0)
    def _():
        hf = _rms(x_ref[...], gf_ref[...])
        h_hi = hf.astype(BF16)
        h_lo = (hf - h_hi.astype(F32)).astype(BF16)
        h_scr[...] = h_hi
        logits = _dot(h_hi, wrh_ref[...]) + _dot(h_lo, wrh_ref[...]) + _dot(h_hi, wrl_ref[...]) + br_ref[...]
        lane = lax.broadcasted_iota(I32, (tm, LANES), 1)
        lanef = lane.astype(F32)
        ninf = -jnp.inf
        isg = (lane >= ROUTE_GROUP_LANE0) & (lane < ROUTE_GROUP_LANE0 + N_EXPERT_GROUPS)
        gl = jnp.where(isg, logits, ninf)
        gmax = jnp.max(gl, axis=1, keepdims=True)
        gsel = jnp.min(jnp.where(gl == gmax, lanef, 1e9), axis=1, keepdims=True)
        gprob = 1.0 / jnp.sum(jnp.exp(gl - gmax), axis=1, keepdims=True)
        e0 = (gsel - ROUTE_GROUP_LANE0) * EXPERTS_PER_GROUP
        el = jnp.where((lanef >= e0) & (lanef < e0 + EXPERTS_PER_GROUP), logits, ninf)
        v1 = jnp.max(el, axis=1, keepdims=True)
        i1 = jnp.min(jnp.where(el == v1, lanef, 1e9), axis=1, keepdims=True)
        el2 = jnp.where(lanef == i1, ninf, el)
        v2 = jnp.max(el2, axis=1, keepdims=True)
        i2 = jnp.min(jnp.where(el2 == v2, lanef, 1e9), axis=1, keepdims=True)
        e = jnp.exp(v2 - v1)
        w1 = (1.0 / (1.0 + e)) * gprob
        w2 = (e / (1.0 + e)) * gprob
        for n, val in enumerate((i1, i2, w1, w2)):
            route_scr[n] = jnp.broadcast_to(val, (tm, LANES))
        acc_scr[...] = jnp.zeros(acc_scr.shape, F32)

    h = h_scr[...]
    i1, i2, w1, w2 = route_scr[0], route_scr[1], route_scr[2], route_scr[3]
    pair = 2 * D_EXPERT
    for p in range(ec // 2):
        gate = _dot(h, wg_ref[:, p * pair:(p + 1) * pair])
        up = _dot(h, wu_ref[:, p * pair:(p + 1) * pair])
        act = gate * jax.nn.sigmoid(gate) * up
        halves = []
        for q in range(2):
            ef = jnp.full((tm, LANES), j * ec + 2 * p + q, I32).astype(F32)
            comb = jnp.where(i1 == ef, w1, 0.0) + jnp.where(i2 == ef, w2, 0.0)
            halves.append(act[:, q * D_EXPERT:(q + 1) * D_EXPERT] * comb)
        act = jnp.concatenate(halves, axis=1).astype(BF16)
        acc_scr[...] += _dot(act, wd_ref[p * pair:(p + 1) * pair, :])

    @pl.when(j == pl.num_programs(1) - 1)
    def _():
        y_ref[...] = _rms(x_ref[...] + acc_scr[...], gl_ref[...])


def _moe(x, w, *, tm, ec=4):
    rows, d = x.shape
    nt = rows // tm
    ne = N_EXPERTS // ec
    cw = ec * D_EXPERT
    const = lambda a: pl.BlockSpec(a.shape, lambda i, j: (0,) * a.ndim)
    return pl.pallas_call(
        functools.partial(_moe_kernel, ec=ec),
        grid=(nt, ne),
        in_specs=[pl.BlockSpec((tm, d), lambda i, j: (i, 0)), const(w["gffn"]), const(w["gfinal"]),
                  const(w["wr_hi"]), const(w["wr_lo"]), const(w["br"]),
                  pl.BlockSpec((d, cw), lambda i, j: (0, j)), pl.BlockSpec((d, cw), lambda i, j: (0, j)),
                  pl.BlockSpec((cw, d), lambda i, j: (j, 0))],
        out_specs=pl.BlockSpec((tm, d), lambda i, j: (i, 0)),
        out_shape=jax.ShapeDtypeStruct((rows, d), F32),
        scratch_shapes=[pltpu.VMEM((tm, d), BF16), pltpu.VMEM((4, tm, LANES), F32), pltpu.VMEM((tm, d), F32)],
        compiler_params=pltpu.CompilerParams(dimension_semantics=("arbitrary", "arbitrary"),
                                             vmem_limit_bytes=VMEM_LIMIT_BYTES),
        name="moe",
    )(x, w["gffn"], w["gfinal"], w["wr_hi"], w["wr_lo"], w["br"], w["wg"], w["wu"], w["wd"])


def _rope_table(pos):
    posf = pos.astype(F32)[:, None]

    def cs(half):
        inv = jnp.exp(-math.log(ROPE_THETA) * jnp.arange(half, dtype=F32) * (1.0 / half))
        ang = posf * inv[None, :]
        return jnp.cos(ang), jnp.sin(ang)

    cq, sq = cs(QK_HALF)
    ci, si = cs(IX_HALF)
    used = 2 * QK_HALF + 2 * IX_HALF
    return jnp.concatenate([cq, sq, ci, si, jnp.zeros((pos.shape[0], LANES - used), F32)], axis=1)


def _prep_weights(l, norm_mix_g, norm_ffn_g, norm_final_g, w_in, w_pool, pool_scale, w_attn_out, w_out,
                  w_router_group, b_router_group, w_router_expert, b_router_expert,
                  w_expert_gate, w_expert_up, w_expert_down):
    d = w_in.shape[1]
    pw = w_pool.shape[1] * w_pool.shape[2]
    qw = N_HEADS * HEAD_DIM
    kvw = N_KV_HEADS * HEAD_DIM
    widths = (pw, qw, kvw, kvw, N_IDX_HEADS * IDX_DIM, IDX_DIM, N_IDX_HEADS, d, d)
    offs = np.concatenate([[0], np.cumsum(widths)])
    wi = w_in[l]
    piece = lambda a, b: wi[:, offs[a]:offs[b]].astype(BF16)
    wsm = jnp.concatenate([wi[:, offs[5]:offs[7]], jnp.zeros((d, LANES - IDX_DIM - N_IDX_HEADS), F32)], axis=1)
    wr = jnp.concatenate([w_router_expert[l], w_router_group[l],
                          jnp.zeros((d, LANES - N_EXPERTS - N_EXPERT_GROUPS), F32)], axis=1)
    wr_hi = wr.astype(BF16)
    br = jnp.concatenate([b_router_expert[l], b_router_group[l],
                          jnp.zeros((LANES - N_EXPERTS - N_EXPERT_GROUPS,), F32)])[None, :]
    flat_e = lambda a: a.transpose(1, 0, 2).reshape(d, N_EXPERTS * D_EXPERT).astype(BF16)
    return dict(
        gmix=norm_mix_g[l][None, :], gffn=norm_ffn_g[l][None, :], gfinal=norm_final_g[None, :],
        wz=piece(0, 1), wq=piece(1, 2), wkv=piece(2, 4), wqi=piece(4, 5), wsm=wsm.astype(BF16), wgab=piece(7, 9),
        wpool=w_pool[l].astype(BF16), pscale=pool_scale[l][None, :],
        wao=w_attn_out[l].astype(BF16), wout=w_out[l].astype(BF16),
        wr_hi=wr_hi, wr_lo=(wr - wr_hi.astype(F32)).astype(BF16), br=br,
        wg=flat_e(w_expert_gate[l]), wu=flat_e(w_expert_up[l]),
        wd=w_expert_down[l].reshape(N_EXPERTS * D_EXPERT, d).astype(BF16),
    )


def _dup_lanes(ki):
    return jnp.concatenate([ki, ki], axis=-1)


def kernel(x_prompt, x_sample, cache_k, cache_v, cache_idx_k, state_pool, meta_tokens, norm_mix_g, norm_ffn_g,
           norm_final_g, w_in, w_pool, pool_scale, w_attn_out, w_out, w_router_group, b_router_group,
           w_router_expert, b_router_expert, w_expert_gate, w_expert_up, w_expert_down):
    bsz, seq, d = x_prompt.shape
    dbsz, dec, _ = x_sample.shape
    assert w_in.shape[0] == 1, "single-layer step"
    l = 0
    past = cache_k.shape[2] - N_META
    n = N_META + seq
    topk_p = min(IDX_TOPK, seq // 4)
    topk_s = min(IDX_TOPK, (past + dec) // 4)
    kvw = N_KV_HEADS * HEAD_DIM

    tq, tk = 128, 512
    sp = _round_up(n, tk)
    tm = _pick_tile(sp, 600)
    tm_moe = _pick_tile(sp, 1100)
    w = _prep_weights(l, norm_mix_g, norm_ffn_g, norm_final_g, w_in, w_pool, pool_scale, w_attn_out, w_out,
                      w_router_group, b_router_group, w_router_expert, b_router_expert,
                      w_expert_gate, w_expert_up, w_expert_down)

    xp = jnp.concatenate([jnp.broadcast_to(meta_tokens.astype(F32)[None], (bsz, N_META, d)), x_prompt,
                          jnp.zeros((bsz, sp - n, d), F32)], axis=1).reshape(bsz * sp, d)
    nt = sp // tm
    tab_p = _rope_table(jnp.arange(sp))
    z, q, k, v, kb, vb, qis, sm, kid, sga, sgb = _inproj(xp, w["gmix"], tab_p, w, tm=tm, tab_tiles=nt)
    b3 = lambda a: a.reshape(bsz, sp, a.shape[-1])
    o = _attention(b3(qis), b3(sm), b3(kid), b3(q), b3(kb), b3(vb),
                   tq=tq, tk=tk, topk=topk_p, causal=True, n_keys=n)
    hs = POOL_STATE + 1
    z4 = z.reshape(bsz, nt, tm, d)
    halo = jnp.concatenate([jnp.zeros((bsz, 1, hs, d), F32), z4[:, :-1, tm - hs:, :]], axis=1).reshape(bsz * nt, hs, d)
    tbase = jnp.asarray(np.tile(np.arange(nt) * tm, bsz), I32)
    x1 = _mix(tbase, o.reshape(bsz * sp, -1), z, halo, sga, sgb, xp, w, tm=tm)
    yp = _moe(x1, w, tm=tm_moe)
    y_prompt = yp.reshape(bsz, sp, d)[:, N_META:n]
    k_prompt = b3(k)[:, :n].reshape(1, bsz, n, N_KV_HEADS, HEAD_DIM)
    v_prompt = b3(v)[:, :n].reshape(1, bsz, n, N_KV_HEADS, HEAD_DIM)
    idx_k_prompt = b3(sm)[:, :n, :IDX_DIM][None]
    pool_prompt = b3(z)[:, n - POOL_STATE:n][None]

    rows_s = dbsz * dec
    tab_s = jnp.tile(_rope_table(N_META + past + jnp.arange(dec)), (dbsz, 1))
    xs = x_sample.reshape(rows_s, d)
    zs, qs, ks, vs, kbs, vbs, qiss, sms, kids, sgas, sgbs = _inproj(xs, w["gmix"], tab_s, w, tm=rows_s, tab_tiles=1)
    nk = N_META + past + dec
    sk = _round_up(nk, tk)
    s3 = lambda a: a.reshape(dbsz, dec, a.shape[-1])
    padk = lambda a: jnp.concatenate([a, jnp.zeros((dbsz, sk - nk, a.shape[-1]), a.dtype)], axis=1)
    k_all = padk(jnp.concatenate([cache_k[l].reshape(dbsz, N_META + past, kvw).astype(BF16), s3(kbs)], axis=1))
    v_all = padk(jnp.concatenate([cache_v[l].reshape(dbsz, N_META + past, kvw).astype(BF16), s3(vbs)], axis=1))
    ki_all = padk(jnp.concatenate([_dup_lanes(cache_idx_k[l]).astype(BF16), s3(kids)], axis=1))
    os_ = _attention(s3(qiss), s3(sms), ki_all, s3(qs), k_all, v_all,
                     tq=dec, tk=tk, topk=topk_s, causal=False, n_keys=nk)
    zcat = jnp.concatenate([state_pool[l].astype(F32), s3(zs)], axis=1)
    halo_s = jnp.concatenate([jnp.zeros((dbsz, 1, d), F32), zcat[:, :POOL_STATE]], axis=1)
    tbase_s = jnp.full((dbsz,), POOL_STATE, I32)
    x1s = _mix(tbase_s, os_.reshape(rows_s, -1), zs, halo_s, sgas, sgbs, xs, w, tm=dec)
    ys = _moe(x1s, w, tm=rows_s)
    y_sample = ys.reshape(dbsz, dec, d)
    k_sample = s3(ks).reshape(1, dbsz, dec, N_KV_HEADS, HEAD_DIM)
    v_sample = s3(vs).reshape(1, dbsz, dec, N_KV_HEADS, HEAD_DIM)
    idx_k_sample = s3(sms)[:, :, :IDX_DIM][None]
    pool_sample = zcat[:, zcat.shape[1] - POOL_STATE:][None]

    return (y_prompt, y_sample, k_prompt, v_prompt, idx_k_prompt, pool_prompt,
            k_sample, v_sample, idx_k_sample, pool_sample)
```

```python
import functools
import math

import numpy as np
import jax
import jax.numpy as jnp
from jax import lax
from jax.experimental import pallas as pl
from jax.experimental.pallas import tpu as pltpu

CHUNK = 64
N_META = 16
EPS = 1e-6
POOL_WINDOWS = (2, 4, 8, 16)
POOL_STATE = max(POOL_WINDOWS) - 1
N_HEADS = 8
N_KV_HEADS = 2
HEAD_DIM = 128
KV_REP = N_HEADS // N_KV_HEADS
N_IDX_HEADS = 4
IDX_DIM = 64
IDX_TOPK = 256
IDX_W_SCALE = (N_IDX_HEADS * IDX_DIM) ** -0.5
ATTN_SCALE = HEAD_DIM ** -0.5
Q_PRESCALE = ATTN_SCALE * math.log2(math.e)
ROPE_THETA = 500000.0
ROT_FRACTION = 4
N_EXPERT_GROUPS = 4
EXPERTS_PER_GROUP = 8
N_EXPERTS = N_EXPERT_GROUPS * EXPERTS_PER_GROUP
D_EXPERT = 128

LANES = 128
SUBLANES = 8
BF16_SUBLANES = 16
VMEM_LIMIT_BYTES = 56 * 1024 * 1024

QK_HALF = HEAD_DIM // ROT_FRACTION // 2
IX_HALF = IDX_DIM // ROT_FRACTION // 2
TAB_COS_QK = 0
TAB_SIN_QK = QK_HALF
TAB_COS_IX = 2 * QK_HALF
TAB_SIN_IX = 2 * QK_HALF + IX_HALF

ATTN_TQ = LANES
ATTN_TQ_PROMPT = 2 * LANES
ATTN_TK = 512
TIE_BLOCK = 256
INT_MIN = -(2 ** 31)
NEG_BIG = -1e30

F32 = jnp.float32
BF16 = jnp.bfloat16
I32 = jnp.int32


def _round_up(x, m):
    return -(-x // m) * m


def _pick_tile(rows, cap):
    return max(t for t in range(BF16_SUBLANES, cap + 1, BF16_SUBLANES) if rows % t == 0)


def _rms(x, g):
    return x * lax.rsqrt(jnp.mean(x * x, axis=-1, keepdims=True) + EPS) * g


def _dot(a, b):
    return jnp.dot(a, b, preferred_element_type=F32)


def _dot_nt(a, b):
    return lax.dot_general(a, b, (((1,), (1,)), ((), ())), preferred_element_type=F32)


def _rope(y, c, a, b, shift_a, shift_b):
    outs = []
    for j in range(y.shape[1] // LANES):
        ys = y[:, j * LANES:(j + 1) * LANES]
        outs.append(ys * c + pltpu.roll(ys, shift_a, 1) * a + pltpu.roll(ys, shift_b, 1) * b)
    return outs[0] if len(outs) == 1 else jnp.concatenate(outs, axis=1)


def _inproj_kernel(x_ref, g_ref, tab_ref, wz_ref, wq_ref, wkv_ref, wqi_ref, wsm_ref, wgab_ref,
                   z_ref, q_ref, k_ref, v_ref, kb_ref, vt_ref, qis_ref, sm_ref, kid_ref, sga_ref, sgb_ref):
    h = _rms(x_ref[...], g_ref[...]).astype(BF16)
    tm = h.shape[0]
    d = x_ref.shape[1]
    lane = lax.broadcasted_iota(I32, (tm, LANES), 1)
    t = tab_ref[...]

    def rl(s):
        return pltpu.roll(t, s % LANES, 1) if s % LANES else t

    c_qk = jnp.where(lane < QK_HALF, rl(-TAB_COS_QK), jnp.where(lane < 2 * QK_HALF, rl(QK_HALF - TAB_COS_QK), 1.0))
    a_qk = jnp.where(lane < QK_HALF, -rl(-TAB_SIN_QK), 0.0)
    b_qk = jnp.where((lane >= QK_HALF) & (lane < 2 * QK_HALF), rl(QK_HALF - TAB_SIN_QK), 0.0)
    l64 = lane & (IDX_DIM - 1)
    hi = lane >= IDX_DIM
    cos_lo = jnp.where(hi, rl(IDX_DIM - TAB_COS_IX), rl(-TAB_COS_IX))
    cos_hi = jnp.where(hi, rl(IDX_DIM + IX_HALF - TAB_COS_IX), rl(IX_HALF - TAB_COS_IX))
    sin_lo = jnp.where(hi, rl(IDX_DIM - TAB_SIN_IX), rl(-TAB_SIN_IX))
    sin_hi = jnp.where(hi, rl(IDX_DIM + IX_HALF - TAB_SIN_IX), rl(IX_HALF - TAB_SIN_IX))
    c_ix = jnp.where(l64 < IX_HALF, cos_lo, jnp.where(l64 < 2 * IX_HALF, cos_hi, 1.0))
    a_ix = jnp.where(l64 < IX_HALF, -sin_lo, 0.0)
    b_ix = jnp.where((l64 >= IX_HALF) & (l64 < 2 * IX_HALF), sin_hi, 0.0)

    z_ref[...] = _dot(h, wz_ref[...])

    q = _rope(_dot(h, wq_ref[...]), c_qk, a_qk, b_qk, LANES - QK_HALF, QK_HALF)
    q_ref[...] = (q * Q_PRESCALE).astype(BF16)

    kv = _dot(h, wkv_ref[...])
    kvw = kv.shape[1] // 2
    k = _rope(kv[:, :kvw], c_qk, a_qk, b_qk, LANES - QK_HALF, QK_HALF)
    v = kv[:, kvw:]
    k_ref[...] = k
    v_ref[...] = v
    kb_ref[...] = k.astype(BF16)
    vt_ref[...] = v.T.astype(BF16)

    qi = _rope(_dot(h, wqi_ref[...]), c_ix, a_ix, b_ix, LANES - IX_HALF, IX_HALF)
    pieces = []
    for hd in range(N_IDX_HEADS):
        col = qi[:, (hd // 2) * LANES:(hd // 2 + 1) * LANES]
        keep = hi if hd % 2 else jnp.logical_not(hi)
        pieces.append(jnp.where(keep, col, 0.0))
    qis_ref[...] = jnp.concatenate(pieces, axis=1).astype(BF16)

    c_sm = jnp.where(hi, IDX_W_SCALE, c_ix)
    a_sm = jnp.where(hi, 0.0, a_ix)
    b_sm = jnp.where(hi, 0.0, b_ix)
    sm = _rope(_dot(h, wsm_ref[...]), c_sm, a_sm, b_sm, LANES - IX_HALF, IX_HALF)
    sm_ref[...] = sm
    kid_ref[...] = jnp.where(hi, pltpu.roll(sm, IDX_DIM, 1), sm).astype(BF16)

    gab = _dot(h, wgab_ref[...])
    sga_ref[...] = jax.nn.sigmoid(gab[:, :d])
    sgb_ref[...] = jax.nn.sigmoid(gab[:, d:])


def _inproj(x2d, g, tab, w, *, tm, tab_tiles):
    rows, d = x2d.shape
    nt = rows // tm
    qw = w["wq"].shape[1]
    kvw = w["wkv"].shape[1] // 2
    row_spec = lambda n: pl.BlockSpec((tm, n), lambda i: (i, 0))
    const = lambda a: pl.BlockSpec(a.shape, lambda i: (0,) * a.ndim)
    outs = [
        (d, F32),
        (qw, BF16),
        (kvw, F32), (kvw, F32),
        (kvw, BF16),
        None,
        (N_IDX_HEADS * LANES, BF16),
        (LANES, F32),
        (LANES, BF16),
        (d, F32), (d, F32),
    ]
    out_specs = [pl.BlockSpec((kvw, tm), lambda i: (0, i)) if o is None else row_spec(o[0]) for o in outs]
    out_shape = [jax.ShapeDtypeStruct((kvw, rows), BF16) if o is None else jax.ShapeDtypeStruct((rows, o[0]), o[1])
                 for o in outs]
    return pl.pallas_call(
        _inproj_kernel,
        grid=(nt,),
        in_specs=[row_spec(d), const(g),
                  pl.BlockSpec((tm, LANES), lambda i: (i % tab_tiles, 0)),
                  const(w["wz"]), const(w["wq"]), const(w["wkv"]), const(w["wqi"]), const(w["wsm"]), const(w["wgab"])],
        out_specs=out_specs,
        out_shape=out_shape,
        compiler_params=pltpu.CompilerParams(dimension_semantics=("arbitrary",), vmem_limit_bytes=VMEM_LIMIT_BYTES),
        name="inproj",
    )(x2d, g, tab, w["wz"], w["wq"], w["wkv"], w["wqi"], w["wsm"], w["wgab"])


def _row_fold(x, op):
    parts = [x[j * SUBLANES:(j + 1) * SUBLANES] for j in range(x.shape[0] // SUBLANES)]
    while len(parts) > 1:
        parts = [op(parts[j], parts[j + 1]) for j in range(0, len(parts) - 1, 2)] + parts[len(parts) & ~1:]
    return parts[0]


def _attn_kernel(nch_ref, qis_ref, sm_ref, kid_ref, q_ref, k_ref, vt_ref, tri_ref, o_ref,
                 keys_scr, acc_scr, s_scr, *, tq, tk, topk, causal, n_keys):
    i = pl.program_id(1)
    nch = nch_ref[i]
    sub = tri_ref.shape[0]
    nq4 = KV_REP * tq
    qpos = i * tq + lax.broadcasted_iota(I32, (1, tq), 1)
    if causal:
        lim = N_META + CHUNK * (jnp.right_shift(qpos - N_META, int(math.log2(CHUNK))) + 1)
        lim = jnp.minimum(lim, n_keys)
    else:
        lim = jnp.full((1, tq), n_keys, I32)

    def sort_key(s):
        bits = pltpu.bitcast(s, I32)
        return bits ^ (jnp.right_shift(bits, 31) & 0x7FFFFFFF)

    qstack = jnp.concatenate([qis_ref[:, hd * LANES:(hd + 1) * LANES] for hd in range(N_IDX_HEADS)], axis=0)
    smt = sm_ref[...].T
    wlane = jnp.concatenate([smt[IDX_DIM + hd:IDX_DIM + hd + 1, :] for hd in range(N_IDX_HEADS)], axis=1)

    def score_body(c, smax):
        off = pl.multiple_of(c * tk, tk)
        sc = _dot_nt(kid_ref[pl.ds(off, tk), :], qstack)
        sc = jnp.maximum(sc, 0.0) * wlane
        s = sc[:, 0:tq]
        for hd in range(1, N_IDX_HEADS):
            s = s + sc[:, hd * tq:(hd + 1) * tq]
        s = jnp.where(s == 0.0, 0.0, s)
        ok = off + lax.broadcasted_iota(I32, (tk, tq), 0) < lim
        keys_scr[pl.ds(off, tk), :] = jnp.where(ok, sort_key(s), INT_MIN)
        return jnp.maximum(smax, _row_fold(jnp.where(ok, s, -jnp.inf), jnp.maximum))

    smax = lax.fori_loop(0, nch, score_body, jnp.full((SUBLANES, tq), -jnp.inf, F32))
    kmax = sort_key(jnp.max(smax, axis=0, keepdims=True))

    def count_ge(cand):
        def body(c, acc):
            kb = keys_scr[pl.ds(pl.multiple_of(c * tk, tk), tk), :]
            return acc + _row_fold(jnp.where(kb >= cand, 1.0, 0.0), jnp.add)
        acc = lax.fori_loop(0, nch, body, jnp.zeros((SUBLANES, tq), F32))
        return jnp.sum(acc, axis=0, keepdims=True)

    take_all = lim <= topk
    lo0 = jnp.where(take_all, INT_MIN, INT_MIN + 1)
    hi0 = jnp.where(take_all, INT_MIN + 1, kmax + 1)

    def n_open(lo, hi):
        return jnp.max(jnp.where(hi == lo + 1, 0.0, 1.0))

    def search_cond(st):
        return (st[0] < 48) & (st[3] > 0.5)

    def search_body(st):
        it, lo, hi, _ = st
        mid = jnp.right_shift(lo, 1) + jnp.right_shift(hi, 1) + (lo & hi & 1)
        near = jnp.where(kmax > (1 << 26), kmax - (1 << 25), mid)
        pref = jnp.where(it == 0, 0, jnp.where(it == 1, 1, jnp.where(it == 2, near, mid)))
        cand = jnp.where((pref > lo) & (pref < hi), pref, mid)
        is_open = hi != lo + 1
        cnt = count_ge(cand)
        ge = cnt >= topk
        hit = cnt == topk
        lo_n = jnp.where(is_open & ge, cand, lo)
        hi_n = jnp.where(is_open & hit, cand + 1, jnp.where(is_open & jnp.logical_not(ge), cand, hi))
        return it + 1, lo_n, hi_n, n_open(lo_n, hi_n)

    _, thr, _, _ = lax.while_loop(search_cond, search_body, (jnp.int32(0), lo0, hi0, n_open(lo0, hi0)))
    n_tie = jnp.where(take_all, 0.0, topk - count_ge(thr + 1))

    acc_scr[...] = jnp.zeros(acc_scr.shape, F32)
    qgs = [jnp.concatenate([q_ref[:, (g * KV_REP + r) * HEAD_DIM:(g * KV_REP + r + 1) * HEAD_DIM]
                            for r in range(KV_REP)], axis=0) for g in range(N_KV_HEADS)]

    ones_rows = jnp.ones((BF16_SUBLANES, tk), BF16)

    def logits(c, slot, seen):
        off = pl.multiple_of(c * tk, tk)
        kb = keys_scr[pl.ds(off, tk), :]
        eq = kb == thr
        eqf = jnp.where(eq, 1.0, 0.0)
        ranks = []
        for hf in range(tk // sub):
            e = eqf[hf * sub:(hf + 1) * sub]
            ranks.append(seen + _dot(tri_ref[...], e.astype(BF16)))
            seen = seen + jnp.sum(_row_fold(e, jnp.add), axis=0, keepdims=True)
        rank = jnp.concatenate(ranks, axis=0)
        sel = (kb > thr) | (eq & (rank < n_tie))
        maskf = jnp.where(sel, 0.0, NEG_BIG)
        mask4 = jnp.concatenate([maskf] * KV_REP, axis=1)
        mcs = []
        for g in range(N_KV_HEADS):
            kc = k_ref[pl.ds(off, tk), g * HEAD_DIM:(g + 1) * HEAD_DIM]
            s = _dot_nt(kc, qgs[g]) + mask4
            s_scr[slot, g] = s
            mcs.append(jnp.max(_row_fold(s, jnp.maximum), axis=0, keepdims=True))
        return seen, tuple(mcs)

    def update(c, slot, ms, mcs):
        off = pl.multiple_of(c * tk, tk)
        ms_n = []
        for g in range(N_KV_HEADS):
            vtc = vt_ref[g * HEAD_DIM:(g + 1) * HEAD_DIM, pl.ds(off, tk)]
            m_new = jnp.maximum(ms[g], mcs[g])
            alpha = jnp.exp2(ms[g] - m_new)
            p = jnp.exp2(s_scr[slot, g] - m_new).astype(BF16)
            acc_scr[g] = alpha * acc_scr[g] + _dot(jnp.concatenate([vtc, ones_rows], axis=0), p)
            ms_n.append(m_new)
        return tuple(ms_n)

    def flash_body(c, st):
        seen, ms, mcs = st
        slot = c & 1
        ms = update(c, slot, ms, mcs)
        seen, mcs = logits(c + 1, 1 - slot, seen)
        return seen, ms, mcs

    m0 = tuple(jnp.full((1, nq4), NEG_BIG, F32) for _ in range(N_KV_HEADS))
    seen0, mc0 = logits(0, 0, jnp.zeros((1, tq), F32))
    _, ms, mcs = lax.fori_loop(0, nch - 1, flash_body, (seen0, m0, mc0))
    update(nch - 1, (nch - 1) & 1, ms, mcs)

    for g in range(N_KV_HEADS):
        og = acc_scr[g, 0:HEAD_DIM] / acc_scr[g, HEAD_DIM:HEAD_DIM + 1]
        for r in range(KV_REP):
            hd = g * KV_REP + r
            o_ref[:, hd * HEAD_DIM:(hd + 1) * HEAD_DIM] = og[:, r * tq:(r + 1) * tq].T.astype(o_ref.dtype)


def _attention(qis, sm, kid, q, kb, vt, *, tq, topk, causal, n_keys):
    tk, sub = ATTN_TK, TIE_BLOCK
    bsz, sq, _ = q.shape
    sk = kb.shape[1]
    assert sq % tq == 0 and sk % tk == 0
    nq = sq // tq
    if causal:
        ends = np.minimum(N_META + CHUNK * ((np.arange(nq) * tq + tq - 1 - N_META) // CHUNK + 1), n_keys)
    else:
        ends = np.full((nq,), n_keys)
    nch = jnp.asarray(-(-ends // tk), I32)
    tri = jnp.asarray(np.tril(np.ones((sub, sub), np.float32), -1), BF16)
    qrow = lambda n: pl.BlockSpec((None, tq, n), lambda b, i, s: (b, i, 0))
    krow = lambda n: pl.BlockSpec((None, sk, n), lambda b, i, s: (b, 0, 0))
    kern = functools.partial(_attn_kernel, tq=tq, tk=tk, topk=topk, causal=causal, n_keys=n_keys)
    return pl.pallas_call(
        kern,
        grid_spec=pltpu.PrefetchScalarGridSpec(
            num_scalar_prefetch=1,
            grid=(bsz, nq),
            in_specs=[qrow(qis.shape[2]), qrow(sm.shape[2]), krow(kid.shape[2]), qrow(q.shape[2]),
                      krow(kb.shape[2]),
                      pl.BlockSpec((vt.shape[0], sk), lambda b, i, s: (0, b)),
                      pl.BlockSpec((sub, sub), lambda b, i, s: (0, 0))],
            out_specs=qrow(q.shape[2]),
            scratch_shapes=[pltpu.VMEM((sk, tq), I32),
                            pltpu.VMEM((N_KV_HEADS, HEAD_DIM + BF16_SUBLANES, KV_REP * tq), F32),
                            pltpu.VMEM((2, N_KV_HEADS, tk, KV_REP * tq), F32)]),
        out_shape=jax.ShapeDtypeStruct(q.shape, BF16),
        compiler_params=pltpu.CompilerParams(dimension_semantics=("arbitrary", "arbitrary"),
                                             vmem_limit_bytes=VMEM_LIMIT_BYTES),
        name="attn",
    )(nch, qis, sm, kid, q, kb, vt, tri)


def _mix_kernel(tb_ref, o_ref, z_ref, halo_ref, sga_ref, sgb_ref, x_ref, wpool_ref, ps_ref, wao_ref, wout_ref,
                x1_ref, zc_scr, *, tm):
    hs = halo_ref.shape[0]
    zc_scr[0:hs, :] = halo_ref[...]
    zc_scr[hs:hs + tm, :] = z_ref[...]
    avail = tb_ref[pl.program_id(0)] + lax.broadcasted_iota(I32, (tm, 1), 0) + 1
    gw = z_ref.shape[1] // len(POOL_WINDOWS)
    ys = []
    for g, win in enumerate(POOL_WINDOWS):
        cols = slice(g * gw, (g + 1) * gw)
        zt = zc_scr[hs:hs + tm, cols]
        s = zt
        for j in range(1, win):
            s = s + zc_scr[hs - j:hs - j + tm, cols]
        cnt = jnp.minimum(avail, win).astype(F32)
        dlt = s / cnt - zt
        ys.append(_dot(dlt.astype(BF16), wpool_ref[g]))
    a = jnp.concatenate(ys, axis=1) * ps_ref[...]
    b = _dot(o_ref[...], wao_ref[...])
    mrg = sga_ref[...] * a + sgb_ref[...] * b
    x1_ref[...] = x_ref[...] + _dot(mrg.astype(BF16), wout_ref[...])


def _mix(tbase, o, z, halo, sga, sgb, x, w, *, tm):
    rows, d = x.shape
    nt = rows // tm
    hs = halo.shape[1]
    row = lambda n: pl.BlockSpec((tm, n), lambda i, s: (i, 0))
    const = lambda a: pl.BlockSpec(a.shape, lambda i, s: (0,) * a.ndim)
    return pl.pallas_call(
        functools.partial(_mix_kernel, tm=tm),
        grid_spec=pltpu.PrefetchScalarGridSpec(
            num_scalar_prefetch=1,
            grid=(nt,),
            in_specs=[row(d), row(d), pl.BlockSpec((None, hs, d), lambda i, s: (i, 0, 0)), row(d), row(d), row(d),
                      const(w["wpool"]), const(w["pscale"]), const(w["wao"]), const(w["wout"])],
            out_specs=row(d),
            scratch_shapes=[pltpu.VMEM((hs + tm, d), F32)]),
        out_shape=jax.ShapeDtypeStruct((rows, d), F32),
        compiler_params=pltpu.CompilerParams(dimension_semantics=("arbitrary",), vmem_limit_bytes=VMEM_LIMIT_BYTES),
        name="mix",
    )(tbase, o, z, halo, sga, sgb, x, w["wpool"], w["pscale"], w["wao"], w["wout"])


ROUTE_GROUP_LANE0 = N_EXPERTS


def _moe_kernel(x_ref, gf_ref, gl_ref, wrh_ref, wrl_ref, br_ref, wg_ref, wu_ref, wd_ref, y_ref,
                h_scr, route_scr, acc_scr, *, ec):
    j = pl.program_id(1)
    tm = x_ref.shape[0]

    @pl.when(j == 0)
    def _():
        hf = _rms(x_ref[...], gf_ref[...])
        h_hi = hf.astype(BF16)
        h_lo = (hf - h_hi.astype(F32)).astype(BF16)
        h_scr[...] = h_hi
        logits = _dot(h_hi, wrh_ref[...]) + _dot(h_lo, wrh_ref[...]) + _dot(h_hi, wrl_ref[...]) + br_ref[...]
        lane = lax.broadcasted_iota(I32, (tm, LANES), 1)
        lanef = lane.astype(F32)
        ninf = -jnp.inf
        isg = (lane >= ROUTE_GROUP_LANE0) & (lane < ROUTE_GROUP_LANE0 + N_EXPERT_GROUPS)
        gl = jnp.where(isg, logits, ninf)
        gmax = jnp.max(gl, axis=1, keepdims=True)
        gsel = jnp.min(jnp.where(gl == gmax, lanef, 1e9), axis=1, keepdims=True)
        gprob = 1.0 / jnp.sum(jnp.exp(gl - gmax), axis=1, keepdims=True)
        e0 = (gsel - ROUTE_GROUP_LANE0) * EXPERTS_PER_GROUP
        el = jnp.where((lanef >= e0) & (lanef < e0 + EXPERTS_PER_GROUP), logits, ninf)
        v1 = jnp.max(el, axis=1, keepdims=True)
        i1 = jnp.min(jnp.where(el == v1, lanef, 1e9), axis=1, keepdims=True)
        el2 = jnp.where(lanef == i1, ninf, el)
        v2 = jnp.max(el2, axis=1, keepdims=True)
        i2 = jnp.min(jnp.where(el2 == v2, lanef, 1e9), axis=1, keepdims=True)
        e = jnp.exp(v2 - v1)
        w1 = (1.0 / (1.0 + e)) * gprob
        w2 = (e / (1.0 + e)) * gprob
        for n, val in enumerate((i1, i2, w1, w2)):
            route_scr[n] = jnp.broadcast_to(val, (tm, LANES))
        acc_scr[...] = jnp.zeros(acc_scr.shape, F32)

    h = h_scr[...]
    i1, i2, w1, w2 = route_scr[0], route_scr[1], route_scr[2], route_scr[3]
    pair = 2 * D_EXPERT
    for p in range(ec // 2):
        gate = _dot(h, wg_ref[:, p * pair:(p + 1) * pair])
        up = _dot(h, wu_ref[:, p * pair:(p + 1) * pair])
        act = gate * jax.nn.sigmoid(gate) * up
        halves = []
        for q in range(2):
            ef = jnp.full((tm, LANES), j * ec + 2 * p + q, I32).astype(F32)
            comb = jnp.where(i1 == ef, w1, 0.0) + jnp.where(i2 == ef, w2, 0.0)
            halves.append(act[:, q * D_EXPERT:(q + 1) * D_EXPERT] * comb)
        act = jnp.concatenate(halves, axis=1).astype(BF16)
        acc_scr[...] += _dot(act, wd_ref[p * pair:(p + 1) * pair, :])

    @pl.when(j == pl.num_programs(1) - 1)
    def _():
        y_ref[...] = _rms(x_ref[...] + acc_scr[...], gl_ref[...])


def _moe(x, w, *, tm, ec=4):
    rows, d = x.shape
    nt = rows // tm
    ne = N_EXPERTS // ec
    cw = ec * D_EXPERT
    const = lambda a: pl.BlockSpec(a.shape, lambda i, j: (0,) * a.ndim)
    return pl.pallas_call(
        functools.partial(_moe_kernel, ec=ec),
        grid=(nt, ne),
        in_specs=[pl.BlockSpec((tm, d), lambda i, j: (i, 0)), const(w["gffn"]), const(w["gfinal"]),
                  const(w["wr_hi"]), const(w["wr_lo"]), const(w["br"]),
                  pl.BlockSpec((d, cw), lambda i, j: (0, j)), pl.BlockSpec((d, cw), lambda i, j: (0, j)),
                  pl.BlockSpec((cw, d), lambda i, j: (j, 0))],
        out_specs=pl.BlockSpec((tm, d), lambda i, j: (i, 0)),
        out_shape=jax.ShapeDtypeStruct((rows, d), F32),
        scratch_shapes=[pltpu.VMEM((tm, d), BF16), pltpu.VMEM((4, tm, LANES), F32), pltpu.VMEM((tm, d), F32)],
        compiler_params=pltpu.CompilerParams(dimension_semantics=("arbitrary", "arbitrary"),
                                             vmem_limit_bytes=VMEM_LIMIT_BYTES),
        name="moe",
    )(x, w["gffn"], w["gfinal"], w["wr_hi"], w["wr_lo"], w["br"], w["wg"], w["wu"], w["wd"])


def _rope_table(pos):
    posf = np.asarray(pos, np.float32)[:, None]

    def cs(half):
        inv = np.exp(np.float32(-math.log(ROPE_THETA)) * np.arange(half, dtype=np.float32) * np.float32(1.0 / half))
        ang = posf * inv[None, :]
        return np.cos(ang), np.sin(ang)

    cq, sq = cs(QK_HALF)
    ci, si = cs(IX_HALF)
    used = 2 * QK_HALF + 2 * IX_HALF
    tab = np.concatenate([cq, sq, ci, si, np.zeros((posf.shape[0], LANES - used), np.float32)], axis=1)
    return jnp.asarray(tab, F32)


def _prep_weights(l, norm_mix_g, norm_ffn_g, norm_final_g, w_in, w_pool, pool_scale, w_attn_out, w_out,
                  w_router_group, b_router_group, w_router_expert, b_router_expert,
                  w_expert_gate, w_expert_up, w_expert_down):
    d = w_in.shape[1]
    pw = w_pool.shape[1] * w_pool.shape[2]
    qw = N_HEADS * HEAD_DIM
    kvw = N_KV_HEADS * HEAD_DIM
    widths = (pw, qw, kvw, kvw, N_IDX_HEADS * IDX_DIM, IDX_DIM, N_IDX_HEADS, d, d)
    offs = np.concatenate([[0], np.cumsum(widths)])
    wi = w_in[l]
    piece = lambda a, b: wi[:, offs[a]:offs[b]].astype(BF16)
    wsm = jnp.concatenate([wi[:, offs[5]:offs[7]], jnp.zeros((d, LANES - IDX_DIM - N_IDX_HEADS), F32)], axis=1)
    wr = jnp.concatenate([w_router_expert[l], w_router_group[l],
                          jnp.zeros((d, LANES - N_EXPERTS - N_EXPERT_GROUPS), F32)], axis=1)
    wr_hi = wr.astype(BF16)
    br = jnp.concatenate([b_router_expert[l], b_router_group[l],
                          jnp.zeros((LANES - N_EXPERTS - N_EXPERT_GROUPS,), F32)])[None, :]
    flat_e = lambda a: a.transpose(1, 0, 2).reshape(d, N_EXPERTS * D_EXPERT).astype(BF16)
    return dict(
        gmix=norm_mix_g[l][None, :], gffn=norm_ffn_g[l][None, :], gfinal=norm_final_g[None, :],
        wz=piece(0, 1), wq=piece(1, 2), wkv=piece(2, 4), wqi=piece(4, 5), wsm=wsm.astype(BF16), wgab=piece(7, 9),
        wpool=w_pool[l].astype(BF16), pscale=pool_scale[l][None, :],
        wao=w_attn_out[l].astype(BF16), wout=w_out[l].astype(BF16),
        wr_hi=wr_hi, wr_lo=(wr - wr_hi.astype(F32)).astype(BF16), br=br,
        wg=flat_e(w_expert_gate[l]), wu=flat_e(w_expert_up[l]),
        wd=w_expert_down[l].reshape(N_EXPERTS * D_EXPERT, d).astype(BF16),
    )


def _dup_lanes(ki):
    return jnp.concatenate([ki, ki], axis=-1)


def kernel(x_prompt, x_sample, cache_k, cache_v, cache_idx_k, state_pool, meta_tokens, norm_mix_g, norm_ffn_g,
           norm_final_g, w_in, w_pool, pool_scale, w_attn_out, w_out, w_router_group, b_router_group,
           w_router_expert, b_router_expert, w_expert_gate, w_expert_up, w_expert_down):
    bsz, seq, d = x_prompt.shape
    dbsz, dec, _ = x_sample.shape
    assert w_in.shape[0] == 1, "single-layer step"
    l = 0
    past = cache_k.shape[2] - N_META
    n = N_META + seq
    topk_p = min(IDX_TOPK, seq // 4)
    topk_s = min(IDX_TOPK, (past + dec) // 4)
    kvw = N_KV_HEADS * HEAD_DIM

    tk = ATTN_TK
    sp = _round_up(n, tk)
    tm = _pick_tile(sp, 512)
    tm_moe = _pick_tile(sp, 1100)
    w = _prep_weights(l, norm_mix_g, norm_ffn_g, norm_final_g, w_in, w_pool, pool_scale, w_attn_out, w_out,
                      w_router_group, b_router_group, w_router_expert, b_router_expert,
                      w_expert_gate, w_expert_up, w_expert_down)

    xp = jnp.concatenate([jnp.broadcast_to(meta_tokens.astype(F32)[None], (bsz, N_META, d)), x_prompt,
                          jnp.zeros((bsz, sp - n, d), F32)], axis=1).reshape(bsz * sp, d)
    nt = sp // tm
    tab_p = _rope_table(np.arange(sp))
    z, q, k, v, kb, vt, qis, sm, kid, sga, sgb = _inproj(xp, w["gmix"], tab_p, w, tm=tm, tab_tiles=nt)
    b3 = lambda a: a.reshape(bsz, sp, a.shape[-1])
    o = _attention(b3(qis), b3(sm), b3(kid), b3(q), b3(kb), vt, tq=ATTN_TQ_PROMPT, topk=topk_p, causal=True, n_keys=n)
    hs = POOL_STATE + 1
    z4 = z.reshape(bsz, nt, tm, d)
    halo = jnp.concatenate([jnp.zeros((bsz, 1, hs, d), F32), z4[:, :-1, tm - hs:, :]], axis=1).reshape(bsz * nt, hs, d)
    tbase = jnp.asarray(np.tile(np.arange(nt) * tm, bsz), I32)
    x1 = _mix(tbase, o.reshape(bsz * sp, -1), z, halo, sga, sgb, xp, w, tm=tm)
    yp = _moe(x1, w, tm=tm_moe)
    y_prompt = yp.reshape(bsz, sp, d)[:, N_META:n]
    k_prompt = b3(k)[:, :n].reshape(1, bsz, n, N_KV_HEADS, HEAD_DIM)
    v_prompt = b3(v)[:, :n].reshape(1, bsz, n, N_KV_HEADS, HEAD_DIM)
    idx_k_prompt = b3(sm)[:, :n, :IDX_DIM][None]
    pool_prompt = b3(z)[:, n - POOL_STATE:n][None]

    rows_s = dbsz * dec
    tab_s = _rope_table(np.tile(N_META + past + np.arange(dec), dbsz))
    xs = x_sample.reshape(rows_s, d)
    zs, qs, ks, vs, kbs, _, qiss, sms, kids, sgas, sgbs = _inproj(xs, w["gmix"], tab_s, w, tm=rows_s, tab_tiles=1)
    nk = N_META + past + dec
    sk = _round_up(nk, tk)
    sq_s = _round_up(dec, ATTN_TQ)
    s3 = lambda a: a.reshape(dbsz, dec, a.shape[-1])
    padr = lambda a, rows: jnp.concatenate([a, jnp.zeros((dbsz, rows - a.shape[1], a.shape[-1]), a.dtype)], axis=1)
    k_all = padr(jnp.concatenate([cache_k[l].reshape(dbsz, N_META + past, kvw).astype(BF16), s3(kbs)], axis=1), sk)
    v_all = padr(jnp.concatenate([cache_v[l].reshape(dbsz, N_META + past, kvw).astype(BF16),
                                  s3(vs).astype(BF16)], axis=1), sk)
    vt_all = v_all.transpose(2, 0, 1).reshape(kvw, dbsz * sk)
    ki_all = padr(jnp.concatenate([_dup_lanes(cache_idx_k[l]).astype(BF16), s3(kids)], axis=1), sk)
    os_ = _attention(padr(s3(qiss), sq_s), padr(s3(sms), sq_s), ki_all, padr(s3(qs), sq_s), k_all, vt_all,
                     tq=ATTN_TQ, topk=topk_s, causal=False, n_keys=nk)[:, :dec]
    zcat = jnp.concatenate([state_pool[l].astype(F32), s3(zs)], axis=1)
    halo_s = jnp.concatenate([jnp.zeros((dbsz, 1, d), F32), zcat[:, :POOL_STATE]], axis=1)
    tbase_s = jnp.full((dbsz,), POOL_STATE, I32)
    x1s = _mix(tbase_s, os_.reshape(rows_s, -1), zs, halo_s, sgas, sgbs, xs, w, tm=dec)
    ys = _moe(x1s, w, tm=rows_s)
    y_sample = ys.reshape(dbsz, dec, d)
    k_sample = s3(ks).reshape(1, dbsz, dec, N_KV_HEADS, HEAD_DIM)
    v_sample = s3(vs).reshape(1, dbsz, dec, N_KV_HEADS, HEAD_DIM)
    idx_k_sample = s3(sms)[:, :, :IDX_DIM][None]
    pool_sample = zcat[:, zcat.shape[1] - POOL_STATE:][None]

    return (y_prompt, y_sample, k_prompt, v_prompt, idx_k_prompt, pool_prompt,
            k_sample, v_sample, idx_k_sample, pool_sample)
```

```python
import functools
import math

import numpy as np
import jax
import jax.numpy as jnp
from jax import lax
from jax.experimental import pallas as pl
from jax.experimental.pallas import tpu as pltpu

CHUNK = 64
N_META = 16
EPS = 1e-6
POOL_WINDOWS = (2, 4, 8, 16)
POOL_STATE = max(POOL_WINDOWS) - 1
N_HEADS = 8
N_KV_HEADS = 2
HEAD_DIM = 128
KV_REP = N_HEADS // N_KV_HEADS
N_IDX_HEADS = 4
IDX_DIM = 64
IDX_TOPK = 256
IDX_W_SCALE = (N_IDX_HEADS * IDX_DIM) ** -0.5
ATTN_SCALE = HEAD_DIM ** -0.5
Q_PRESCALE = ATTN_SCALE * math.log2(math.e)
ROPE_THETA = 500000.0
ROT_FRACTION = 4
N_EXPERT_GROUPS = 4
EXPERTS_PER_GROUP = 8
N_EXPERTS = N_EXPERT_GROUPS * EXPERTS_PER_GROUP
D_EXPERT = 128

LANES = 128
SUBLANES = 8
BF16_SUBLANES = 16
VMEM_LIMIT_BYTES = 56 * 1024 * 1024

QK_HALF = HEAD_DIM // ROT_FRACTION // 2
IX_HALF = IDX_DIM // ROT_FRACTION // 2
TAB_COS_QK = 0
TAB_SIN_QK = QK_HALF
TAB_COS_IX = 2 * QK_HALF
TAB_SIN_IX = 2 * QK_HALF + IX_HALF

ATTN_TQ = LANES
ATTN_TQ_PROMPT = 2 * LANES
ATTN_TK = 512
TIE_BLOCK = 256
F32_TINY = float(np.finfo(np.float32).tiny)
NEG_BIG = -1e30

F32 = jnp.float32
BF16 = jnp.bfloat16
I32 = jnp.int32


def _round_up(x, m):
    return -(-x // m) * m


def _pick_tile(rows, cap):
    return max(t for t in range(BF16_SUBLANES, cap + 1, BF16_SUBLANES) if rows % t == 0)


def _rms(x, g):
    return x * lax.rsqrt(jnp.mean(x * x, axis=-1, keepdims=True) + EPS) * g


def _dot(a, b):
    return jnp.dot(a, b, preferred_element_type=F32)


def _dot_nt(a, b):
    return lax.dot_general(a, b, (((1,), (1,)), ((), ())), preferred_element_type=F32)


def _rope(y, c, a, b, shift_a, shift_b):
    outs = []
    for j in range(y.shape[1] // LANES):
        ys = y[:, j * LANES:(j + 1) * LANES]
        outs.append(ys * c + pltpu.roll(ys, shift_a, 1) * a + pltpu.roll(ys, shift_b, 1) * b)
    return outs[0] if len(outs) == 1 else jnp.concatenate(outs, axis=1)


def _inproj_kernel(x_ref, g_ref, tab_ref, wz_ref, wq_ref, wkv_ref, wqi_ref, wsm_ref, wgab_ref,
                   z_ref, q_ref, k_ref, v_ref, kb_ref, vt_ref, qis_ref, sm_ref, kid_ref, sga_ref, sgb_ref):
    h = _rms(x_ref[...], g_ref[...]).astype(BF16)
    tm = h.shape[0]
    d = x_ref.shape[1]
    lane = lax.broadcasted_iota(I32, (tm, LANES), 1)
    t = tab_ref[...]

    def rl(s):
        return pltpu.roll(t, s % LANES, 1) if s % LANES else t

    c_qk = jnp.where(lane < QK_HALF, rl(-TAB_COS_QK), jnp.where(lane < 2 * QK_HALF, rl(QK_HALF - TAB_COS_QK), 1.0))
    a_qk = jnp.where(lane < QK_HALF, -rl(-TAB_SIN_QK), 0.0)
    b_qk = jnp.where((lane >= QK_HALF) & (lane < 2 * QK_HALF), rl(QK_HALF - TAB_SIN_QK), 0.0)
    l64 = lane & (IDX_DIM - 1)
    hi = lane >= IDX_DIM
    cos_lo = jnp.where(hi, rl(IDX_DIM - TAB_COS_IX), rl(-TAB_COS_IX))
    cos_hi = jnp.where(hi, rl(IDX_DIM + IX_HALF - TAB_COS_IX), rl(IX_HALF - TAB_COS_IX))
    sin_lo = jnp.where(hi, rl(IDX_DIM - TAB_SIN_IX), rl(-TAB_SIN_IX))
    sin_hi = jnp.where(hi, rl(IDX_DIM + IX_HALF - TAB_SIN_IX), rl(IX_HALF - TAB_SIN_IX))
    c_ix = jnp.where(l64 < IX_HALF, cos_lo, jnp.where(l64 < 2 * IX_HALF, cos_hi, 1.0))
    a_ix = jnp.where(l64 < IX_HALF, -sin_lo, 0.0)
    b_ix = jnp.where((l64 >= IX_HALF) & (l64 < 2 * IX_HALF), sin_hi, 0.0)

    z_ref[...] = _dot(h, wz_ref[...])

    q = _rope(_dot(h, wq_ref[...]), c_qk, a_qk, b_qk, LANES - QK_HALF, QK_HALF)
    q_ref[...] = (q * Q_PRESCALE).astype(BF16)

    kv = _dot(h, wkv_ref[...])
    kvw = kv.shape[1] // 2
    k = _rope(kv[:, :kvw], c_qk, a_qk, b_qk, LANES - QK_HALF, QK_HALF)
    v = kv[:, kvw:]
    k_ref[...] = k
    v_ref[...] = v
    kb_ref[...] = k.astype(BF16)
    vt_ref[...] = v.T.astype(BF16)

    qi = _rope(_dot(h, wqi_ref[...]), c_ix, a_ix, b_ix, LANES - IX_HALF, IX_HALF)
    pieces = []
    for hd in range(N_IDX_HEADS):
        col = qi[:, (hd // 2) * LANES:(hd // 2 + 1) * LANES]
        keep = hi if hd % 2 else jnp.logical_not(hi)
        pieces.append(jnp.where(keep, col, 0.0))
    qis_ref[...] = jnp.concatenate(pieces, axis=1).astype(BF16)

    c_sm = jnp.where(hi, IDX_W_SCALE, c_ix)
    a_sm = jnp.where(hi, 0.0, a_ix)
    b_sm = jnp.where(hi, 0.0, b_ix)
    sm = _rope(_dot(h, wsm_ref[...]), c_sm, a_sm, b_sm, LANES - IX_HALF, IX_HALF)
    sm_ref[...] = sm
    kid_ref[...] = jnp.where(hi, pltpu.roll(sm, IDX_DIM, 1), sm).astype(BF16)

    gab = _dot(h, wgab_ref[...])
    sga_ref[...] = jax.nn.sigmoid(gab[:, :d])
    sgb_ref[...] = jax.nn.sigmoid(gab[:, d:])


def _inproj(x2d, g, tab, w, *, tm, tab_tiles):
    rows, d = x2d.shape
    nt = rows // tm
    qw = w["wq"].shape[1]
    kvw = w["wkv"].shape[1] // 2
    row_spec = lambda n: pl.BlockSpec((tm, n), lambda i: (i, 0))
    const = lambda a: pl.BlockSpec(a.shape, lambda i: (0,) * a.ndim)
    outs = [
        (d, F32),
        (qw, BF16),
        (kvw, F32), (kvw, F32),
        (kvw, BF16),
        None,
        (N_IDX_HEADS * LANES, BF16),
        (LANES, F32),
        (LANES, BF16),
        (d, F32), (d, F32),
    ]
    out_specs = [pl.BlockSpec((kvw, tm), lambda i: (0, i)) if o is None else row_spec(o[0]) for o in outs]
    out_shape = [jax.ShapeDtypeStruct((kvw, rows), BF16) if o is None else jax.ShapeDtypeStruct((rows, o[0]), o[1])
                 for o in outs]
    return pl.pallas_call(
        _inproj_kernel,
        grid=(nt,),
        in_specs=[row_spec(d), const(g),
                  pl.BlockSpec((tm, LANES), lambda i: (i % tab_tiles, 0)),
                  const(w["wz"]), const(w["wq"]), const(w["wkv"]), const(w["wqi"]), const(w["wsm"]), const(w["wgab"])],
        out_specs=out_specs,
        out_shape=out_shape,
        compiler_params=pltpu.CompilerParams(dimension_semantics=("arbitrary",), vmem_limit_bytes=VMEM_LIMIT_BYTES),
        name="inproj",
    )(x2d, g, tab, w["wz"], w["wq"], w["wkv"], w["wqi"], w["wsm"], w["wgab"])


def _row_fold(x, op, group=SUBLANES):
    parts = [x[j * group:(j + 1) * group] for j in range(x.shape[0] // group)]
    while len(parts) > 1:
        parts = [op(parts[j], parts[j + 1]) for j in range(0, len(parts) - 1, 2)] + parts[len(parts) & ~1:]
    return parts[0]


def _attn_kernel(nch_ref, qis_ref, sm_ref, kid_ref, q_ref, k_ref, vt_ref, tri_ref, o_ref,
                 sc_scr, acc_scr, s_scr, *, tq, tk, topk, causal, n_keys):
    i = pl.program_id(1)
    nch = nch_ref[i]
    sub = tri_ref.shape[0]
    nq4 = KV_REP * tq
    qpos = i * tq + lax.broadcasted_iota(I32, (1, tq), 1)
    if causal:
        lim = N_META + CHUNK * (jnp.right_shift(qpos - N_META, int(math.log2(CHUNK))) + 1)
        lim = jnp.minimum(lim, n_keys)
    else:
        lim = jnp.full((1, tq), n_keys, I32)

    qstack = jnp.concatenate([qis_ref[:, hd * LANES:(hd + 1) * LANES] for hd in range(N_IDX_HEADS)], axis=0)
    smt = sm_ref[...].T
    wlane = jnp.concatenate([smt[IDX_DIM + hd:IDX_DIM + hd + 1, :] for hd in range(N_IDX_HEADS)], axis=1)

    def score_body(c, st):
        smax, smin = st
        off = pl.multiple_of(c * tk, tk)
        sc = _dot_nt(kid_ref[pl.ds(off, tk), :], qstack)
        sc = jnp.maximum(sc, 0.0) * wlane
        s = sc[:, 0:tq]
        for hd in range(1, N_IDX_HEADS):
            s = s + sc[:, hd * tq:(hd + 1) * tq]
        s = jnp.where(s == 0.0, 0.0, s)
        ok = off + lax.broadcasted_iota(I32, (tk, tq), 0) < lim
        sc_scr[pl.ds(off, tk), :] = jnp.where(ok, s, -jnp.inf)
        return (jnp.maximum(smax, _row_fold(jnp.where(ok, s, -jnp.inf), jnp.maximum)),
                jnp.minimum(smin, _row_fold(jnp.where(ok, s, jnp.inf), jnp.minimum)))

    smax, smin = lax.fori_loop(0, nch, score_body, (jnp.full((SUBLANES, tq), -jnp.inf, F32),
                                                    jnp.full((SUBLANES, tq), jnp.inf, F32)))
    smax = jnp.max(smax, axis=0, keepdims=True)
    smin = jnp.min(smin, axis=0, keepdims=True)

    def count_ge(cand):
        def body(c, acc):
            kb = sc_scr[pl.ds(pl.multiple_of(c * tk, tk), tk), :]
            return acc + _row_fold(jnp.where(kb >= cand, 1.0, 0.0), jnp.add)
        acc = lax.fori_loop(0, nch, body, jnp.zeros((SUBLANES, tq), F32))
        return jnp.sum(acc, axis=0, keepdims=True)

    take_all = lim <= topk
    kf = float(topk)
    lo0 = jnp.where(take_all, NEG_BIG, smin)
    hi0 = jnp.where(take_all, NEG_BIG, smax + jnp.abs(smax) * 2.0 ** -10 + F32_TINY * 2.0 ** 24)
    c_lo0 = lim.astype(F32)
    c_hi0 = jnp.where(take_all, kf, 0.0)
    closed0 = jnp.where(take_all, 1.0, 0.0)

    def search_cond(st):
        return (st[0] < 64) & (jnp.max(1.0 - st[5]) > 0.5)

    def search_body(st):
        it, lo, hi, c_lo, c_hi, closed = st
        mid = lo + 0.5 * (hi - lo)
        interp = hi - (hi - lo) * ((kf - c_hi) / jnp.maximum(c_lo - c_hi, 1.0))
        inside = lambda v: (v > lo) & (v < hi)
        cand = jnp.where((it & 1) == 0, interp, mid)
        cand = jnp.where((it == 0) & inside(0.0), 0.0, cand)
        cand = jnp.where((it == 1) & inside(F32_TINY), F32_TINY, cand)
        cand = jnp.where(inside(cand), cand, mid)
        cnt = count_ge(cand)
        is_open = closed < 0.5
        up = is_open & (cnt > kf)
        down = is_open & (cnt <= kf)
        lo = jnp.where(up, cand, lo)
        c_lo = jnp.where(up, cnt, c_lo)
        hi = jnp.where(down, cand, hi)
        c_hi = jnp.where(down, cnt, c_hi)
        done = (c_hi == kf) | jnp.logical_not(inside(lo + 0.5 * (hi - lo))) | ((lo == 0.0) & (hi == F32_TINY))
        return it + 1, lo, hi, c_lo, c_hi, jnp.where(done, 1.0, closed)

    _, thr_lo, thr_hi, _, c_hi, _ = lax.while_loop(search_cond, search_body,
                                                   (jnp.int32(0), lo0, hi0, c_lo0, c_hi0, closed0))
    n_tie = kf - c_hi

    acc_scr[...] = jnp.zeros(acc_scr.shape, F32)
    qgs = [jnp.concatenate([q_ref[:, (g * KV_REP + r) * HEAD_DIM:(g * KV_REP + r + 1) * HEAD_DIM]
                            for r in range(KV_REP)], axis=0) for g in range(N_KV_HEADS)]

    ones_rows = jnp.ones((BF16_SUBLANES, tk), BF16)

    def logits(c, slot, seen):
        off = pl.multiple_of(c * tk, tk)
        kb = sc_scr[pl.ds(off, tk), :]
        above = kb >= thr_hi
        eq = (kb >= thr_lo) & jnp.logical_not(above)
        eqf = jnp.where(eq, 1.0, 0.0)
        ranks = []
        for hf in range(tk // sub):
            e = eqf[hf * sub:(hf + 1) * sub]
            ranks.append(seen + _dot(tri_ref[...], e.astype(BF16)))
            seen = seen + jnp.sum(_row_fold(e, jnp.add), axis=0, keepdims=True)
        rank = jnp.concatenate(ranks, axis=0)
        sel = above | (eq & (rank < n_tie))
        maskf = jnp.where(sel, 0.0, NEG_BIG)
        mask4 = jnp.concatenate([maskf] * KV_REP, axis=1)
        mcs = []
        for g in range(N_KV_HEADS):
            kc = k_ref[pl.ds(off, tk), g * HEAD_DIM:(g + 1) * HEAD_DIM]
            s = _dot_nt(kc, qgs[g]) + mask4
            s_scr[slot, g] = s
            mcs.append(jnp.max(_row_fold(s, jnp.maximum), axis=0, keepdims=True))
        return seen, tuple(mcs)

    def update(c, slot, ms, mcs):
        off = pl.multiple_of(c * tk, tk)
        ms_n = []
        for g in range(N_KV_HEADS):
            vtc = vt_ref[g * HEAD_DIM:(g + 1) * HEAD_DIM, pl.ds(off, tk)]
            m_new = jnp.maximum(ms[g], mcs[g])
            alpha = jnp.exp2(ms[g] - m_new)
            p = jnp.exp2(s_scr[slot, g] - m_new).astype(BF16)
            acc_scr[g] = alpha * acc_scr[g] + _dot(jnp.concatenate([vtc, ones_rows], axis=0), p)
            ms_n.append(m_new)
        return tuple(ms_n)

    def flash_body(c, st):
        seen, ms, mcs = st
        slot = c & 1
        ms = update(c, slot, ms, mcs)
        seen, mcs = logits(c + 1, 1 - slot, seen)
        return seen, ms, mcs

    m0 = tuple(jnp.full((1, nq4), NEG_BIG, F32) for _ in range(N_KV_HEADS))
    seen0, mc0 = logits(0, 0, jnp.zeros((1, tq), F32))
    _, ms, mcs = lax.fori_loop(0, nch - 1, flash_body, (seen0, m0, mc0))
    update(nch - 1, (nch - 1) & 1, ms, mcs)

    for g in range(N_KV_HEADS):
        og = acc_scr[g, 0:HEAD_DIM] / acc_scr[g, HEAD_DIM:HEAD_DIM + 1]
        for r in range(KV_REP):
            hd = g * KV_REP + r
            o_ref[:, hd * HEAD_DIM:(hd + 1) * HEAD_DIM] = og[:, r * tq:(r + 1) * tq].T.astype(o_ref.dtype)


def _attention(qis, sm, kid, q, kb, vt, *, tq, topk, causal, n_keys):
    tk, sub = ATTN_TK, TIE_BLOCK
    bsz, sq, _ = q.shape
    sk = kb.shape[1]
    assert sq % tq == 0 and sk % tk == 0
    nq = sq // tq
    if causal:
        ends = np.minimum(N_META + CHUNK * ((np.arange(nq) * tq + tq - 1 - N_META) // CHUNK + 1), n_keys)
    else:
        ends = np.full((nq,), n_keys)
    nch = jnp.asarray(-(-ends // tk), I32)
    tri = jnp.asarray(np.tril(np.ones((sub, sub), np.float32), -1), BF16)
    qrow = lambda n: pl.BlockSpec((None, tq, n), lambda b, i, s: (b, i, 0))
    krow = lambda n: pl.BlockSpec((None, sk, n), lambda b, i, s: (b, 0, 0), pipeline_mode=pl.Buffered(1))
    kern = functools.partial(_attn_kernel, tq=tq, tk=tk, topk=topk, causal=causal, n_keys=n_keys)
    return pl.pallas_call(
        kern,
        grid_spec=pltpu.PrefetchScalarGridSpec(
            num_scalar_prefetch=1,
            grid=(bsz, nq),
            in_specs=[qrow(qis.shape[2]), qrow(sm.shape[2]), krow(kid.shape[2]), qrow(q.shape[2]),
                      krow(kb.shape[2]),
                      pl.BlockSpec((vt.shape[0], sk), lambda b, i, s: (0, b), pipeline_mode=pl.Buffered(1)),
                      pl.BlockSpec((sub, sub), lambda b, i, s: (0, 0))],
            out_specs=qrow(q.shape[2]),
            scratch_shapes=[pltpu.VMEM((sk, tq), F32),
                            pltpu.VMEM((N_KV_HEADS, HEAD_DIM + BF16_SUBLANES, KV_REP * tq), F32),
                            pltpu.VMEM((2, N_KV_HEADS, tk, KV_REP * tq), F32)]),
        out_shape=jax.ShapeDtypeStruct(q.shape, BF16),
        compiler_params=pltpu.CompilerParams(dimension_semantics=("arbitrary", "arbitrary"),
                                             vmem_limit_bytes=VMEM_LIMIT_BYTES),
        name="attn",
    )(nch, qis, sm, kid, q, kb, vt, tri)


def _mix_kernel(tb_ref, o_ref, z_ref, halo_ref, sga_ref, sgb_ref, x_ref, wpool_ref, ps_ref, wao_ref, wout_ref,
                x1_ref, zc_scr, *, tm):
    hs = halo_ref.shape[0]
    zc_scr[0:hs, :] = halo_ref[...]
    zc_scr[hs:hs + tm, :] = z_ref[...]
    avail = tb_ref[pl.program_id(0)] + lax.broadcasted_iota(I32, (tm, 1), 0) + 1
    gw = z_ref.shape[1] // len(POOL_WINDOWS)
    ys = []
    for g, win in enumerate(POOL_WINDOWS):
        cols = slice(g * gw, (g + 1) * gw)
        zt = zc_scr[hs:hs + tm, cols]
        s = zt
        for j in range(1, win):
            s = s + zc_scr[hs - j:hs - j + tm, cols]
        cnt = jnp.minimum(avail, win).astype(F32)
        dlt = s / cnt - zt
        ys.append(_dot(dlt.astype(BF16), wpool_ref[g]))
    a = jnp.concatenate(ys, axis=1) * ps_ref[...]
    b = _dot(o_ref[...], wao_ref[...])
    mrg = sga_ref[...] * a + sgb_ref[...] * b
    x1_ref[...] = x_ref[...] + _dot(mrg.astype(BF16), wout_ref[...])


def _mix(tbase, o, z, halo, sga, sgb, x, w, *, tm):
    rows, d = x.shape
    nt = rows // tm
    hs = halo.shape[1]
    row = lambda n: pl.BlockSpec((tm, n), lambda i, s: (i, 0))
    const = lambda a: pl.BlockSpec(a.shape, lambda i, s: (0,) * a.ndim)
    return pl.pallas_call(
        functools.partial(_mix_kernel, tm=tm),
        grid_spec=pltpu.PrefetchScalarGridSpec(
            num_scalar_prefetch=1,
            grid=(nt,),
            in_specs=[row(d), row(d), pl.BlockSpec((None, hs, d), lambda i, s: (i, 0, 0)), row(d), row(d), row(d),
                      const(w["wpool"]), const(w["pscale"]), const(w["wao"]), const(w["wout"])],
            out_specs=row(d),
            scratch_shapes=[pltpu.VMEM((hs + tm, d), F32)]),
        out_shape=jax.ShapeDtypeStruct((rows, d), F32),
        compiler_params=pltpu.CompilerParams(dimension_semantics=("arbitrary",), vmem_limit_bytes=VMEM_LIMIT_BYTES),
        name="mix",
    )(tbase, o, z, halo, sga, sgb, x, w["wpool"], w["pscale"], w["wao"], w["wout"])


ROUTE_GROUP_LANE0 = N_EXPERTS


def _moe_kernel(x_ref, gf_ref, gl_ref, wrh_ref, wrl_ref, br_ref, wg_ref, wu_ref, wd_ref, y_ref,
                h_scr, route_scr, acc_scr, *, ec):
    j = pl.program_id(1)
    tm = x_ref.shape[0]

    @pl.when(j == 0)
    def _():
        hf = _rms(x_ref[...], gf_ref[...])
        h_hi = hf.astype(BF16)
        h_lo = (hf - h_hi.astype(F32)).astype(BF16)
        h_scr[...] = h_hi
        logits = _dot(h_hi, wrh_ref[...]) + _dot(h_lo, wrh_ref[...]) + _dot(h_hi, wrl_ref[...]) + br_ref[...]
        lane = lax.broadcasted_iota(I32, (tm, LANES), 1)
        lanef = lane.astype(F32)
        ninf = -jnp.inf
        isg = (lane >= ROUTE_GROUP_LANE0) & (lane < ROUTE_GROUP_LANE0 + N_EXPERT_GROUPS)
        gl = jnp.where(isg, logits, ninf)
        gmax = jnp.max(gl, axis=1, keepdims=True)
        gsel = jnp.min(jnp.where(gl == gmax, lanef, 1e9), axis=1, keepdims=True)
        gprob = 1.0 / jnp.sum(jnp.exp(gl - gmax), axis=1, keepdims=True)
        e0 = (gsel - ROUTE_GROUP_LANE0) * EXPERTS_PER_GROUP
        el = jnp.where((lanef >= e0) & (lanef < e0 + EXPERTS_PER_GROUP), logits, ninf)
        v1 = jnp.max(el, axis=1, keepdims=True)
        i1 = jnp.min(jnp.where(el == v1, lanef, 1e9), axis=1, keepdims=True)
        el2 = jnp.where(lanef == i1, ninf, el)
        v2 = jnp.max(el2, axis=1, keepdims=True)
        i2 = jnp.min(jnp.where(el2 == v2, lanef, 1e9), axis=1, keepdims=True)
        e = jnp.exp(v2 - v1)
        w1 = (1.0 / (1.0 + e)) * gprob
        w2 = (e / (1.0 + e)) * gprob
        for n, val in enumerate((i1, i2, w1, w2)):
            route_scr[n] = jnp.broadcast_to(val, (tm, LANES))
        acc_scr[...] = jnp.zeros(acc_scr.shape, F32)

    h = h_scr[...]
    i1, i2, w1, w2 = route_scr[0], route_scr[1], route_scr[2], route_scr[3]
    gate = _dot(h, wg_ref[...])
    up = _dot(h, wu_ref[...])
    act = gate * jax.nn.sigmoid(gate) * up
    parts = []
    for q in range(ec):
        ef = jnp.full((tm, LANES), j * ec + q, I32).astype(F32)
        comb = jnp.where(i1 == ef, w1, 0.0) + jnp.where(i2 == ef, w2, 0.0)
        parts.append(act[:, q * D_EXPERT:(q + 1) * D_EXPERT] * comb)
    acc_scr[...] += _dot(jnp.concatenate(parts, axis=1).astype(BF16), wd_ref[...])

    @pl.when(j == pl.num_programs(1) - 1)
    def _():
        y_ref[...] = _rms(x_ref[...] + acc_scr[...], gl_ref[...])


def _moe(x, w, *, tm, ec=4):
    rows, d = x.shape
    nt = rows // tm
    ne = N_EXPERTS // ec
    cw = ec * D_EXPERT
    const = lambda a: pl.BlockSpec(a.shape, lambda i, j: (0,) * a.ndim)
    return pl.pallas_call(
        functools.partial(_moe_kernel, ec=ec),
        grid=(nt, ne),
        in_specs=[pl.BlockSpec((tm, d), lambda i, j: (i, 0)), const(w["gffn"]), const(w["gfinal"]),
                  const(w["wr_hi"]), const(w["wr_lo"]), const(w["br"]),
                  pl.BlockSpec((d, cw), lambda i, j: (0, j)), pl.BlockSpec((d, cw), lambda i, j: (0, j)),
                  pl.BlockSpec((cw, d), lambda i, j: (j, 0))],
        out_specs=pl.BlockSpec((tm, d), lambda i, j: (i, 0)),
        out_shape=jax.ShapeDtypeStruct((rows, d), F32),
        scratch_shapes=[pltpu.VMEM((tm, d), BF16), pltpu.VMEM((4, tm, LANES), F32), pltpu.VMEM((tm, d), F32)],
        compiler_params=pltpu.CompilerParams(dimension_semantics=("arbitrary", "arbitrary"),
                                             vmem_limit_bytes=VMEM_LIMIT_BYTES),
        name="moe",
    )(x, w["gffn"], w["gfinal"], w["wr_hi"], w["wr_lo"], w["br"], w["wg"], w["wu"], w["wd"])


def _rope_table(pos):
    posf = pos.astype(F32)[:, None]

    def cs(half):
        inv = jnp.exp(-math.log(ROPE_THETA) * jnp.arange(half, dtype=F32) * (1.0 / half))
        ang = posf * inv[None, :]
        return jnp.cos(ang), jnp.sin(ang)

    cq, sq = cs(QK_HALF)
    ci, si = cs(IX_HALF)
    used = 2 * QK_HALF + 2 * IX_HALF
    return jnp.concatenate([cq, sq, ci, si, jnp.zeros((pos.shape[0], LANES - used), F32)], axis=1)


def _prep_weights(l, norm_mix_g, norm_ffn_g, norm_final_g, w_in, w_pool, pool_scale, w_attn_out, w_out,
                  w_router_group, b_router_group, w_router_expert, b_router_expert,
                  w_expert_gate, w_expert_up, w_expert_down):
    d = w_in.shape[1]
    pw = w_pool.shape[1] * w_pool.shape[2]
    qw = N_HEADS * HEAD_DIM
    kvw = N_KV_HEADS * HEAD_DIM
    widths = (pw, qw, kvw, kvw, N_IDX_HEADS * IDX_DIM, IDX_DIM, N_IDX_HEADS, d, d)
    offs = np.concatenate([[0], np.cumsum(widths)])
    wi = w_in[l]
    piece = lambda a, b: wi[:, offs[a]:offs[b]].astype(BF16)
    wsm = jnp.concatenate([wi[:, offs[5]:offs[7]], jnp.zeros((d, LANES - IDX_DIM - N_IDX_HEADS), F32)], axis=1)
    wr = jnp.concatenate([w_router_expert[l], w_router_group[l],
                          jnp.zeros((d, LANES - N_EXPERTS - N_EXPERT_GROUPS), F32)], axis=1)
    wr_hi = wr.astype(BF16)
    br = jnp.concatenate([b_router_expert[l], b_router_group[l],
                          jnp.zeros((LANES - N_EXPERTS - N_EXPERT_GROUPS,), F32)])[None, :]
    flat_e = lambda a: a.transpose(1, 0, 2).reshape(d, N_EXPERTS * D_EXPERT).astype(BF16)
    return dict(
        gmix=norm_mix_g[l][None, :], gffn=norm_ffn_g[l][None, :], gfinal=norm_final_g[None, :],
        wz=piece(0, 1), wq=piece(1, 2), wkv=piece(2, 4), wqi=piece(4, 5), wsm=wsm.astype(BF16), wgab=piece(7, 9),
        wpool=w_pool[l].astype(BF16), pscale=pool_scale[l][None, :],
        wao=w_attn_out[l].astype(BF16), wout=w_out[l].astype(BF16),
        wr_hi=wr_hi, wr_lo=(wr - wr_hi.astype(F32)).astype(BF16), br=br,
        wg=flat_e(w_expert_gate[l]), wu=flat_e(w_expert_up[l]),
        wd=w_expert_down[l].reshape(N_EXPERTS * D_EXPERT, d).astype(BF16),
    )


def _dup_lanes(ki):
    return jnp.concatenate([ki, ki], axis=-1)


def kernel(x_prompt, x_sample, cache_k, cache_v, cache_idx_k, state_pool, meta_tokens, norm_mix_g, norm_ffn_g,
           norm_final_g, w_in, w_pool, pool_scale, w_attn_out, w_out, w_router_group, b_router_group,
           w_router_expert, b_router_expert, w_expert_gate, w_expert_up, w_expert_down):
    bsz, seq, d = x_prompt.shape
    dbsz, dec, _ = x_sample.shape
    assert w_in.shape[0] == 1, "single-layer step"
    l = 0
    past = cache_k.shape[2] - N_META
    n = N_META + seq
    topk_p = min(IDX_TOPK, seq // 4)
    topk_s = min(IDX_TOPK, (past + dec) // 4)
    kvw = N_KV_HEADS * HEAD_DIM

    tk = ATTN_TK
    sp = _round_up(n, tk)
    tm = _pick_tile(sp, 512)
    tm_moe = _pick_tile(sp, 1100)
    w = _prep_weights(l, norm_mix_g, norm_ffn_g, norm_final_g, w_in, w_pool, pool_scale, w_attn_out, w_out,
                      w_router_group, b_router_group, w_router_expert, b_router_expert,
                      w_expert_gate, w_expert_up, w_expert_down)

    xp = jnp.concatenate([jnp.broadcast_to(meta_tokens.astype(F32)[None], (bsz, N_META, d)), x_prompt,
                          jnp.zeros((bsz, sp - n, d), F32)], axis=1).reshape(bsz * sp, d)
    nt = sp // tm
    tab_p = _rope_table(jnp.arange(sp))
    z, q, k, v, kb, vt, qis, sm, kid, sga, sgb = _inproj(xp, w["gmix"], tab_p, w, tm=tm, tab_tiles=nt)
    b3 = lambda a: a.reshape(bsz, sp, a.shape[-1])
    o = _attention(b3(qis), b3(sm), b3(kid), b3(q), b3(kb), vt, tq=ATTN_TQ_PROMPT, topk=topk_p, causal=True, n_keys=n)
    hs = POOL_STATE + 1
    z4 = z.reshape(bsz, nt, tm, d)
    halo = jnp.concatenate([jnp.zeros((bsz, 1, hs, d), F32), z4[:, :-1, tm - hs:, :]], axis=1).reshape(bsz * nt, hs, d)
    tbase = jnp.asarray(np.tile(np.arange(nt) * tm, bsz), I32)
    x1 = _mix(tbase, o.reshape(bsz * sp, -1), z, halo, sga, sgb, xp, w, tm=tm)
    yp = _moe(x1, w, tm=tm_moe)
    y_prompt = yp.reshape(bsz, sp, d)[:, N_META:n]
    k_prompt = b3(k)[:, :n].reshape(1, bsz, n, N_KV_HEADS, HEAD_DIM)
    v_prompt = b3(v)[:, :n].reshape(1, bsz, n, N_KV_HEADS, HEAD_DIM)
    idx_k_prompt = b3(sm)[:, :n, :IDX_DIM][None]
    pool_prompt = b3(z)[:, n - POOL_STATE:n][None]

    rows_s = dbsz * dec
    tab_s = jnp.tile(_rope_table(N_META + past + jnp.arange(dec)), (dbsz, 1))
    xs = x_sample.reshape(rows_s, d)
    zs, qs, ks, vs, kbs, _, qiss, sms, kids, sgas, sgbs = _inproj(xs, w["gmix"], tab_s, w, tm=rows_s, tab_tiles=1)
    nk = N_META + past + dec
    sk = _round_up(nk, tk)
    sq_s = _round_up(dec, ATTN_TQ)
    s3 = lambda a: a.reshape(dbsz, dec, a.shape[-1])
    padr = lambda a, rows: jnp.concatenate([a, jnp.zeros((dbsz, rows - a.shape[1], a.shape[-1]), a.dtype)], axis=1)
    k_all = padr(jnp.concatenate([cache_k[l].reshape(dbsz, N_META + past, kvw).astype(BF16), s3(kbs)], axis=1), sk)
    v_all = padr(jnp.concatenate([cache_v[l].reshape(dbsz, N_META + past, kvw).astype(BF16),
                                  s3(vs).astype(BF16)], axis=1), sk)
    vt_all = v_all.transpose(2, 0, 1).reshape(kvw, dbsz * sk)
    ki_all = padr(jnp.concatenate([_dup_lanes(cache_idx_k[l]).astype(BF16), s3(kids)], axis=1), sk)
    os_ = _attention(padr(s3(qiss), sq_s), padr(s3(sms), sq_s), ki_all, padr(s3(qs), sq_s), k_all, vt_all,
                     tq=ATTN_TQ, topk=topk_s, causal=False, n_keys=nk)[:, :dec]
    zcat = jnp.concatenate([state_pool[l].astype(F32), s3(zs)], axis=1)
    halo_s = jnp.concatenate([jnp.zeros((dbsz, 1, d), F32), zcat[:, :POOL_STATE]], axis=1)
    tbase_s = jnp.full((dbsz,), POOL_STATE, I32)
    x1s = _mix(tbase_s, os_.reshape(rows_s, -1), zs, halo_s, sgas, sgbs, xs, w, tm=dec)
    ys = _moe(x1s, w, tm=rows_s)
    y_sample = ys.reshape(dbsz, dec, d)
    k_sample = s3(ks).reshape(1, dbsz, dec, N_KV_HEADS, HEAD_DIM)
    v_sample = s3(vs).reshape(1, dbsz, dec, N_KV_HEADS, HEAD_DIM)
    idx_k_sample = s3(sms)[:, :, :IDX_DIM][None]
    pool_sample = zcat[:, zcat.shape[1] - POOL_STATE:][None]

    return (y_prompt, y_sample, k_prompt, v_prompt, idx_k_prompt, pool_prompt,
            k_sample, v_sample, idx_k_sample, pool_sample)
```

```python
import functools
import math

import numpy as np
import jax
import jax.numpy as jnp
from jax import lax
from jax.experimental import pallas as pl
from jax.experimental.pallas import tpu as pltpu

CHUNK = 64
N_META = 16
EPS = 1e-6
POOL_WINDOWS = (2, 4, 8, 16)
POOL_STATE = max(POOL_WINDOWS) - 1
N_HEADS = 8
N_KV_HEADS = 2
HEAD_DIM = 128
KV_REP = N_HEADS // N_KV_HEADS
N_IDX_HEADS = 4
IDX_DIM = 64
IDX_TOPK = 256
IDX_W_SCALE = (N_IDX_HEADS * IDX_DIM) ** -0.5
ATTN_SCALE = HEAD_DIM ** -0.5
Q_PRESCALE = ATTN_SCALE * math.log2(math.e)
ROPE_THETA = 500000.0
ROT_FRACTION = 4
N_EXPERT_GROUPS = 4
EXPERTS_PER_GROUP = 8
N_EXPERTS = N_EXPERT_GROUPS * EXPERTS_PER_GROUP
D_EXPERT = 128

LANES = 128
SUBLANES = 8
BF16_SUBLANES = 16
VMEM_LIMIT_BYTES = 56 * 1024 * 1024

QK_HALF = HEAD_DIM // ROT_FRACTION // 2
IX_HALF = IDX_DIM // ROT_FRACTION // 2
TAB_COS_QK = 0
TAB_SIN_QK = QK_HALF
TAB_COS_IX = 2 * QK_HALF
TAB_SIN_IX = 2 * QK_HALF + IX_HALF

ATTN_TQ = LANES
ATTN_TQ_PROMPT = 2 * LANES
ATTN_TK = 512
TIE_BLOCK = 256
F32_TINY = float(np.finfo(np.float32).tiny)
NEG_BIG = -1e30

F32 = jnp.float32
BF16 = jnp.bfloat16
I32 = jnp.int32


def _round_up(x, m):
    return -(-x // m) * m


def _pick_tile(rows, cap):
    return max(t for t in range(BF16_SUBLANES, cap + 1, BF16_SUBLANES) if rows % t == 0)


def _rms(x, g):
    return x * lax.rsqrt(jnp.mean(x * x, axis=-1, keepdims=True) + EPS) * g


def _dot(a, b):
    return jnp.dot(a, b, preferred_element_type=F32)


def _dot_nt(a, b):
    return lax.dot_general(a, b, (((1,), (1,)), ((), ())), preferred_element_type=F32)


def _rope(y, c, a, b, shift_a, shift_b):
    outs = []
    for j in range(y.shape[1] // LANES):
        ys = y[:, j * LANES:(j + 1) * LANES]
        outs.append(ys * c + pltpu.roll(ys, shift_a, 1) * a + pltpu.roll(ys, shift_b, 1) * b)
    return outs[0] if len(outs) == 1 else jnp.concatenate(outs, axis=1)


def _inproj_kernel(x_ref, prev_ref, meta_ref, g_ref, tab_ref, wz_ref, wq_ref, wkv_ref, wqi_ref, wsm_ref, wgab_ref,
                   xs_ref, z_ref, q_ref, k4_ref, v4_ref, kb_ref, vt_ref, qis_ref, sm_ref, kid_ref, sga_ref, sgb_ref,
                   *, lead, frame_tiles):
    tm, d = x_ref.shape
    if lead:
        ti = pl.program_id(0) % (frame_tiles + 1)
        head = jnp.where(ti == 0, meta_ref[...], prev_ref[...])
        body = jnp.where(ti < frame_tiles, x_ref[0:tm - lead, :], 0.0)
        x = jnp.concatenate([head, body], axis=0)
    else:
        x = x_ref[...]
    xs_ref[...] = x
    h = _rms(x, g_ref[...]).astype(BF16)
    lane = lax.broadcasted_iota(I32, (tm, LANES), 1)
    t = tab_ref[...]

    def rl(s):
        return pltpu.roll(t, s % LANES, 1) if s % LANES else t

    c_qk = jnp.where(lane < QK_HALF, rl(-TAB_COS_QK), jnp.where(lane < 2 * QK_HALF, rl(QK_HALF - TAB_COS_QK), 1.0))
    a_qk = jnp.where(lane < QK_HALF, -rl(-TAB_SIN_QK), 0.0)
    b_qk = jnp.where((lane >= QK_HALF) & (lane < 2 * QK_HALF), rl(QK_HALF - TAB_SIN_QK), 0.0)
    l64 = lane & (IDX_DIM - 1)
    hi = lane >= IDX_DIM
    cos_lo = jnp.where(hi, rl(IDX_DIM - TAB_COS_IX), rl(-TAB_COS_IX))
    cos_hi = jnp.where(hi, rl(IDX_DIM + IX_HALF - TAB_COS_IX), rl(IX_HALF - TAB_COS_IX))
    sin_lo = jnp.where(hi, rl(IDX_DIM - TAB_SIN_IX), rl(-TAB_SIN_IX))
    sin_hi = jnp.where(hi, rl(IDX_DIM + IX_HALF - TAB_SIN_IX), rl(IX_HALF - TAB_SIN_IX))
    c_ix = jnp.where(l64 < IX_HALF, cos_lo, jnp.where(l64 < 2 * IX_HALF, cos_hi, 1.0))
    a_ix = jnp.where(l64 < IX_HALF, -sin_lo, 0.0)
    b_ix = jnp.where((l64 >= IX_HALF) & (l64 < 2 * IX_HALF), sin_hi, 0.0)

    z_ref[...] = _dot(h, wz_ref[...])

    q = _rope(_dot(h, wq_ref[...]), c_qk, a_qk, b_qk, LANES - QK_HALF, QK_HALF)
    q_ref[...] = (q * Q_PRESCALE).astype(BF16)

    kv = _dot(h, wkv_ref[...])
    kvw = kv.shape[1] // 2
    k = _rope(kv[:, :kvw], c_qk, a_qk, b_qk, LANES - QK_HALF, QK_HALF)
    v = kv[:, kvw:]
    for g in range(N_KV_HEADS):
        k4_ref[:, g, :] = k[:, g * HEAD_DIM:(g + 1) * HEAD_DIM]
        v4_ref[:, g, :] = v[:, g * HEAD_DIM:(g + 1) * HEAD_DIM]
    kb_ref[...] = k.astype(BF16)
    vt_ref[...] = v.T.astype(BF16)

    qi = _rope(_dot(h, wqi_ref[...]), c_ix, a_ix, b_ix, LANES - IX_HALF, IX_HALF)
    pieces = []
    for hd in range(N_IDX_HEADS):
        col = qi[:, (hd // 2) * LANES:(hd // 2 + 1) * LANES]
        keep = hi if hd % 2 else jnp.logical_not(hi)
        pieces.append(jnp.where(keep, col, 0.0))
    qis_ref[...] = jnp.concatenate(pieces, axis=1).astype(BF16)

    c_sm = jnp.where(hi, IDX_W_SCALE, c_ix)
    a_sm = jnp.where(hi, 0.0, a_ix)
    b_sm = jnp.where(hi, 0.0, b_ix)
    sm = _rope(_dot(h, wsm_ref[...]), c_sm, a_sm, b_sm, LANES - IX_HALF, IX_HALF)
    sm_ref[...] = sm
    kid_ref[...] = jnp.where(hi, pltpu.roll(sm, IDX_DIM, 1), sm).astype(BF16)

    gab = _dot(h, wgab_ref[...])
    sga_ref[...] = jax.nn.sigmoid(gab[:, :d])
    sgb_ref[...] = jax.nn.sigmoid(gab[:, d:])


def _inproj(x, meta, g, tab, w, *, tm, n_out, lead):
    bsz, s, d = x.shape
    assert s % tm == 0
    ft = s // tm
    nt = ft + 1 if lead else ft
    rows = bsz * nt * tm
    qw = w["wq"].shape[1]
    kvw = w["wkv"].shape[1] // 2
    lb = max(lead, SUBLANES)
    row_spec = lambda n: pl.BlockSpec((tm, n), lambda i: (i, 0))
    const = lambda a: pl.BlockSpec(a.shape, lambda i: (0,) * a.ndim)
    cache_spec = pl.BlockSpec((None, tm, N_KV_HEADS, HEAD_DIM), lambda i: (i // nt, i % nt, 0, 0))
    cache_shape = jax.ShapeDtypeStruct((bsz, n_out, N_KV_HEADS, HEAD_DIM), F32)
    outs = [
        (d, F32),
        (d, F32),
        (qw, BF16),
        "cache", "cache",
        (kvw, BF16),
        "vt",
        (N_IDX_HEADS * LANES, BF16),
        (LANES, F32),
        (LANES, BF16),
        (d, F32), (d, F32),
    ]
    spec_of = lambda o: (cache_spec if o == "cache" else pl.BlockSpec((kvw, tm), lambda i: (0, i)) if o == "vt"
                         else row_spec(o[0]))
    shape_of = lambda o: (cache_shape if o == "cache" else jax.ShapeDtypeStruct((kvw, rows), BF16) if o == "vt"
                          else jax.ShapeDtypeStruct((rows, o[0]), o[1]))
    return pl.pallas_call(
        functools.partial(_inproj_kernel, lead=lead, frame_tiles=ft),
        grid=(bsz * nt,),
        in_specs=[pl.BlockSpec((None, tm, d), lambda i: (i // nt, jnp.minimum(i % nt, ft - 1), 0)),
                  pl.BlockSpec((None, lb, d), lambda i: (i // nt, jnp.maximum((i % nt) * (tm // lb) - 1, 0), 0)),
                  const(meta), const(g),
                  pl.BlockSpec((tm, LANES), lambda i: (i % nt, 0)),
                  const(w["wz"]), const(w["wq"]), const(w["wkv"]), const(w["wqi"]), const(w["wsm"]), const(w["wgab"])],
        out_specs=[spec_of(o) for o in outs],
        out_shape=[shape_of(o) for o in outs],
        compiler_params=pltpu.CompilerParams(dimension_semantics=("arbitrary",), vmem_limit_bytes=VMEM_LIMIT_BYTES),
        name="inproj",
    )(x, x, meta, g, tab, w["wz"], w["wq"], w["wkv"], w["wqi"], w["wsm"], w["wgab"])


def _row_fold(x, op, group=SUBLANES):
    ways = 4
    accs = []
    for j in range(x.shape[0] // group):
        part = x[j * group:(j + 1) * group]
        if j < ways:
            accs.append(part)
        else:
            accs[j % ways] = op(accs[j % ways], part)
    while len(accs) > 1:
        accs = [op(accs[j], accs[j + 1]) for j in range(0, len(accs) - 1, 2)] + accs[len(accs) & ~1:]
    return accs[0]


def _attn_kernel(nch_ref, qis_ref, sm_ref, kid_ref, q_ref, k_ref, vt_ref, tri_ref, o_ref,
                 sc_scr, acc_scr, s_scr, *, tq, tk, topk, causal, n_keys):
    i = pl.program_id(1)
    nch = nch_ref[i]
    sub = tri_ref.shape[0]
    nq4 = KV_REP * tq
    qpos = i * tq + lax.broadcasted_iota(I32, (1, tq), 1)
    if causal:
        lim = N_META + CHUNK * (jnp.right_shift(qpos - N_META, int(math.log2(CHUNK))) + 1)
        lim = jnp.minimum(lim, n_keys)
    else:
        lim = jnp.full((1, tq), n_keys, I32)

    qstack = jnp.concatenate([qis_ref[:, hd * LANES:(hd + 1) * LANES] for hd in range(N_IDX_HEADS)], axis=0)
    smt = sm_ref[...].T
    wlane = jnp.concatenate([smt[IDX_DIM + hd:IDX_DIM + hd + 1, :] for hd in range(N_IDX_HEADS)], axis=1)

    key_row = lax.broadcasted_iota(I32, (tk, tq), 0)

    def score_body(c, st):
        smax, smin = st
        off = pl.multiple_of(c * tk, tk)
        sc = _dot_nt(kid_ref[pl.ds(off, tk), :], qstack)
        sc = jnp.maximum(sc, 0.0) * wlane
        s = sc[:, 0:tq]
        for hd in range(1, N_IDX_HEADS):
            s = s + sc[:, hd * tq:(hd + 1) * tq]
        ok = key_row < lim - off
        masked = jnp.where(ok, s, -jnp.inf)
        sc_scr[pl.ds(off, tk), :] = masked
        return (jnp.maximum(smax, _row_fold(masked, jnp.maximum)),
                jnp.minimum(smin, _row_fold(jnp.where(ok, s, jnp.inf), jnp.minimum)))

    smax, smin = lax.fori_loop(0, nch, score_body, (jnp.full((SUBLANES, tq), -jnp.inf, F32),
                                                    jnp.full((SUBLANES, tq), jnp.inf, F32)))
    smax = jnp.max(smax, axis=0, keepdims=True)
    smin = jnp.min(smin, axis=0, keepdims=True)

    def count_ge(cand):
        def hits(c, cnd):
            kb = sc_scr[pl.ds(pl.multiple_of(c * tk, tk), tk), :]
            return _row_fold(jnp.where(kb >= cnd, 1.0, 0.0), jnp.add)

        def body(c2, acc):
            second = 2 * c2 + 1
            cnd2 = jnp.where(second < nch, cand, jnp.inf)
            return acc + (hits(2 * c2, cand) + hits(jnp.minimum(second, nch - 1), cnd2))
        acc = lax.fori_loop(0, (nch + 1) // 2, body, jnp.zeros((SUBLANES, tq), F32))
        return jnp.sum(acc, axis=0, keepdims=True)

    take_all = lim <= topk
    kf = float(topk)
    lo0 = jnp.where(take_all, NEG_BIG, smin)
    hi0 = jnp.where(take_all, NEG_BIG, smax + jnp.abs(smax) * 2.0 ** -10 + F32_TINY * 2.0 ** 24)
    c_lo0 = lim.astype(F32)
    c_hi0 = jnp.where(take_all, kf, 0.0)
    closed0 = jnp.where(take_all, 1.0, 0.0)

    def search_cond(st):
        return (st[0] < 64) & (jnp.max(1.0 - st[5]) > 0.5)

    def search_body(st):
        it, lo, hi, c_lo, c_hi, closed = st
        mid = lo + 0.5 * (hi - lo)
        interp = hi - (hi - lo) * ((kf - c_hi) / jnp.maximum(c_lo - c_hi, 1.0))
        inside = lambda v: (v > lo) & (v < hi)
        cand = jnp.where((it & 1) == 0, interp, mid)
        cand = jnp.where((it == 0) & inside(0.0), 0.0, cand)
        cand = jnp.where((it == 1) & inside(F32_TINY), F32_TINY, cand)
        cand = jnp.where(inside(cand), cand, mid)
        cnt = count_ge(cand)
        is_open = closed < 0.5
        up = is_open & (cnt > kf)
        down = is_open & (cnt <= kf)
        lo = jnp.where(up, cand, lo)
        c_lo = jnp.where(up, cnt, c_lo)
        hi = jnp.where(down, cand, hi)
        c_hi = jnp.where(down, cnt, c_hi)
        done = (c_hi == kf) | jnp.logical_not(inside(lo + 0.5 * (hi - lo))) | ((lo == 0.0) & (hi == F32_TINY))
        return it + 1, lo, hi, c_lo, c_hi, jnp.where(done, 1.0, closed)

    _, thr_lo, thr_hi, _, c_hi, _ = lax.while_loop(search_cond, search_body,
                                                   (jnp.int32(0), lo0, hi0, c_lo0, c_hi0, closed0))
    n_tie = kf - c_hi

    acc_scr[...] = jnp.zeros(acc_scr.shape, F32)
    qgs = [jnp.concatenate([q_ref[:, (g * KV_REP + r) * HEAD_DIM:(g * KV_REP + r + 1) * HEAD_DIM]
                            for r in range(KV_REP)], axis=0) for g in range(N_KV_HEADS)]

    ones_rows = jnp.ones((BF16_SUBLANES, tk), BF16)

    def logits(c, slot, seen):
        off = pl.multiple_of(c * tk, tk)
        kb = sc_scr[pl.ds(off, tk), :]
        above = kb >= thr_hi
        at_least = kb >= thr_lo
        eqf = jnp.where(at_least, 1.0, 0.0) - jnp.where(above, 1.0, 0.0)
        ranks = []
        for hf in range(tk // sub):
            e = eqf[hf * sub:(hf + 1) * sub]
            ranks.append(seen + _dot(tri_ref[...], e.astype(BF16)))
            seen = seen + jnp.sum(_row_fold(e, jnp.add), axis=0, keepdims=True)
        rank = jnp.concatenate(ranks, axis=0)
        maskf = jnp.where(above, 0.0, jnp.where(at_least, jnp.where(rank < n_tie, 0.0, NEG_BIG), NEG_BIG))
        mask4 = jnp.concatenate([maskf] * KV_REP, axis=1)
        mcs = []
        for g in range(N_KV_HEADS):
            kc = k_ref[pl.ds(off, tk), g * HEAD_DIM:(g + 1) * HEAD_DIM]
            s = _dot_nt(kc, qgs[g]) + mask4
            s_scr[slot, g] = s
            mcs.append(jnp.max(_row_fold(s, jnp.maximum), axis=0, keepdims=True))
        return seen, tuple(mcs)

    def update(c, slot, ms, mcs):
        off = pl.multiple_of(c * tk, tk)
        ms_n = []
        for g in range(N_KV_HEADS):
            vtc = vt_ref[g * HEAD_DIM:(g + 1) * HEAD_DIM, pl.ds(off, tk)]
            m_new = jnp.maximum(ms[g], mcs[g])
            alpha = jnp.exp2(ms[g] - m_new)
            p = jnp.exp2(s_scr[slot, g] - m_new).astype(BF16)
            acc_scr[g] = alpha * acc_scr[g] + _dot(jnp.concatenate([vtc, ones_rows], axis=0), p)
            ms_n.append(m_new)
        return tuple(ms_n)

    def flash_body(c, st):
        seen, ms, mcs = st
        slot = c & 1
        ms = update(c, slot, ms, mcs)
        seen, mcs = logits(c + 1, 1 - slot, seen)
        return seen, ms, mcs

    m0 = tuple(jnp.full((1, nq4), NEG_BIG, F32) for _ in range(N_KV_HEADS))
    seen0, mc0 = logits(0, 0, jnp.zeros((1, tq), F32))
    _, ms, mcs = lax.fori_loop(0, nch - 1, flash_body, (seen0, m0, mc0))
    update(nch - 1, (nch - 1) & 1, ms, mcs)

    for g in range(N_KV_HEADS):
        og = acc_scr[g, 0:HEAD_DIM] / acc_scr[g, HEAD_DIM:HEAD_DIM + 1]
        for r in range(KV_REP):
            hd = g * KV_REP + r
            o_ref[:, hd * HEAD_DIM:(hd + 1) * HEAD_DIM] = og[:, r * tq:(r + 1) * tq].T.astype(o_ref.dtype)


def _attention(qis, sm, kid, q, kb, vt, *, tq, topk, causal, n_keys):
    tk, sub = ATTN_TK, TIE_BLOCK
    bsz, sq, _ = q.shape
    sk = kb.shape[1]
    assert sq % tq == 0 and sk % tk == 0
    nq = sq // tq
    if causal:
        ends = np.minimum(N_META + CHUNK * ((np.arange(nq) * tq + tq - 1 - N_META) // CHUNK + 1), n_keys)
    else:
        ends = np.full((nq,), n_keys)
    nch = jnp.asarray(-(-ends // tk), I32)
    tri = jnp.asarray(np.tril(np.ones((sub, sub), np.float32), -1), BF16)
    qrow = lambda n: pl.BlockSpec((None, tq, n), lambda b, i, s: (b, i, 0))
    krow = lambda n: pl.BlockSpec((None, sk, n), lambda b, i, s: (b, 0, 0), pipeline_mode=pl.Buffered(1))
    kern = functools.partial(_attn_kernel, tq=tq, tk=tk, topk=topk, causal=causal, n_keys=n_keys)
    return pl.pallas_call(
        kern,
        grid_spec=pltpu.PrefetchScalarGridSpec(
            num_scalar_prefetch=1,
            grid=(bsz, nq),
            in_specs=[qrow(qis.shape[2]), qrow(sm.shape[2]), krow(kid.shape[2]), qrow(q.shape[2]),
                      krow(kb.shape[2]),
                      pl.BlockSpec((vt.shape[0], sk), lambda b, i, s: (0, b), pipeline_mode=pl.Buffered(1)),
                      pl.BlockSpec((sub, sub), lambda b, i, s: (0, 0))],
            out_specs=qrow(q.shape[2]),
            scratch_shapes=[pltpu.VMEM((sk, tq), F32),
                            pltpu.VMEM((N_KV_HEADS, HEAD_DIM + BF16_SUBLANES, KV_REP * tq), F32),
                            pltpu.VMEM((2, N_KV_HEADS, tk, KV_REP * tq), F32)]),
        out_shape=jax.ShapeDtypeStruct(q.shape, BF16),
        compiler_params=pltpu.CompilerParams(dimension_semantics=("arbitrary", "arbitrary"),
                                             vmem_limit_bytes=VMEM_LIMIT_BYTES),
        name="attn",
    )(nch, qis, sm, kid, q, kb, vt, tri)


def _mix_kernel(tb_ref, o_ref, z_ref, halo_ref, sga_ref, sgb_ref, x_ref, wpool_ref, ps_ref, wao_ref, wout_ref,
                x1_ref, zc_scr, *, tm):
    hs = halo_ref.shape[0]
    zc_scr[0:hs, :] = halo_ref[...]
    zc_scr[hs:hs + tm, :] = z_ref[...]
    avail = tb_ref[pl.program_id(0)] + lax.broadcasted_iota(I32, (tm, 1), 0) + 1
    gw = z_ref.shape[1] // len(POOL_WINDOWS)
    ys = []
    for g, win in enumerate(POOL_WINDOWS):
        cols = slice(g * gw, (g + 1) * gw)
        zt = zc_scr[hs:hs + tm, cols]
        s = zt
        for j in range(1, win):
            s = s + zc_scr[hs - j:hs - j + tm, cols]
        cnt = jnp.minimum(avail, win).astype(F32)
        dlt = s / cnt - zt
        ys.append(_dot(dlt.astype(BF16), wpool_ref[g]))
    a = jnp.concatenate(ys, axis=1) * ps_ref[...]
    b = _dot(o_ref[...], wao_ref[...])
    mrg = sga_ref[...] * a + sgb_ref[...] * b
    x1_ref[...] = x_ref[...] + _dot(mrg.astype(BF16), wout_ref[...])


def _mix(tbase, o, z, halo, sga, sgb, x, w, *, tm):
    rows, d = x.shape
    nt = rows // tm
    hs = halo.shape[1]
    row = lambda n: pl.BlockSpec((tm, n), lambda i, s: (i, 0))
    const = lambda a: pl.BlockSpec(a.shape, lambda i, s: (0,) * a.ndim)
    return pl.pallas_call(
        functools.partial(_mix_kernel, tm=tm),
        grid_spec=pltpu.PrefetchScalarGridSpec(
            num_scalar_prefetch=1,
            grid=(nt,),
            in_specs=[row(d), row(d), pl.BlockSpec((None, hs, d), lambda i, s: (i, 0, 0)), row(d), row(d), row(d),
                      const(w["wpool"]), const(w["pscale"]), const(w["wao"]), const(w["wout"])],
            out_specs=row(d),
            scratch_shapes=[pltpu.VMEM((hs + tm, d), F32)]),
        out_shape=jax.ShapeDtypeStruct((rows, d), F32),
        compiler_params=pltpu.CompilerParams(dimension_semantics=("arbitrary",), vmem_limit_bytes=VMEM_LIMIT_BYTES),
        name="mix",
    )(tbase, o, z, halo, sga, sgb, x, w["wpool"], w["pscale"], w["wao"], w["wout"])


ROUTE_GROUP_LANE0 = N_EXPERTS


def _moe_kernel(x_ref, nxt_ref, gf_ref, gl_ref, wrh_ref, wrl_ref, br_ref, wg_ref, wu_ref, wd_ref, y_ref,
                x_scr, h_scr, route_scr, acc_scr, *, ec, lead):
    j = pl.program_id(1)
    tm = x_ref.shape[0]

    @pl.when(j == 0)
    def _():
        x_scr[...] = jnp.concatenate([x_ref[lead:, :], nxt_ref[...]], axis=0) if lead else x_ref[...]
        hf = _rms(x_scr[...], gf_ref[...])
        h_hi = hf.astype(BF16)
        h_lo = (hf - h_hi.astype(F32)).astype(BF16)
        h_scr[...] = h_hi
        logits = _dot(h_hi, wrh_ref[...]) + _dot(h_lo, wrh_ref[...]) + _dot(h_hi, wrl_ref[...]) + br_ref[...]
        lane = lax.broadcasted_iota(I32, (tm, LANES), 1)
        lanef = lane.astype(F32)
        ninf = -jnp.inf
        isg = (lane >= ROUTE_GROUP_LANE0) & (lane < ROUTE_GROUP_LANE0 + N_EXPERT_GROUPS)
        gl = jnp.where(isg, logits, ninf)
        gmax = jnp.max(gl, axis=1, keepdims=True)
        gsel = jnp.min(jnp.where(gl == gmax, lanef, 1e9), axis=1, keepdims=True)
        gprob = 1.0 / jnp.sum(jnp.exp(gl - gmax), axis=1, keepdims=True)
        e0 = (gsel - ROUTE_GROUP_LANE0) * EXPERTS_PER_GROUP
        el = jnp.where((lanef >= e0) & (lanef < e0 + EXPERTS_PER_GROUP), logits, ninf)
        v1 = jnp.max(el, axis=1, keepdims=True)
        i1 = jnp.min(jnp.where(el == v1, lanef, 1e9), axis=1, keepdims=True)
        el2 = jnp.where(lanef == i1, ninf, el)
        v2 = jnp.max(el2, axis=1, keepdims=True)
        i2 = jnp.min(jnp.where(el2 == v2, lanef, 1e9), axis=1, keepdims=True)
        e = jnp.exp(v2 - v1)
        w1 = (1.0 / (1.0 + e)) * gprob
        w2 = (e / (1.0 + e)) * gprob
        for n, val in enumerate((i1, i2, w1, w2)):
            route_scr[n] = jnp.broadcast_to(val, (tm, LANES))
        acc_scr[...] = jnp.zeros(acc_scr.shape, F32)

    h = h_scr[...]
    i1, i2, w1, w2 = route_scr[0], route_scr[1], route_scr[2], route_scr[3]
    gate = _dot(h, wg_ref[...])
    up = _dot(h, wu_ref[...])
    act = gate * jax.nn.sigmoid(gate) * up
    parts = []
    for q in range(ec):
        ef = jnp.full((tm, LANES), j * ec + q, I32).astype(F32)
        comb = jnp.where(i1 == ef, w1, 0.0) + jnp.where(i2 == ef, w2, 0.0)
        parts.append(act[:, q * D_EXPERT:(q + 1) * D_EXPERT] * comb)
    acc_scr[...] += _dot(jnp.concatenate(parts, axis=1).astype(BF16), wd_ref[...])

    @pl.when(j == pl.num_programs(1) - 1)
    def _():
        y_ref[...] = _rms(x_scr[...] + acc_scr[...], gl_ref[...])


def _moe(x, w, *, tm, n_out, lead, ec=4):
    bsz, s, d = x.shape
    assert n_out % tm == 0 and n_out + lead <= s
    nt = n_out // tm
    ne = N_EXPERTS // ec
    cw = ec * D_EXPERT
    lb = max(lead, SUBLANES)
    const = lambda a: pl.BlockSpec(a.shape, lambda i, j: (0,) * a.ndim)
    tile = pl.BlockSpec((None, tm, d), lambda i, j: (i // nt, i % nt, 0))
    after = pl.BlockSpec((None, lb, d), lambda i, j: (i // nt, jnp.minimum((i % nt + 1) * (tm // lb), s // lb - 1), 0))
    return pl.pallas_call(
        functools.partial(_moe_kernel, ec=ec, lead=lead),
        grid=(bsz * nt, ne),
        in_specs=[tile, after, const(w["gffn"]), const(w["gfinal"]),
                  const(w["wr_hi"]), const(w["wr_lo"]), const(w["br"]),
                  pl.BlockSpec((d, cw), lambda i, j: (0, j)), pl.BlockSpec((d, cw), lambda i, j: (0, j)),
                  pl.BlockSpec((cw, d), lambda i, j: (j, 0))],
        out_specs=tile,
        out_shape=jax.ShapeDtypeStruct((bsz, n_out, d), F32),
        scratch_shapes=[pltpu.VMEM((tm, d), F32), pltpu.VMEM((tm, d), BF16), pltpu.VMEM((4, tm, LANES), F32),
                        pltpu.VMEM((tm, d), F32)],
        compiler_params=pltpu.CompilerParams(dimension_semantics=("arbitrary", "arbitrary"),
                                             vmem_limit_bytes=VMEM_LIMIT_BYTES),
        name="moe",
    )(x, x, w["gffn"], w["gfinal"], w["wr_hi"], w["wr_lo"], w["br"], w["wg"], w["wu"], w["wd"])


def _rope_table(pos):
    posf = pos.astype(F32)[:, None]

    def cs(half):
        inv = jnp.exp(-math.log(ROPE_THETA) * jnp.arange(half, dtype=F32) * (1.0 / half))
        ang = posf * inv[None, :]
        return jnp.cos(ang), jnp.sin(ang)

    cq, sq = cs(QK_HALF)
    ci, si = cs(IX_HALF)
    used = 2 * QK_HALF + 2 * IX_HALF
    return jnp.concatenate([cq, sq, ci, si, jnp.zeros((pos.shape[0], LANES - used), F32)], axis=1)


def _prep_weights(l, norm_mix_g, norm_ffn_g, norm_final_g, w_in, w_pool, pool_scale, w_attn_out, w_out,
                  w_router_group, b_router_group, w_router_expert, b_router_expert,
                  w_expert_gate, w_expert_up, w_expert_down):
    d = w_in.shape[1]
    pw = w_pool.shape[1] * w_pool.shape[2]
    qw = N_HEADS * HEAD_DIM
    kvw = N_KV_HEADS * HEAD_DIM
    widths = (pw, qw, kvw, kvw, N_IDX_HEADS * IDX_DIM, IDX_DIM, N_IDX_HEADS, d, d)
    offs = np.concatenate([[0], np.cumsum(widths)])
    wi = w_in[l]
    piece = lambda a, b: wi[:, offs[a]:offs[b]].astype(BF16)
    wsm = jnp.concatenate([wi[:, offs[5]:offs[7]], jnp.zeros((d, LANES - IDX_DIM - N_IDX_HEADS), F32)], axis=1)
    wr = jnp.concatenate([w_router_expert[l], w_router_group[l],
                          jnp.zeros((d, LANES - N_EXPERTS - N_EXPERT_GROUPS), F32)], axis=1)
    wr_hi = wr.astype(BF16)
    br = jnp.concatenate([b_router_expert[l], b_router_group[l],
                          jnp.zeros((LANES - N_EXPERTS - N_EXPERT_GROUPS,), F32)])[None, :]
    flat_e = lambda a: a.transpose(1, 0, 2).reshape(d, N_EXPERTS * D_EXPERT).astype(BF16)
    return dict(
        gmix=norm_mix_g[l][None, :], gffn=norm_ffn_g[l][None, :], gfinal=norm_final_g[None, :],
        wz=piece(0, 1), wq=piece(1, 2), wkv=piece(2, 4), wqi=piece(4, 5), wsm=wsm.astype(BF16), wgab=piece(7, 9),
        wpool=w_pool[l].astype(BF16), pscale=pool_scale[l][None, :],
        wao=w_attn_out[l].astype(BF16), wout=w_out[l].astype(BF16),
        wr_hi=wr_hi, wr_lo=(wr - wr_hi.astype(F32)).astype(BF16), br=br,
        wg=flat_e(w_expert_gate[l]), wu=flat_e(w_expert_up[l]),
        wd=w_expert_down[l].reshape(N_EXPERTS * D_EXPERT, d).astype(BF16),
    )


def _dup_lanes(ki):
    return jnp.concatenate([ki, ki], axis=-1)


def kernel(x_prompt, x_sample, cache_k, cache_v, cache_idx_k, state_pool, meta_tokens, norm_mix_g, norm_ffn_g,
           norm_final_g, w_in, w_pool, pool_scale, w_attn_out, w_out, w_router_group, b_router_group,
           w_router_expert, b_router_expert, w_expert_gate, w_expert_up, w_expert_down):
    bsz, seq, d = x_prompt.shape
    dbsz, dec, _ = x_sample.shape
    assert w_in.shape[0] == 1, "single-layer step"
    l = 0
    past = cache_k.shape[2] - N_META
    n = N_META + seq
    topk_p = min(IDX_TOPK, seq // 4)
    topk_s = min(IDX_TOPK, (past + dec) // 4)
    kvw = N_KV_HEADS * HEAD_DIM

    tk = ATTN_TK
    tm = tk
    assert seq % tm == 0 and N_META <= tm
    sp = seq + tm
    nt = sp // tm
    tm_moe = _pick_tile(seq, 1100)
    w = _prep_weights(l, norm_mix_g, norm_ffn_g, norm_final_g, w_in, w_pool, pool_scale, w_attn_out, w_out,
                      w_router_group, b_router_group, w_router_expert, b_router_expert,
                      w_expert_gate, w_expert_up, w_expert_down)

    tab_p = _rope_table(jnp.arange(sp))
    xp, z, q, k_c, v_c, kb, vt, qis, sm, kid, sga, sgb = _inproj(
        x_prompt, meta_tokens.astype(F32), w["gmix"], tab_p, w, tm=tm, n_out=n, lead=N_META)
    b3 = lambda a: a.reshape(bsz, sp, a.shape[-1])
    o = _attention(b3(qis), b3(sm), b3(kid), b3(q), b3(kb), vt, tq=ATTN_TQ_PROMPT, topk=topk_p, causal=True, n_keys=n)
    hs = POOL_STATE + 1
    z4 = z.reshape(bsz, nt, tm, d)
    halo = jnp.concatenate([jnp.zeros((bsz, 1, hs, d), F32), z4[:, :-1, tm - hs:, :]], axis=1).reshape(bsz * nt, hs, d)
    tbase = jnp.asarray(np.tile(np.arange(nt) * tm, bsz), I32)
    x1 = _mix(tbase, o.reshape(bsz * sp, -1), z, halo, sga, sgb, xp, w, tm=tm)
    y_prompt = _moe(b3(x1), w, tm=tm_moe, n_out=seq, lead=N_META)
    k_prompt = k_c[None]
    v_prompt = v_c[None]
    idx_k_prompt = b3(sm)[:, :n, :IDX_DIM][None]
    pool_prompt = b3(z)[:, n - POOL_STATE:n][None]

    rows_s = dbsz * dec
    tab_s = jnp.tile(_rope_table(N_META + past + jnp.arange(dec)), (dbsz, 1))
    xs = x_sample.reshape(rows_s, d)
    _, zs, qs, ks_c, vs_c, kbs, _, qiss, sms, kids, sgas, sgbs = _inproj(
        xs[None], jnp.zeros((SUBLANES, d), F32), w["gmix"], tab_s, w, tm=rows_s, n_out=rows_s, lead=0)
    vs = vs_c.reshape(rows_s, kvw)
    nk = N_META + past + dec
    sk = _round_up(nk, tk)
    sq_s = _round_up(dec, ATTN_TQ)
    s3 = lambda a: a.reshape(dbsz, dec, a.shape[-1])
    padr = lambda a, rows: jnp.concatenate([a, jnp.zeros((dbsz, rows - a.shape[1], a.shape[-1]), a.dtype)], axis=1)
    k_all = padr(jnp.concatenate([cache_k[l].reshape(dbsz, N_META + past, kvw).astype(BF16), s3(kbs)], axis=1), sk)
    v_all = padr(jnp.concatenate([cache_v[l].reshape(dbsz, N_META + past, kvw).astype(BF16),
                                  s3(vs).astype(BF16)], axis=1), sk)
    vt_all = v_all.transpose(2, 0, 1).reshape(kvw, dbsz * sk)
    ki_all = padr(jnp.concatenate([_dup_lanes(cache_idx_k[l]).astype(BF16), s3(kids)], axis=1), sk)
    os_ = _attention(padr(s3(qiss), sq_s), padr(s3(sms), sq_s), ki_all, padr(s3(qs), sq_s), k_all, vt_all,
                     tq=ATTN_TQ, topk=topk_s, causal=False, n_keys=nk)[:, :dec]
    zcat = jnp.concatenate([state_pool[l].astype(F32), s3(zs)], axis=1)
    halo_s = jnp.concatenate([jnp.zeros((dbsz, 1, d), F32), zcat[:, :POOL_STATE]], axis=1)
    tbase_s = jnp.full((dbsz,), POOL_STATE, I32)
    x1s = _mix(tbase_s, os_.reshape(rows_s, -1), zs, halo_s, sgas, sgbs, xs, w, tm=dec)
    y_sample = _moe(x1s[None], w, tm=rows_s, n_out=rows_s, lead=0).reshape(dbsz, dec, d)
    k_sample = ks_c.reshape(1, dbsz, dec, N_KV_HEADS, HEAD_DIM)
    v_sample = vs_c.reshape(1, dbsz, dec, N_KV_HEADS, HEAD_DIM)
    idx_k_sample = s3(sms)[:, :, :IDX_DIM][None]
    pool_sample = zcat[:, zcat.shape[1] - POOL_STATE:][None]

    return (y_prompt, y_sample, k_prompt, v_prompt, idx_k_prompt, pool_prompt,
            k_sample, v_sample, idx_k_sample, pool_sample)
```

```python
import functools
import math

import numpy as np
import jax
import jax.numpy as jnp
from jax import lax
from jax.experimental import pallas as pl
from jax.experimental.pallas import tpu as pltpu

CHUNK = 64
N_META = 16
EPS = 1e-6
POOL_WINDOWS = (2, 4, 8, 16)
POOL_STATE = max(POOL_WINDOWS) - 1
N_HEADS = 8
N_KV_HEADS = 2
HEAD_DIM = 128
KV_REP = N_HEADS // N_KV_HEADS
N_IDX_HEADS = 4
IDX_DIM = 64
IDX_TOPK = 256
IDX_W_SCALE = (N_IDX_HEADS * IDX_DIM) ** -0.5
ATTN_SCALE = HEAD_DIM ** -0.5
Q_PRESCALE = ATTN_SCALE * math.log2(math.e)
ROPE_THETA = 500000.0
ROT_FRACTION = 4
N_EXPERT_GROUPS = 4
EXPERTS_PER_GROUP = 8
N_EXPERTS = N_EXPERT_GROUPS * EXPERTS_PER_GROUP
D_EXPERT = 128

LANES = 128
SUBLANES = 8
BF16_SUBLANES = 16
VMEM_LIMIT_BYTES = 56 * 1024 * 1024

QK_HALF = HEAD_DIM // ROT_FRACTION // 2
IX_HALF = IDX_DIM // ROT_FRACTION // 2
TAB_COS_QK = 0
TAB_SIN_QK = QK_HALF
TAB_COS_IX = 2 * QK_HALF
TAB_SIN_IX = 2 * QK_HALF + IX_HALF

ATTN_TQ = LANES
ATTN_TQ_PROMPT = 2 * LANES
ATTN_TK = 512
TIE_BLOCK = 256
F32_TINY = float(np.finfo(np.float32).tiny)
NEG_BIG = -1e30

F32 = jnp.float32
BF16 = jnp.bfloat16
I32 = jnp.int32


def _round_up(x, m):
    return -(-x // m) * m


def _pick_tile(rows, cap):
    return max(t for t in range(BF16_SUBLANES, cap + 1, BF16_SUBLANES) if rows % t == 0)


def _rms(x, g):
    return x * lax.rsqrt(jnp.mean(x * x, axis=-1, keepdims=True) + EPS) * g


def _dot(a, b):
    return jnp.dot(a, b, preferred_element_type=F32)


def _dot_nt(a, b):
    return lax.dot_general(a, b, (((1,), (1,)), ((), ())), preferred_element_type=F32)


def _rope(y, c, a, b, shift_a, shift_b):
    outs = []
    for j in range(y.shape[1] // LANES):
        ys = y[:, j * LANES:(j + 1) * LANES]
        outs.append(ys * c + pltpu.roll(ys, shift_a, 1) * a + pltpu.roll(ys, shift_b, 1) * b)
    return outs[0] if len(outs) == 1 else jnp.concatenate(outs, axis=1)


def _inproj_kernel(x_ref, prev_ref, meta_ref, g_ref, tab_ref, wz_ref, wq_ref, wkv_ref, wqi_ref, wsm_ref, wgab_ref,
                   xs_ref, z_ref, q_ref, k4_ref, v4_ref, kb_ref, vt_ref, qis_ref, sm_ref, kid_ref, sga_ref, sgb_ref,
                   *, lead, frame_tiles):
    tm, d = x_ref.shape
    if lead:
        ti = pl.program_id(0) % (frame_tiles + 1)
        head = jnp.where(ti == 0, meta_ref[...], prev_ref[...])
        body = jnp.where(ti < frame_tiles, x_ref[0:tm - lead, :], 0.0)
        x = jnp.concatenate([head, body], axis=0)
    else:
        x = x_ref[...]
    xs_ref[...] = x
    h = _rms(x, g_ref[...]).astype(BF16)
    lane = lax.broadcasted_iota(I32, (tm, LANES), 1)
    t = tab_ref[...]

    def rl(s):
        return pltpu.roll(t, s % LANES, 1) if s % LANES else t

    c_qk = jnp.where(lane < QK_HALF, rl(-TAB_COS_QK), jnp.where(lane < 2 * QK_HALF, rl(QK_HALF - TAB_COS_QK), 1.0))
    a_qk = jnp.where(lane < QK_HALF, -rl(-TAB_SIN_QK), 0.0)
    b_qk = jnp.where((lane >= QK_HALF) & (lane < 2 * QK_HALF), rl(QK_HALF - TAB_SIN_QK), 0.0)
    l64 = lane & (IDX_DIM - 1)
    hi = lane >= IDX_DIM
    cos_lo = jnp.where(hi, rl(IDX_DIM - TAB_COS_IX), rl(-TAB_COS_IX))
    cos_hi = jnp.where(hi, rl(IDX_DIM + IX_HALF - TAB_COS_IX), rl(IX_HALF - TAB_COS_IX))
    sin_lo = jnp.where(hi, rl(IDX_DIM - TAB_SIN_IX), rl(-TAB_SIN_IX))
    sin_hi = jnp.where(hi, rl(IDX_DIM + IX_HALF - TAB_SIN_IX), rl(IX_HALF - TAB_SIN_IX))
    c_ix = jnp.where(l64 < IX_HALF, cos_lo, jnp.where(l64 < 2 * IX_HALF, cos_hi, 1.0))
    a_ix = jnp.where(l64 < IX_HALF, -sin_lo, 0.0)
    b_ix = jnp.where((l64 >= IX_HALF) & (l64 < 2 * IX_HALF), sin_hi, 0.0)

    z_ref[...] = _dot(h, wz_ref[...])

    q = _rope(_dot(h, wq_ref[...]), c_qk, a_qk, b_qk, LANES - QK_HALF, QK_HALF)
    q_ref[...] = (q * Q_PRESCALE).astype(BF16)

    kv = _dot(h, wkv_ref[...])
    kvw = kv.shape[1] // 2
    k = _rope(kv[:, :kvw], c_qk, a_qk, b_qk, LANES - QK_HALF, QK_HALF)
    v = kv[:, kvw:]
    for g in range(N_KV_HEADS):
        k4_ref[:, g, :] = k[:, g * HEAD_DIM:(g + 1) * HEAD_DIM]
        v4_ref[:, g, :] = v[:, g * HEAD_DIM:(g + 1) * HEAD_DIM]
    kb_ref[...] = k.astype(BF16)
    vt_ref[...] = v.T.astype(BF16)

    qi = _rope(_dot(h, wqi_ref[...]), c_ix, a_ix, b_ix, LANES - IX_HALF, IX_HALF)
    pieces = []
    for hd in range(N_IDX_HEADS):
        col = qi[:, (hd // 2) * LANES:(hd // 2 + 1) * LANES]
        keep = hi if hd % 2 else jnp.logical_not(hi)
        pieces.append(jnp.where(keep, col, 0.0))
    qis_ref[...] = jnp.concatenate(pieces, axis=1).astype(BF16)

    c_sm = jnp.where(hi, IDX_W_SCALE, c_ix)
    a_sm = jnp.where(hi, 0.0, a_ix)
    b_sm = jnp.where(hi, 0.0, b_ix)
    sm = _rope(_dot(h, wsm_ref[...]), c_sm, a_sm, b_sm, LANES - IX_HALF, IX_HALF)
    sm_ref[...] = sm
    kid_ref[...] = jnp.where(hi, pltpu.roll(sm, IDX_DIM, 1), sm).astype(BF16)

    gab = _dot(h, wgab_ref[...])
    sga_ref[...] = jax.nn.sigmoid(gab[:, :d])
    sgb_ref[...] = jax.nn.sigmoid(gab[:, d:])


def _inproj(x, meta, g, tab, w, *, tm, n_out, lead):
    bsz, s, d = x.shape
    assert s % tm == 0
    ft = s // tm
    nt = ft + 1 if lead else ft
    rows = bsz * nt * tm
    qw = w["wq"].shape[1]
    kvw = w["wkv"].shape[1] // 2
    lb = max(lead, SUBLANES)
    row_spec = lambda n: pl.BlockSpec((tm, n), lambda i: (i, 0))
    const = lambda a: pl.BlockSpec(a.shape, lambda i: (0,) * a.ndim)
    cache_spec = pl.BlockSpec((None, tm, N_KV_HEADS, HEAD_DIM), lambda i: (i // nt, i % nt, 0, 0))
    cache_shape = jax.ShapeDtypeStruct((bsz, n_out, N_KV_HEADS, HEAD_DIM), F32)
    outs = [
        (d, F32),
        (d, F32),
        (qw, BF16),
        "cache", "cache",
        (kvw, BF16),
        "vt",
        (N_IDX_HEADS * LANES, BF16),
        (LANES, F32),
        (LANES, BF16),
        (d, F32), (d, F32),
    ]
    spec_of = lambda o: (cache_spec if o == "cache" else pl.BlockSpec((kvw, tm), lambda i: (0, i)) if o == "vt"
                         else row_spec(o[0]))
    shape_of = lambda o: (cache_shape if o == "cache" else jax.ShapeDtypeStruct((kvw, rows), BF16) if o == "vt"
                          else jax.ShapeDtypeStruct((rows, o[0]), o[1]))
    return pl.pallas_call(
        functools.partial(_inproj_kernel, lead=lead, frame_tiles=ft),
        grid=(bsz * nt,),
        in_specs=[pl.BlockSpec((None, tm, d), lambda i: (i // nt, jnp.minimum(i % nt, ft - 1), 0)),
                  pl.BlockSpec((None, lb, d), lambda i: (i // nt, jnp.maximum((i % nt) * (tm // lb) - 1, 0), 0)),
                  const(meta), const(g),
                  pl.BlockSpec((tm, LANES), lambda i: (i % nt, 0)),
                  const(w["wz"]), const(w["wq"]), const(w["wkv"]), const(w["wqi"]), const(w["wsm"]), const(w["wgab"])],
        out_specs=[spec_of(o) for o in outs],
        out_shape=[shape_of(o) for o in outs],
        compiler_params=pltpu.CompilerParams(dimension_semantics=("arbitrary",), vmem_limit_bytes=VMEM_LIMIT_BYTES),
        name="inproj",
    )(x, x, meta, g, tab, w["wz"], w["wq"], w["wkv"], w["wqi"], w["wsm"], w["wgab"])


def _row_fold(x, op, group=SUBLANES):
    ways = 4
    accs = []
    for j in range(x.shape[0] // group):
        part = x[j * group:(j + 1) * group]
        if j < ways:
            accs.append(part)
        else:
            accs[j % ways] = op(accs[j % ways], part)
    while len(accs) > 1:
        accs = [op(accs[j], accs[j + 1]) for j in range(0, len(accs) - 1, 2)] + accs[len(accs) & ~1:]
    return accs[0]


def _attn_kernel(nch_ref, qis_ref, sm_ref, kid_ref, q_ref, k_ref, vt_ref, tri_ref, o_ref,
                 sc_scr, acc_scr, s_scr, *, tq, tk, topk, causal, n_keys):
    i = pl.program_id(1)
    nch = nch_ref[i]
    sub = tri_ref.shape[0]
    nq4 = KV_REP * tq
    qpos = i * tq + lax.broadcasted_iota(I32, (1, tq), 1)
    if causal:
        lim = N_META + CHUNK * (jnp.right_shift(qpos - N_META, int(math.log2(CHUNK))) + 1)
        lim = jnp.minimum(lim, n_keys)
    else:
        lim = jnp.full((1, tq), n_keys, I32)

    qstack = jnp.concatenate([qis_ref[:, hd * LANES:(hd + 1) * LANES] for hd in range(N_IDX_HEADS)], axis=0)
    smt = sm_ref[...].T
    wlane = jnp.concatenate([smt[IDX_DIM + hd:IDX_DIM + hd + 1, :] for hd in range(N_IDX_HEADS)], axis=1)

    key_row = lax.broadcasted_iota(I32, (tk, tq), 0)

    def score_body(c, st):
        smax, smin = st
        off = pl.multiple_of(c * tk, tk)
        sc = _dot_nt(kid_ref[pl.ds(off, tk), :], qstack)
        sc = jnp.maximum(sc, 0.0) * wlane
        s = sc[:, 0:tq]
        for hd in range(1, N_IDX_HEADS):
            s = s + sc[:, hd * tq:(hd + 1) * tq]
        ok = key_row < lim - off
        masked = jnp.where(ok, s, -jnp.inf)
        sc_scr[pl.ds(off, tk), :] = masked
        return (jnp.maximum(smax, _row_fold(masked, jnp.maximum)),
                jnp.minimum(smin, _row_fold(jnp.where(ok, s, jnp.inf), jnp.minimum)))

    smax, smin = lax.fori_loop(0, nch, score_body, (jnp.full((SUBLANES, tq), -jnp.inf, F32),
                                                    jnp.full((SUBLANES, tq), jnp.inf, F32)))
    smax = jnp.max(smax, axis=0, keepdims=True)
    smin = jnp.min(smin, axis=0, keepdims=True)

    def count_ge(cand):
        def hits(c, cnd):
            kb = sc_scr[pl.ds(pl.multiple_of(c * tk, tk), tk), :]
            return _row_fold(jnp.where(kb >= cnd, 1.0, 0.0), jnp.add)

        def body(c2, acc):
            second = 2 * c2 + 1
            cnd2 = jnp.where(second < nch, cand, jnp.inf)
            return acc + (hits(2 * c2, cand) + hits(jnp.minimum(second, nch - 1), cnd2))
        acc = lax.fori_loop(0, (nch + 1) // 2, body, jnp.zeros((SUBLANES, tq), F32))
        return jnp.sum(acc, axis=0, keepdims=True)

    take_all = lim <= topk
    kf = float(topk)
    lo0 = jnp.where(take_all, NEG_BIG, smin)
    hi0 = jnp.where(take_all, NEG_BIG, smax + jnp.abs(smax) * 2.0 ** -10 + F32_TINY * 2.0 ** 24)
    c_lo0 = lim.astype(F32)
    c_hi0 = jnp.where(take_all, kf, 0.0)
    closed0 = jnp.where(take_all, 1.0, 0.0)

    def search_cond(st):
        return (st[0] < 64) & (jnp.max(1.0 - st[5]) > 0.5)

    def search_body(st):
        return search_step(search_step(st))

    def search_step(st):
        it, lo, hi, c_lo, c_hi, closed = st
        mid = lo + 0.5 * (hi - lo)
        interp = hi - (hi - lo) * ((kf - c_hi) / jnp.maximum(c_lo - c_hi, 1.0))
        inside = lambda v: (v > lo) & (v < hi)
        cand = jnp.where((it & 1) == 0, interp, mid)
        cand = jnp.where((it == 0) & inside(0.0), 0.0, cand)
        cand = jnp.where((it == 1) & inside(F32_TINY), F32_TINY, cand)
        cand = jnp.where(inside(cand), cand, mid)
        cnt = count_ge(cand)
        is_open = closed < 0.5
        up = is_open & (cnt > kf)
        down = is_open & (cnt <= kf)
        lo = jnp.where(up, cand, lo)
        c_lo = jnp.where(up, cnt, c_lo)
        hi = jnp.where(down, cand, hi)
        c_hi = jnp.where(down, cnt, c_hi)
        done = (c_hi == kf) | jnp.logical_not(inside(lo + 0.5 * (hi - lo))) | ((lo == 0.0) & (hi == F32_TINY))
        return it + 1, lo, hi, c_lo, c_hi, jnp.where(done, 1.0, closed)

    _, thr_lo, thr_hi, _, c_hi, _ = lax.while_loop(search_cond, search_body,
                                                   (jnp.int32(0), lo0, hi0, c_lo0, c_hi0, closed0))
    n_tie = kf - c_hi

    acc_scr[...] = jnp.zeros(acc_scr.shape, F32)
    qgs = [jnp.concatenate([q_ref[:, (g * KV_REP + r) * HEAD_DIM:(g * KV_REP + r + 1) * HEAD_DIM]
                            for r in range(KV_REP)], axis=0) for g in range(N_KV_HEADS)]

    ones_rows = jnp.ones((BF16_SUBLANES, tk), BF16)

    def logits(c, slot, seen):
        off = pl.multiple_of(c * tk, tk)
        kb = sc_scr[pl.ds(off, tk), :]
        above = kb >= thr_hi
        at_least = kb >= thr_lo
        eqf = jnp.where(at_least, 1.0, 0.0) - jnp.where(above, 1.0, 0.0)
        ranks = []
        for hf in range(tk // sub):
            e = eqf[hf * sub:(hf + 1) * sub]
            ranks.append(seen + _dot(tri_ref[...], e.astype(BF16)))
            seen = seen + jnp.sum(_row_fold(e, jnp.add), axis=0, keepdims=True)
        rank = jnp.concatenate(ranks, axis=0)
        maskf = jnp.where(above, 0.0, jnp.where(at_least, jnp.where(rank < n_tie, 0.0, NEG_BIG), NEG_BIG))
        mask4 = jnp.concatenate([maskf] * KV_REP, axis=1)
        mcs = []
        for g in range(N_KV_HEADS):
            kc = k_ref[pl.ds(off, tk), g * HEAD_DIM:(g + 1) * HEAD_DIM]
            s = _dot_nt(kc, qgs[g]) + mask4
            s_scr[slot, g] = s
            mcs.append(jnp.max(_row_fold(s, jnp.maximum), axis=0, keepdims=True))
        return seen, tuple(mcs)

    def update(c, slot, ms, mcs):
        off = pl.multiple_of(c * tk, tk)
        ms_n = []
        for g in range(N_KV_HEADS):
            vtc = vt_ref[g * HEAD_DIM:(g + 1) * HEAD_DIM, pl.ds(off, tk)]
            m_new = jnp.maximum(ms[g], mcs[g])
            alpha = jnp.exp2(ms[g] - m_new)
            p = jnp.exp2(s_scr[slot, g] - m_new).astype(BF16)
            acc_scr[g] = alpha * acc_scr[g] + _dot(jnp.concatenate([vtc, ones_rows], axis=0), p)
            ms_n.append(m_new)
        return tuple(ms_n)

    def flash_body(c, st):
        seen, ms, mcs = st
        slot = c & 1
        ms = update(c, slot, ms, mcs)
        seen, mcs = logits(c + 1, 1 - slot, seen)
        return seen, ms, mcs

    m0 = tuple(jnp.full((1, nq4), NEG_BIG, F32) for _ in range(N_KV_HEADS))
    seen0, mc0 = logits(0, 0, jnp.zeros((1, tq), F32))
    _, ms, mcs = lax.fori_loop(0, nch - 1, flash_body, (seen0, m0, mc0))
    update(nch - 1, (nch - 1) & 1, ms, mcs)

    for g in range(N_KV_HEADS):
        og = acc_scr[g, 0:HEAD_DIM] / acc_scr[g, HEAD_DIM:HEAD_DIM + 1]
        for r in range(KV_REP):
            hd = g * KV_REP + r
            o_ref[:, hd * HEAD_DIM:(hd + 1) * HEAD_DIM] = og[:, r * tq:(r + 1) * tq].T.astype(o_ref.dtype)


def _attention(qis, sm, kid, q, kb, vt, *, tq, topk, causal, n_keys):
    tk, sub = ATTN_TK, TIE_BLOCK
    bsz, sq, _ = q.shape
    sk = kb.shape[1]
    assert sq % tq == 0 and sk % tk == 0
    nq = sq // tq
    if causal:
        ends = np.minimum(N_META + CHUNK * ((np.arange(nq) * tq + tq - 1 - N_META) // CHUNK + 1), n_keys)
    else:
        ends = np.full((nq,), n_keys)
    nch = jnp.asarray(-(-ends // tk), I32)
    tri = jnp.asarray(np.tril(np.ones((sub, sub), np.float32), -1), BF16)
    qrow = lambda n: pl.BlockSpec((None, tq, n), lambda b, i, s: (b, i, 0))
    krow = lambda n: pl.BlockSpec((None, sk, n), lambda b, i, s: (b, 0, 0), pipeline_mode=pl.Buffered(1))
    kern = functools.partial(_attn_kernel, tq=tq, tk=tk, topk=topk, causal=causal, n_keys=n_keys)
    return pl.pallas_call(
        kern,
        grid_spec=pltpu.PrefetchScalarGridSpec(
            num_scalar_prefetch=1,
            grid=(bsz, nq),
            in_specs=[qrow(qis.shape[2]), qrow(sm.shape[2]), krow(kid.shape[2]), qrow(q.shape[2]),
                      krow(kb.shape[2]),
                      pl.BlockSpec((vt.shape[0], sk), lambda b, i, s: (0, b), pipeline_mode=pl.Buffered(1)),
                      pl.BlockSpec((sub, sub), lambda b, i, s: (0, 0))],
            out_specs=qrow(q.shape[2]),
            scratch_shapes=[pltpu.VMEM((sk, tq), F32),
                            pltpu.VMEM((N_KV_HEADS, HEAD_DIM + BF16_SUBLANES, KV_REP * tq), F32),
                            pltpu.VMEM((2, N_KV_HEADS, tk, KV_REP * tq), F32)]),
        out_shape=jax.ShapeDtypeStruct(q.shape, BF16),
        compiler_params=pltpu.CompilerParams(dimension_semantics=("arbitrary", "arbitrary"),
                                             vmem_limit_bytes=VMEM_LIMIT_BYTES),
        name="attn",
    )(nch, qis, sm, kid, q, kb, vt, tri)


def _mix_kernel(tb_ref, o_ref, z_ref, halo_ref, sga_ref, sgb_ref, x_ref, wpool_ref, ps_ref, wao_ref, wout_ref,
                x1_ref, zc_scr, *, tm):
    hs = halo_ref.shape[0]
    zc_scr[0:hs, :] = halo_ref[...]
    zc_scr[hs:hs + tm, :] = z_ref[...]
    avail = tb_ref[pl.program_id(0)] + lax.broadcasted_iota(I32, (tm, 1), 0) + 1
    gw = z_ref.shape[1] // len(POOL_WINDOWS)
    ys = []
    for g, win in enumerate(POOL_WINDOWS):
        cols = slice(g * gw, (g + 1) * gw)
        zt = zc_scr[hs:hs + tm, cols]
        s = zt
        for j in range(1, win):
            s = s + zc_scr[hs - j:hs - j + tm, cols]
        cnt = jnp.minimum(avail, win).astype(F32)
        dlt = s / cnt - zt
        ys.append(_dot(dlt.astype(BF16), wpool_ref[g]))
    a = jnp.concatenate(ys, axis=1) * ps_ref[...]
    b = _dot(o_ref[...], wao_ref[...])
    mrg = sga_ref[...] * a + sgb_ref[...] * b
    x1_ref[...] = x_ref[...] + _dot(mrg.astype(BF16), wout_ref[...])


def _mix(tbase, o, z, halo, sga, sgb, x, w, *, tm):
    rows, d = x.shape
    nt = rows // tm
    hs = halo.shape[1]
    row = lambda n: pl.BlockSpec((tm, n), lambda i, s: (i, 0))
    const = lambda a: pl.BlockSpec(a.shape, lambda i, s: (0,) * a.ndim)
    return pl.pallas_call(
        functools.partial(_mix_kernel, tm=tm),
        grid_spec=pltpu.PrefetchScalarGridSpec(
            num_scalar_prefetch=1,
            grid=(nt,),
            in_specs=[row(d), row(d), pl.BlockSpec((None, hs, d), lambda i, s: (i, 0, 0)), row(d), row(d), row(d),
                      const(w["wpool"]), const(w["pscale"]), const(w["wao"]), const(w["wout"])],
            out_specs=row(d),
            scratch_shapes=[pltpu.VMEM((hs + tm, d), F32)]),
        out_shape=jax.ShapeDtypeStruct((rows, d), F32),
        compiler_params=pltpu.CompilerParams(dimension_semantics=("arbitrary",), vmem_limit_bytes=VMEM_LIMIT_BYTES),
        name="mix",
    )(tbase, o, z, halo, sga, sgb, x, w["wpool"], w["pscale"], w["wao"], w["wout"])


ROUTE_GROUP_LANE0 = N_EXPERTS


def _moe_kernel(x_ref, nxt_ref, gf_ref, gl_ref, wrh_ref, wrl_ref, br_ref, wg_ref, wu_ref, wd_ref, y_ref,
                x_scr, h_scr, route_scr, acc_scr, *, ec, lead):
    j = pl.program_id(1)
    tm = x_ref.shape[0]

    @pl.when(j == 0)
    def _():
        x_scr[...] = jnp.concatenate([x_ref[lead:, :], nxt_ref[...]], axis=0) if lead else x_ref[...]
        hf = _rms(x_scr[...], gf_ref[...])
        h_hi = hf.astype(BF16)
        h_lo = (hf - h_hi.astype(F32)).astype(BF16)
        h_scr[...] = h_hi
        logits = _dot(h_hi, wrh_ref[...]) + _dot(h_lo, wrh_ref[...]) + _dot(h_hi, wrl_ref[...]) + br_ref[...]
        lane = lax.broadcasted_iota(I32, (tm, LANES), 1)
        lanef = lane.astype(F32)
        ninf = -jnp.inf
        isg = (lane >= ROUTE_GROUP_LANE0) & (lane < ROUTE_GROUP_LANE0 + N_EXPERT_GROUPS)
        gl = jnp.where(isg, logits, ninf)
        gmax = jnp.max(gl, axis=1, keepdims=True)
        gsel = jnp.min(jnp.where(gl == gmax, lanef, 1e9), axis=1, keepdims=True)
        gprob = 1.0 / jnp.sum(jnp.exp(gl - gmax), axis=1, keepdims=True)
        e0 = (gsel - ROUTE_GROUP_LANE0) * EXPERTS_PER_GROUP
        el = jnp.where((lanef >= e0) & (lanef < e0 + EXPERTS_PER_GROUP), logits, ninf)
        v1 = jnp.max(el, axis=1, keepdims=True)
        i1 = jnp.min(jnp.where(el == v1, lanef, 1e9), axis=1, keepdims=True)
        el2 = jnp.where(lanef == i1, ninf, el)
        v2 = jnp.max(el2, axis=1, keepdims=True)
        i2 = jnp.min(jnp.where(el2 == v2, lanef, 1e9), axis=1, keepdims=True)
        e = jnp.exp(v2 - v1)
        w1 = (1.0 / (1.0 + e)) * gprob
        w2 = (e / (1.0 + e)) * gprob
        for n, val in enumerate((i1, i2, w1, w2)):
            route_scr[n] = jnp.broadcast_to(val, (tm, LANES))
        acc_scr[...] = jnp.zeros(acc_scr.shape, F32)

    h = h_scr[...]
    i1, i2, w1, w2 = route_scr[0], route_scr[1], route_scr[2], route_scr[3]
    gate = _dot(h, wg_ref[...])
    up = _dot(h, wu_ref[...])
    act = gate * jax.nn.sigmoid(gate) * up
    parts = []
    for q in range(ec):
        ef = jnp.full((tm, LANES), j * ec + q, I32).astype(F32)
        comb = jnp.where(i1 == ef, w1, 0.0) + jnp.where(i2 == ef, w2, 0.0)
        parts.append(act[:, q * D_EXPERT:(q + 1) * D_EXPERT] * comb)
    acc_scr[...] += _dot(jnp.concatenate(parts, axis=1).astype(BF16), wd_ref[...])

    @pl.when(j == pl.num_programs(1) - 1)
    def _():
        y_ref[...] = _rms(x_scr[...] + acc_scr[...], gl_ref[...])


def _moe(x, w, *, tm, n_out, lead, ec=EXPERTS_PER_GROUP):
    bsz, s, d = x.shape
    assert n_out % tm == 0 and n_out + lead <= s
    nt = n_out // tm
    ne = N_EXPERTS // ec
    cw = ec * D_EXPERT
    lb = max(lead, SUBLANES)
    const = lambda a: pl.BlockSpec(a.shape, lambda i, j: (0,) * a.ndim)
    tile = pl.BlockSpec((None, tm, d), lambda i, j: (i // nt, i % nt, 0))
    after = pl.BlockSpec((None, lb, d), lambda i, j: (i // nt, jnp.minimum((i % nt + 1) * (tm // lb), s // lb - 1), 0))
    return pl.pallas_call(
        functools.partial(_moe_kernel, ec=ec, lead=lead),
        grid=(bsz * nt, ne),
        in_specs=[tile, after, const(w["gffn"]), const(w["gfinal"]),
                  const(w["wr_hi"]), const(w["wr_lo"]), const(w["br"]),
                  pl.BlockSpec((d, cw), lambda i, j: (0, j)), pl.BlockSpec((d, cw), lambda i, j: (0, j)),
                  pl.BlockSpec((cw, d), lambda i, j: (j, 0))],
        out_specs=tile,
        out_shape=jax.ShapeDtypeStruct((bsz, n_out, d), F32),
        scratch_shapes=[pltpu.VMEM((tm, d), F32), pltpu.VMEM((tm, d), BF16), pltpu.VMEM((4, tm, LANES), F32),
                        pltpu.VMEM((tm, d), F32)],
        compiler_params=pltpu.CompilerParams(dimension_semantics=("arbitrary", "arbitrary"),
                                             vmem_limit_bytes=VMEM_LIMIT_BYTES),
        name="moe",
    )(x, x, w["gffn"], w["gfinal"], w["wr_hi"], w["wr_lo"], w["br"], w["wg"], w["wu"], w["wd"])


def _rope_table(pos):
    posf = pos.astype(F32)[:, None]

    def cs(half):
        inv = jnp.exp(-math.log(ROPE_THETA) * jnp.arange(half, dtype=F32) * (1.0 / half))
        ang = posf * inv[None, :]
        return jnp.cos(ang), jnp.sin(ang)

    cq, sq = cs(QK_HALF)
    ci, si = cs(IX_HALF)
    used = 2 * QK_HALF + 2 * IX_HALF
    return jnp.concatenate([cq, sq, ci, si, jnp.zeros((pos.shape[0], LANES - used), F32)], axis=1)


def _prep_weights(l, norm_mix_g, norm_ffn_g, norm_final_g, w_in, w_pool, pool_scale, w_attn_out, w_out,
                  w_router_group, b_router_group, w_router_expert, b_router_expert,
                  w_expert_gate, w_expert_up, w_expert_down):
    d = w_in.shape[1]
    pw = w_pool.shape[1] * w_pool.shape[2]
    qw = N_HEADS * HEAD_DIM
    kvw = N_KV_HEADS * HEAD_DIM
    widths = (pw, qw, kvw, kvw, N_IDX_HEADS * IDX_DIM, IDX_DIM, N_IDX_HEADS, d, d)
    offs = np.concatenate([[0], np.cumsum(widths)])
    wi = w_in[l]
    piece = lambda a, b: wi[:, offs[a]:offs[b]].astype(BF16)
    wsm = jnp.concatenate([wi[:, offs[5]:offs[7]], jnp.zeros((d, LANES - IDX_DIM - N_IDX_HEADS), F32)], axis=1)
    wr = jnp.concatenate([w_router_expert[l], w_router_group[l],
                          jnp.zeros((d, LANES - N_EXPERTS - N_EXPERT_GROUPS), F32)], axis=1)
    wr_hi = wr.astype(BF16)
    br = jnp.concatenate([b_router_expert[l], b_router_group[l],
                          jnp.zeros((LANES - N_EXPERTS - N_EXPERT_GROUPS,), F32)])[None, :]
    flat_e = lambda a: a.transpose(1, 0, 2).reshape(d, N_EXPERTS * D_EXPERT).astype(BF16)
    return dict(
        gmix=norm_mix_g[l][None, :], gffn=norm_ffn_g[l][None, :], gfinal=norm_final_g[None, :],
        wz=piece(0, 1), wq=piece(1, 2), wkv=piece(2, 4), wqi=piece(4, 5), wsm=wsm.astype(BF16), wgab=piece(7, 9),
        wpool=w_pool[l].astype(BF16), pscale=pool_scale[l][None, :],
        wao=w_attn_out[l].astype(BF16), wout=w_out[l].astype(BF16),
        wr_hi=wr_hi, wr_lo=(wr - wr_hi.astype(F32)).astype(BF16), br=br,
        wg=flat_e(w_expert_gate[l]), wu=flat_e(w_expert_up[l]),
        wd=w_expert_down[l].reshape(N_EXPERTS * D_EXPERT, d).astype(BF16),
    )


def _dup_lanes(ki):
    return jnp.concatenate([ki, ki], axis=-1)


def kernel(x_prompt, x_sample, cache_k, cache_v, cache_idx_k, state_pool, meta_tokens, norm_mix_g, norm_ffn_g,
           norm_final_g, w_in, w_pool, pool_scale, w_attn_out, w_out, w_router_group, b_router_group,
           w_router_expert, b_router_expert, w_expert_gate, w_expert_up, w_expert_down):
    bsz, seq, d = x_prompt.shape
    dbsz, dec, _ = x_sample.shape
    assert w_in.shape[0] == 1, "single-layer step"
    l = 0
    past = cache_k.shape[2] - N_META
    n = N_META + seq
    topk_p = min(IDX_TOPK, seq // 4)
    topk_s = min(IDX_TOPK, (past + dec) // 4)
    kvw = N_KV_HEADS * HEAD_DIM

    tk = ATTN_TK
    tm = tk
    assert seq % tm == 0 and N_META <= tm
    sp = seq + tm
    nt = sp // tm
    tm_moe = _pick_tile(seq, 1100)
    w = _prep_weights(l, norm_mix_g, norm_ffn_g, norm_final_g, w_in, w_pool, pool_scale, w_attn_out, w_out,
                      w_router_group, b_router_group, w_router_expert, b_router_expert,
                      w_expert_gate, w_expert_up, w_expert_down)

    tab_p = _rope_table(jnp.arange(sp))
    xp, z, q, k_c, v_c, kb, vt, qis, sm, kid, sga, sgb = _inproj(
        x_prompt, meta_tokens.astype(F32), w["gmix"], tab_p, w, tm=tm, n_out=n, lead=N_META)
    b3 = lambda a: a.reshape(bsz, sp, a.shape[-1])
    o = _attention(b3(qis), b3(sm), b3(kid), b3(q), b3(kb), vt, tq=ATTN_TQ_PROMPT, topk=topk_p, causal=True, n_keys=n)
    hs = POOL_STATE + 1
    z4 = z.reshape(bsz, nt, tm, d)
    halo = jnp.concatenate([jnp.zeros((bsz, 1, hs, d), F32), z4[:, :-1, tm - hs:, :]], axis=1).reshape(bsz * nt, hs, d)
    tbase = jnp.asarray(np.tile(np.arange(nt) * tm, bsz), I32)
    x1 = _mix(tbase, o.reshape(bsz * sp, -1), z, halo, sga, sgb, xp, w, tm=tm)
    y_prompt = _moe(b3(x1), w, tm=tm_moe, n_out=seq, lead=N_META)
    k_prompt = k_c[None]
    v_prompt = v_c[None]
    idx_k_prompt = b3(sm)[:, :n, :IDX_DIM][None]
    pool_prompt = b3(z)[:, n - POOL_STATE:n][None]

    rows_s = dbsz * dec
    tab_s = jnp.tile(_rope_table(N_META + past + jnp.arange(dec)), (dbsz, 1))
    xs = x_sample.reshape(rows_s, d)
    _, zs, qs, ks_c, vs_c, kbs, _, qiss, sms, kids, sgas, sgbs = _inproj(
        xs[None], jnp.zeros((SUBLANES, d), F32), w["gmix"], tab_s, w, tm=rows_s, n_out=rows_s, lead=0)
    vs = vs_c.reshape(rows_s, kvw)
    nk = N_META + past + dec
    sk = _round_up(nk, tk)
    sq_s = _round_up(dec, ATTN_TQ)
    s3 = lambda a: a.reshape(dbsz, dec, a.shape[-1])
    padr = lambda a, rows: jnp.concatenate([a, jnp.zeros((dbsz, rows - a.shape[1], a.shape[-1]), a.dtype)], axis=1)
    k_all = padr(jnp.concatenate([cache_k[l].reshape(dbsz, N_META + past, kvw).astype(BF16), s3(kbs)], axis=1), sk)
    v_all = padr(jnp.concatenate([cache_v[l].reshape(dbsz, N_META + past, kvw).astype(BF16),
                                  s3(vs).astype(BF16)], axis=1), sk)
    vt_all = v_all.transpose(2, 0, 1).reshape(kvw, dbsz * sk)
    ki_all = padr(jnp.concatenate([_dup_lanes(cache_idx_k[l]).astype(BF16), s3(kids)], axis=1), sk)
    os_ = _attention(padr(s3(qiss), sq_s), padr(s3(sms), sq_s), ki_all, padr(s3(qs), sq_s), k_all, vt_all,
                     tq=ATTN_TQ, topk=topk_s, causal=False, n_keys=nk)[:, :dec]
    zcat = jnp.concatenate([state_pool[l].astype(F32), s3(zs)], axis=1)
    halo_s = jnp.concatenate([jnp.zeros((dbsz, 1, d), F32), zcat[:, :POOL_STATE]], axis=1)
    tbase_s = jnp.full((dbsz,), POOL_STATE, I32)
    x1s = _mix(tbase_s, os_.reshape(rows_s, -1), zs, halo_s, sgas, sgbs, xs, w, tm=dec)
    y_sample = _moe(x1s[None], w, tm=rows_s, n_out=rows_s, lead=0).reshape(dbsz, dec, d)
    k_sample = ks_c.reshape(1, dbsz, dec, N_KV_HEADS, HEAD_DIM)
    v_sample = vs_c.reshape(1, dbsz, dec, N_KV_HEADS, HEAD_DIM)
    idx_k_sample = s3(sms)[:, :, :IDX_DIM][None]
    pool_sample = zcat[:, zcat.shape[1] - POOL_STATE:][None]

    return (y_prompt, y_sample, k_prompt, v_prompt, idx_k_prompt, pool_prompt,
            k_sample, v_sample, idx_k_sample, pool_sample)
```

```python
import functools
import math

import numpy as np
import jax
import jax.numpy as jnp
from jax import lax
from jax.experimental import pallas as pl
from jax.experimental.pallas import tpu as pltpu

CHUNK = 64
N_META = 16
EPS = 1e-6
POOL_WINDOWS = (2, 4, 8, 16)
POOL_STATE = max(POOL_WINDOWS) - 1
N_HEADS = 8
N_KV_HEADS = 2
HEAD_DIM = 128
KV_REP = N_HEADS // N_KV_HEADS
N_IDX_HEADS = 4
IDX_DIM = 64
IDX_TOPK = 256
IDX_W_SCALE = (N_IDX_HEADS * IDX_DIM) ** -0.5
ATTN_SCALE = HEAD_DIM ** -0.5
Q_PRESCALE = ATTN_SCALE * math.log2(math.e)
ROPE_THETA = 500000.0
ROT_FRACTION = 4
N_EXPERT_GROUPS = 4
EXPERTS_PER_GROUP = 8
N_EXPERTS = N_EXPERT_GROUPS * EXPERTS_PER_GROUP
D_EXPERT = 128

LANES = 128
SUBLANES = 8
BF16_SUBLANES = 16
VMEM_LIMIT_BYTES = 56 * 1024 * 1024

QK_HALF = HEAD_DIM // ROT_FRACTION // 2
IX_HALF = IDX_DIM // ROT_FRACTION // 2
TAB_COS_QK = 0
TAB_SIN_QK = QK_HALF
TAB_COS_IX = 2 * QK_HALF
TAB_SIN_IX = 2 * QK_HALF + IX_HALF

ATTN_TQ = LANES
ATTN_TQ_PROMPT = 2 * LANES
ATTN_TK = 512
TIE_BLOCK = 256
F32_TINY = float(np.finfo(np.float32).tiny)
MAX_PROBES = 64
NEG_BIG = -1e30

F32 = jnp.float32
BF16 = jnp.bfloat16
I32 = jnp.int32


def _round_up(x, m):
    return -(-x // m) * m


def _pick_tile(rows, cap):
    return max(t for t in range(BF16_SUBLANES, cap + 1, BF16_SUBLANES) if rows % t == 0)


def _rms(x, g):
    return x * lax.rsqrt(jnp.mean(x * x, axis=-1, keepdims=True) + EPS) * g


def _dot(a, b):
    return jnp.dot(a, b, preferred_element_type=F32)


def _dot_nt(a, b):
    return lax.dot_general(a, b, (((1,), (1,)), ((), ())), preferred_element_type=F32)


def _rope(y, c, a, b, shift_a, shift_b):
    outs = []
    for j in range(y.shape[1] // LANES):
        ys = y[:, j * LANES:(j + 1) * LANES]
        outs.append(ys * c + pltpu.roll(ys, shift_a, 1) * a + pltpu.roll(ys, shift_b, 1) * b)
    return outs[0] if len(outs) == 1 else jnp.concatenate(outs, axis=1)


def _inproj_kernel(x_ref, prev_ref, meta_ref, g_ref, tab_ref, wz_ref, wq_ref, wkv_ref, wqi_ref, wsm_ref, wgab_ref,
                   xs_ref, z_ref, q_ref, k4_ref, v4_ref, kb_ref, vt_ref, qis_ref, sm_ref, kid_ref, sga_ref, sgb_ref,
                   *, lead, frame_tiles):
    tm, d = x_ref.shape
    if lead:
        ti = pl.program_id(0) % (frame_tiles + 1)
        head = jnp.where(ti == 0, meta_ref[...], prev_ref[...])
        body = jnp.where(ti < frame_tiles, x_ref[0:tm - lead, :], 0.0)
        x = jnp.concatenate([head, body], axis=0)
    else:
        x = x_ref[...]
    xs_ref[...] = x
    h = _rms(x, g_ref[...]).astype(BF16)
    lane = lax.broadcasted_iota(I32, (tm, LANES), 1)
    t = tab_ref[...]

    def rl(s):
        return pltpu.roll(t, s % LANES, 1) if s % LANES else t

    c_qk = jnp.where(lane < QK_HALF, rl(-TAB_COS_QK), jnp.where(lane < 2 * QK_HALF, rl(QK_HALF - TAB_COS_QK), 1.0))
    a_qk = jnp.where(lane < QK_HALF, -rl(-TAB_SIN_QK), 0.0)
    b_qk = jnp.where((lane >= QK_HALF) & (lane < 2 * QK_HALF), rl(QK_HALF - TAB_SIN_QK), 0.0)
    l64 = lane & (IDX_DIM - 1)
    hi = lane >= IDX_DIM
    cos_lo = jnp.where(hi, rl(IDX_DIM - TAB_COS_IX), rl(-TAB_COS_IX))
    cos_hi = jnp.where(hi, rl(IDX_DIM + IX_HALF - TAB_COS_IX), rl(IX_HALF - TAB_COS_IX))
    sin_lo = jnp.where(hi, rl(IDX_DIM - TAB_SIN_IX), rl(-TAB_SIN_IX))
    sin_hi = jnp.where(hi, rl(IDX_DIM + IX_HALF - TAB_SIN_IX), rl(IX_HALF - TAB_SIN_IX))
    c_ix = jnp.where(l64 < IX_HALF, cos_lo, jnp.where(l64 < 2 * IX_HALF, cos_hi, 1.0))
    a_ix = jnp.where(l64 < IX_HALF, -sin_lo, 0.0)
    b_ix = jnp.where((l64 >= IX_HALF) & (l64 < 2 * IX_HALF), sin_hi, 0.0)

    z_ref[...] = _dot(h, wz_ref[...])

    q = _rope(_dot(h, wq_ref[...]), c_qk, a_qk, b_qk, LANES - QK_HALF, QK_HALF)
    q_ref[...] = (q * Q_PRESCALE).astype(BF16)

    kv = _dot(h, wkv_ref[...])
    kvw = kv.shape[1] // 2
    k = _rope(kv[:, :kvw], c_qk, a_qk, b_qk, LANES - QK_HALF, QK_HALF)
    v = kv[:, kvw:]
    for g in range(N_KV_HEADS):
        k4_ref[:, g, :] = k[:, g * HEAD_DIM:(g + 1) * HEAD_DIM]
        v4_ref[:, g, :] = v[:, g * HEAD_DIM:(g + 1) * HEAD_DIM]
    kb_ref[...] = k.astype(BF16)
    vt_ref[...] = v.T.astype(BF16)

    qi = _rope(_dot(h, wqi_ref[...]), c_ix, a_ix, b_ix, LANES - IX_HALF, IX_HALF)
    pieces = []
    for hd in range(N_IDX_HEADS):
        col = qi[:, (hd // 2) * LANES:(hd // 2 + 1) * LANES]
        keep = hi if hd % 2 else jnp.logical_not(hi)
        pieces.append(jnp.where(keep, col, 0.0))
    qis_ref[...] = jnp.concatenate(pieces, axis=1).astype(BF16)

    c_sm = jnp.where(hi, IDX_W_SCALE, c_ix)
    a_sm = jnp.where(hi, 0.0, a_ix)
    b_sm = jnp.where(hi, 0.0, b_ix)
    sm = _rope(_dot(h, wsm_ref[...]), c_sm, a_sm, b_sm, LANES - IX_HALF, IX_HALF)
    sm_ref[...] = sm
    kid_ref[...] = jnp.where(hi, pltpu.roll(sm, IDX_DIM, 1), sm).astype(BF16)

    gab = _dot(h, wgab_ref[...])
    sga_ref[...] = jax.nn.sigmoid(gab[:, :d])
    sgb_ref[...] = jax.nn.sigmoid(gab[:, d:])


def _inproj(x, meta, g, tab, w, *, tm, n_out, lead):
    bsz, s, d = x.shape
    assert s % tm == 0
    ft = s // tm
    nt = ft + 1 if lead else ft
    rows = bsz * nt * tm
    qw = w["wq"].shape[1]
    kvw = w["wkv"].shape[1] // 2
    lb = max(lead, SUBLANES)
    row_spec = lambda n: pl.BlockSpec((tm, n), lambda i: (i, 0))
    const = lambda a: pl.BlockSpec(a.shape, lambda i: (0,) * a.ndim)
    cache_spec = pl.BlockSpec((None, tm, N_KV_HEADS, HEAD_DIM), lambda i: (i // nt, i % nt, 0, 0))
    cache_shape = jax.ShapeDtypeStruct((bsz, n_out, N_KV_HEADS, HEAD_DIM), F32)
    outs = [
        (d, F32),
        (d, F32),
        (qw, BF16),
        "cache", "cache",
        (kvw, BF16),
        "vt",
        (N_IDX_HEADS * LANES, BF16),
        (LANES, F32),
        (LANES, BF16),
        (d, F32), (d, F32),
    ]
    spec_of = lambda o: (cache_spec if o == "cache" else pl.BlockSpec((kvw, tm), lambda i: (0, i)) if o == "vt"
                         else row_spec(o[0]))
    shape_of = lambda o: (cache_shape if o == "cache" else jax.ShapeDtypeStruct((kvw, rows), BF16) if o == "vt"
                          else jax.ShapeDtypeStruct((rows, o[0]), o[1]))
    return pl.pallas_call(
        functools.partial(_inproj_kernel, lead=lead, frame_tiles=ft),
        grid=(bsz * nt,),
        in_specs=[pl.BlockSpec((None, tm, d), lambda i: (i // nt, jnp.minimum(i % nt, ft - 1), 0)),
                  pl.BlockSpec((None, lb, d), lambda i: (i // nt, jnp.maximum((i % nt) * (tm // lb) - 1, 0), 0)),
                  const(meta), const(g),
                  pl.BlockSpec((tm, LANES), lambda i: (i % nt, 0)),
                  const(w["wz"]), const(w["wq"]), const(w["wkv"]), const(w["wqi"]), const(w["wsm"]), const(w["wgab"])],
        out_specs=[spec_of(o) for o in outs],
        out_shape=[shape_of(o) for o in outs],
        compiler_params=pltpu.CompilerParams(dimension_semantics=("arbitrary",), vmem_limit_bytes=VMEM_LIMIT_BYTES),
        name="inproj",
    )(x, x, meta, g, tab, w["wz"], w["wq"], w["wkv"], w["wqi"], w["wsm"], w["wgab"])


def _row_fold(x, op, group=SUBLANES):
    ways = 4
    accs = []
    for j in range(x.shape[0] // group):
        part = x[j * group:(j + 1) * group]
        if j < ways:
            accs.append(part)
        else:
            accs[j % ways] = op(accs[j % ways], part)
    while len(accs) > 1:
        accs = [op(accs[j], accs[j + 1]) for j in range(0, len(accs) - 1, 2)] + accs[len(accs) & ~1:]
    return accs[0]


def _attn_kernel(nch_ref, qis_ref, qis_nx_ref, sm_ref, sm_nx_ref, kid_ref, q_ref, q_nx_ref, k_ref, vt_ref, tri_ref,
                 o_ref, sc_scr, acc_scr, s_scr, *, tq, tk, topk, causal, n_keys, lead):
    i = pl.program_id(1)
    nch = nch_ref[i]
    sub = tri_ref.shape[0]
    nq4 = KV_REP * tq
    rows = lambda ref, nx: jnp.concatenate([ref[lead:, :], nx[...]], axis=0) if lead else ref[...]
    qis, sm, q = rows(qis_ref, qis_nx_ref), rows(sm_ref, sm_nx_ref), rows(q_ref, q_nx_ref)
    qpos = lead + i * tq + lax.broadcasted_iota(I32, (1, tq), 1)
    if causal:
        lim = N_META + CHUNK * (jnp.right_shift(qpos - N_META, int(math.log2(CHUNK))) + 1)
        lim = jnp.minimum(lim, n_keys)
    else:
        lim = jnp.full((1, tq), n_keys, I32)

    qstack = jnp.concatenate([qis[:, hd * LANES:(hd + 1) * LANES] for hd in range(N_IDX_HEADS)], axis=0)
    smt = sm.T
    wlane = jnp.concatenate([smt[IDX_DIM + hd:IDX_DIM + hd + 1, :] for hd in range(N_IDX_HEADS)], axis=1)

    key_row = lax.broadcasted_iota(I32, (tk, tq), 0)

    def score_body(c, st):
        smax, smin = st
        off = pl.multiple_of(c * tk, tk)
        sc = _dot_nt(kid_ref[pl.ds(off, tk), :], qstack)
        sc = jnp.maximum(sc, 0.0) * wlane
        s = sc[:, 0:tq]
        for hd in range(1, N_IDX_HEADS):
            s = s + sc[:, hd * tq:(hd + 1) * tq]
        ok = key_row < lim - off
        masked = jnp.where(ok, s, -jnp.inf)
        sc_scr[pl.ds(off, tk), :] = masked
        return (jnp.maximum(smax, _row_fold(masked, jnp.maximum)),
                jnp.minimum(smin, _row_fold(jnp.where(ok, s, jnp.inf), jnp.minimum)))

    smax, smin = lax.fori_loop(0, nch, score_body, (jnp.full((SUBLANES, tq), -jnp.inf, F32),
                                                    jnp.full((SUBLANES, tq), jnp.inf, F32)))
    smax = jnp.max(smax, axis=0, keepdims=True)
    smin = jnp.min(smin, axis=0, keepdims=True)

    def count_ge(cand):
        def hits(c, cnd):
            kb = sc_scr[pl.ds(pl.multiple_of(c * tk, tk), tk), :]
            return _row_fold(jnp.where(kb >= cnd, 1.0, 0.0), jnp.add)

        def body(c2, acc):
            second = 2 * c2 + 1
            cnd2 = jnp.where(second < nch, cand, jnp.inf)
            return acc + (hits(2 * c2, cand) + hits(jnp.minimum(second, nch - 1), cnd2))
        acc = lax.fori_loop(0, (nch + 1) // 2, body, jnp.zeros((SUBLANES, tq), F32))
        return jnp.sum(acc, axis=0, keepdims=True)

    take_all = lim <= topk
    kf = float(topk)
    lo0 = jnp.where(take_all, NEG_BIG, smin)
    hi0 = jnp.where(take_all, NEG_BIG, smax + jnp.abs(smax) * 2.0 ** -10 + F32_TINY * 2.0 ** 24)
    c_lo0 = lim.astype(F32)
    c_hi0 = jnp.where(take_all, kf, 0.0)
    closed0 = jnp.where(take_all, 1.0, 0.0)

    def search_cond(st):
        return (st[0] < MAX_PROBES) & (jnp.max(1.0 - st[5]) > 0.5)

    def search_body(st):
        return search_step(search_step(st))

    def search_step(st):
        it, lo, hi, c_lo, c_hi, closed = st
        mid = lo + 0.5 * (hi - lo)
        interp = hi - (hi - lo) * ((kf - c_hi) / jnp.maximum(c_lo - c_hi, 1.0))
        inside = lambda v: (v > lo) & (v < hi)
        cand = jnp.where((it & 1) == 0, interp, mid)
        cand = jnp.where((it == 0) & inside(0.0), 0.0, cand)
        cand = jnp.where((it == 1) & inside(F32_TINY), F32_TINY, cand)
        cand = jnp.where(inside(cand), cand, mid)
        cnt = count_ge(cand)
        is_open = closed < 0.5
        up = is_open & (cnt > kf)
        down = is_open & (cnt <= kf)
        lo = jnp.where(up, cand, lo)
        c_lo = jnp.where(up, cnt, c_lo)
        hi = jnp.where(down, cand, hi)
        c_hi = jnp.where(down, cnt, c_hi)
        done = (c_hi == kf) | jnp.logical_not(inside(lo + 0.5 * (hi - lo))) | ((lo == 0.0) & (hi == F32_TINY))
        return it + 1, lo, hi, c_lo, c_hi, jnp.where(done, 1.0, closed)

    _, thr_lo, thr_hi, _, c_hi, _ = lax.while_loop(search_cond, search_body,
                                                   (jnp.int32(0), lo0, hi0, c_lo0, c_hi0, closed0))
    n_tie = kf - c_hi

    acc_scr[...] = jnp.zeros(acc_scr.shape, F32)
    qgs = [jnp.concatenate([q[:, (g * KV_REP + r) * HEAD_DIM:(g * KV_REP + r + 1) * HEAD_DIM]
                            for r in range(KV_REP)], axis=0) for g in range(N_KV_HEADS)]

    ones_rows = jnp.ones((BF16_SUBLANES, tk), BF16)

    def logits(c, slot, seen):
        off = pl.multiple_of(c * tk, tk)
        kb = sc_scr[pl.ds(off, tk), :]
        above = kb >= thr_hi
        at_least = kb >= thr_lo
        eqf = jnp.where(at_least, 1.0, 0.0) - jnp.where(above, 1.0, 0.0)
        ranks = []
        for hf in range(tk // sub):
            e = eqf[hf * sub:(hf + 1) * sub]
            ranks.append(seen + _dot(tri_ref[...], e.astype(BF16)))
            seen = seen + jnp.sum(_row_fold(e, jnp.add), axis=0, keepdims=True)
        rank = jnp.concatenate(ranks, axis=0)
        maskf = jnp.where(above, 0.0, jnp.where(at_least, jnp.where(rank < n_tie, 0.0, NEG_BIG), NEG_BIG))
        mask4 = jnp.concatenate([maskf] * KV_REP, axis=1)
        mcs = []
        for g in range(N_KV_HEADS):
            kc = k_ref[pl.ds(off, tk), g * HEAD_DIM:(g + 1) * HEAD_DIM]
            s = _dot_nt(kc, qgs[g]) + mask4
            s_scr[slot, g] = s
            mcs.append(jnp.max(_row_fold(s, jnp.maximum), axis=0, keepdims=True))
        return seen, tuple(mcs)

    def update(c, slot, ms, mcs):
        off = pl.multiple_of(c * tk, tk)
        ms_n = []
        for g in range(N_KV_HEADS):
            vtc = vt_ref[g * HEAD_DIM:(g + 1) * HEAD_DIM, pl.ds(off, tk)]
            m_new = jnp.maximum(ms[g], mcs[g])
            alpha = jnp.exp2(ms[g] - m_new)
            p = jnp.exp2(s_scr[slot, g] - m_new).astype(BF16)
            acc_scr[g] = alpha * acc_scr[g] + _dot(jnp.concatenate([vtc, ones_rows], axis=0), p)
            ms_n.append(m_new)
        return tuple(ms_n)

    def flash_body(c, st):
        seen, ms, mcs = st
        slot = c & 1
        ms = update(c, slot, ms, mcs)
        seen, mcs = logits(c + 1, 1 - slot, seen)
        return seen, ms, mcs

    m0 = tuple(jnp.full((1, nq4), NEG_BIG, F32) for _ in range(N_KV_HEADS))
    seen0, mc0 = logits(0, 0, jnp.zeros((1, tq), F32))
    _, ms, mcs = lax.fori_loop(0, nch - 1, flash_body, (seen0, m0, mc0))
    update(nch - 1, (nch - 1) & 1, ms, mcs)

    for g in range(N_KV_HEADS):
        og = acc_scr[g, 0:HEAD_DIM] / acc_scr[g, HEAD_DIM:HEAD_DIM + 1]
        for r in range(KV_REP):
            hd = g * KV_REP + r
            o_ref[:, hd * HEAD_DIM:(hd + 1) * HEAD_DIM] = og[:, r * tq:(r + 1) * tq].T.astype(o_ref.dtype)


def _attention(qis, sm, kid, q, kb, vt, *, tq, n_q, lead, topk, causal, n_keys):
    tk, sub = ATTN_TK, TIE_BLOCK
    bsz, sq, _ = q.shape
    sk = kb.shape[1]
    assert n_q % tq == 0 and sk % tk == 0 and n_q + lead <= sq
    nq = n_q // tq
    if causal:
        ends = np.minimum(N_META + CHUNK * ((lead + np.arange(nq) * tq + tq - 1 - N_META) // CHUNK + 1), n_keys)
    else:
        ends = np.full((nq,), n_keys)
    nch = jnp.asarray(-(-ends // tk), I32)
    tri = jnp.asarray(np.tril(np.ones((sub, sub), np.float32), -1), BF16)
    lb = max(lead, BF16_SUBLANES)
    qrow = lambda n: pl.BlockSpec((None, tq, n), lambda b, i, s: (b, i, 0))
    qnext = lambda n: pl.BlockSpec((None, lb, n), lambda b, i, s: (b, jnp.minimum((i + 1) * (tq // lb), sq // lb - 1), 0))
    krow = lambda n: pl.BlockSpec((None, sk, n), lambda b, i, s: (b, 0, 0), pipeline_mode=pl.Buffered(1))
    kern = functools.partial(_attn_kernel, tq=tq, tk=tk, topk=topk, causal=causal, n_keys=n_keys, lead=lead)
    return pl.pallas_call(
        kern,
        grid_spec=pltpu.PrefetchScalarGridSpec(
            num_scalar_prefetch=1,
            grid=(bsz, nq),
            in_specs=[qrow(qis.shape[2]), qnext(qis.shape[2]), qrow(sm.shape[2]), qnext(sm.shape[2]),
                      krow(kid.shape[2]), qrow(q.shape[2]), qnext(q.shape[2]),
                      krow(kb.shape[2]),
                      pl.BlockSpec((vt.shape[0], sk), lambda b, i, s: (0, b), pipeline_mode=pl.Buffered(1)),
                      pl.BlockSpec((sub, sub), lambda b, i, s: (0, 0))],
            out_specs=qrow(q.shape[2]),
            scratch_shapes=[pltpu.VMEM((sk, tq), F32),
                            pltpu.VMEM((N_KV_HEADS, HEAD_DIM + BF16_SUBLANES, KV_REP * tq), F32),
                            pltpu.VMEM((2, N_KV_HEADS, tk, KV_REP * tq), F32)]),
        out_shape=jax.ShapeDtypeStruct((bsz, n_q, q.shape[2]), BF16),
        compiler_params=pltpu.CompilerParams(dimension_semantics=("arbitrary", "arbitrary"),
                                             vmem_limit_bytes=VMEM_LIMIT_BYTES),
        name="attn",
    )(nch, qis, qis, sm, sm, kid, q, q, kb, vt, tri)


def _mix_kernel(tb_ref, o_ref, oprev_ref, z_ref, halo_ref, sga_ref, sgb_ref, x_ref, wpool_ref, ps_ref, wao_ref,
                wout_ref, x1_ref, zc_scr, *, tm, lead, tiles):
    if lead:
        head = jnp.where(pl.program_id(0) % tiles == 0, jnp.zeros_like(oprev_ref[...]), oprev_ref[...])
        o = jnp.concatenate([head, o_ref[0:tm - lead, :]], axis=0)
    else:
        o = o_ref[...]
    hs = halo_ref.shape[0]
    zc_scr[0:hs, :] = halo_ref[...]
    zc_scr[hs:hs + tm, :] = z_ref[...]
    avail = tb_ref[pl.program_id(0)] + lax.broadcasted_iota(I32, (tm, 1), 0) + 1
    gw = z_ref.shape[1] // len(POOL_WINDOWS)
    ys = []
    for g, win in enumerate(POOL_WINDOWS):
        cols = slice(g * gw, (g + 1) * gw)
        zt = zc_scr[hs:hs + tm, cols]
        s = zt
        for j in range(1, win):
            s = s + zc_scr[hs - j:hs - j + tm, cols]
        cnt = jnp.minimum(avail, win).astype(F32)
        dlt = s / cnt - zt
        ys.append(_dot(dlt.astype(BF16), wpool_ref[g]))
    a = jnp.concatenate(ys, axis=1) * ps_ref[...]
    b = _dot(o, wao_ref[...])
    mrg = sga_ref[...] * a + sgb_ref[...] * b
    x1_ref[...] = x_ref[...] + _dot(mrg.astype(BF16), wout_ref[...])


def _mix(tbase, o, z, halo, sga, sgb, x, w, *, tm, lead):
    rows, d = x.shape
    bsz, so, _ = o.shape
    nt = rows // tm
    tiles = nt // bsz
    ot = so // tm
    hs = halo.shape[1]
    lb = max(lead, BF16_SUBLANES)
    row = lambda n: pl.BlockSpec((tm, n), lambda i, s: (i, 0))
    const = lambda a: pl.BlockSpec(a.shape, lambda i, s: (0,) * a.ndim)
    o_cur = pl.BlockSpec((None, tm, d), lambda i, s: (i // tiles, jnp.minimum(i % tiles, ot - 1), 0))
    o_prev = pl.BlockSpec((None, lb, d), lambda i, s: (i // tiles, jnp.maximum((i % tiles) * (tm // lb) - 1, 0), 0))
    return pl.pallas_call(
        functools.partial(_mix_kernel, tm=tm, lead=lead, tiles=tiles),
        grid_spec=pltpu.PrefetchScalarGridSpec(
            num_scalar_prefetch=1,
            grid=(nt,),
            in_specs=[o_cur, o_prev, row(d), pl.BlockSpec((None, hs, d), lambda i, s: (i, 0, 0)), row(d), row(d), row(d),
                      const(w["wpool"]), const(w["pscale"]), const(w["wao"]), const(w["wout"])],
            out_specs=row(d),
            scratch_shapes=[pltpu.VMEM((hs + tm, d), F32)]),
        out_shape=jax.ShapeDtypeStruct((rows, d), F32),
        compiler_params=pltpu.CompilerParams(dimension_semantics=("arbitrary",), vmem_limit_bytes=VMEM_LIMIT_BYTES),
        name="mix",
    )(tbase, o, o, z, halo, sga, sgb, x, w["wpool"], w["pscale"], w["wao"], w["wout"])


ROUTE_GROUP_LANE0 = N_EXPERTS


def _moe_kernel(x_ref, nxt_ref, gf_ref, gl_ref, wrh_ref, wrl_ref, br_ref, wg_ref, wu_ref, wd_ref, y_ref,
                x_scr, h_scr, route_scr, acc_scr, *, ec, lead):
    j = pl.program_id(1)
    tm = x_ref.shape[0]

    @pl.when(j == 0)
    def _():
        x_scr[...] = jnp.concatenate([x_ref[lead:, :], nxt_ref[...]], axis=0) if lead else x_ref[...]
        hf = _rms(x_scr[...], gf_ref[...])
        h_hi = hf.astype(BF16)
        h_lo = (hf - h_hi.astype(F32)).astype(BF16)
        h_scr[...] = h_hi
        logits = _dot(h_hi, wrh_ref[...]) + _dot(h_lo, wrh_ref[...]) + _dot(h_hi, wrl_ref[...]) + br_ref[...]
        lane = lax.broadcasted_iota(I32, (tm, LANES), 1)
        lanef = lane.astype(F32)
        ninf = -jnp.inf
        isg = (lane >= ROUTE_GROUP_LANE0) & (lane < ROUTE_GROUP_LANE0 + N_EXPERT_GROUPS)
        gl = jnp.where(isg, logits, ninf)
        gmax = jnp.max(gl, axis=1, keepdims=True)
        gsel = jnp.min(jnp.where(gl == gmax, lanef, 1e9), axis=1, keepdims=True)
        gprob = 1.0 / jnp.sum(jnp.exp(gl - gmax), axis=1, keepdims=True)
        e0 = (gsel - ROUTE_GROUP_LANE0) * EXPERTS_PER_GROUP
        el = jnp.where((lanef >= e0) & (lanef < e0 + EXPERTS_PER_GROUP), logits, ninf)
        v1 = jnp.max(el, axis=1, keepdims=True)
        i1 = jnp.min(jnp.where(el == v1, lanef, 1e9), axis=1, keepdims=True)
        el2 = jnp.where(lanef == i1, ninf, el)
        v2 = jnp.max(el2, axis=1, keepdims=True)
        i2 = jnp.min(jnp.where(el2 == v2, lanef, 1e9), axis=1, keepdims=True)
        e = jnp.exp(v2 - v1)
        w1 = (1.0 / (1.0 + e)) * gprob
        w2 = (e / (1.0 + e)) * gprob
        for n, val in enumerate((i1, i2, w1, w2)):
            route_scr[n] = jnp.broadcast_to(val, (tm, LANES))
        acc_scr[...] = jnp.zeros(acc_scr.shape, F32)

    h = h_scr[...]
    i1, i2, w1, w2 = route_scr[0], route_scr[1], route_scr[2], route_scr[3]
    gate = _dot(h, wg_ref[...])
    up = _dot(h, wu_ref[...])
    act = gate * jax.nn.sigmoid(gate) * up
    parts = []
    for q in range(ec):
        ef = jnp.full((tm, LANES), j * ec + q, I32).astype(F32)
        comb = jnp.where(i1 == ef, w1, 0.0) + jnp.where(i2 == ef, w2, 0.0)
        parts.append(act[:, q * D_EXPERT:(q + 1) * D_EXPERT] * comb)
    acc_scr[...] += _dot(jnp.concatenate(parts, axis=1).astype(BF16), wd_ref[...])

    @pl.when(j == pl.num_programs(1) - 1)
    def _():
        y_ref[...] = _rms(x_scr[...] + acc_scr[...], gl_ref[...])


def _moe(x, w, *, tm, n_out, lead, ec=EXPERTS_PER_GROUP):
    bsz, s, d = x.shape
    assert n_out % tm == 0 and n_out + lead <= s
    nt = n_out // tm
    ne = N_EXPERTS // ec
    cw = ec * D_EXPERT
    lb = max(lead, SUBLANES)
    const = lambda a: pl.BlockSpec(a.shape, lambda i, j: (0,) * a.ndim)
    tile = pl.BlockSpec((None, tm, d), lambda i, j: (i // nt, i % nt, 0))
    after = pl.BlockSpec((None, lb, d), lambda i, j: (i // nt, jnp.minimum((i % nt + 1) * (tm // lb), s // lb - 1), 0))
    return pl.pallas_call(
        functools.partial(_moe_kernel, ec=ec, lead=lead),
        grid=(bsz * nt, ne),
        in_specs=[tile, after, const(w["gffn"]), const(w["gfinal"]),
                  const(w["wr_hi"]), const(w["wr_lo"]), const(w["br"]),
                  pl.BlockSpec((d, cw), lambda i, j: (0, j)), pl.BlockSpec((d, cw), lambda i, j: (0, j)),
                  pl.BlockSpec((cw, d), lambda i, j: (j, 0))],
        out_specs=tile,
        out_shape=jax.ShapeDtypeStruct((bsz, n_out, d), F32),
        scratch_shapes=[pltpu.VMEM((tm, d), F32), pltpu.VMEM((tm, d), BF16), pltpu.VMEM((4, tm, LANES), F32),
                        pltpu.VMEM((tm, d), F32)],
        compiler_params=pltpu.CompilerParams(dimension_semantics=("arbitrary", "arbitrary"),
                                             vmem_limit_bytes=VMEM_LIMIT_BYTES),
        name="moe",
    )(x, x, w["gffn"], w["gfinal"], w["wr_hi"], w["wr_lo"], w["br"], w["wg"], w["wu"], w["wd"])


def _rope_table(pos):
    posf = pos.astype(F32)[:, None]

    def cs(half):
        inv = jnp.exp(-math.log(ROPE_THETA) * jnp.arange(half, dtype=F32) * (1.0 / half))
        ang = posf * inv[None, :]
        return jnp.cos(ang), jnp.sin(ang)

    cq, sq = cs(QK_HALF)
    ci, si = cs(IX_HALF)
    used = 2 * QK_HALF + 2 * IX_HALF
    return jnp.concatenate([cq, sq, ci, si, jnp.zeros((pos.shape[0], LANES - used), F32)], axis=1)


def _prep_weights(l, norm_mix_g, norm_ffn_g, norm_final_g, w_in, w_pool, pool_scale, w_attn_out, w_out,
                  w_router_group, b_router_group, w_router_expert, b_router_expert,
                  w_expert_gate, w_expert_up, w_expert_down):
    d = w_in.shape[1]
    pw = w_pool.shape[1] * w_pool.shape[2]
    qw = N_HEADS * HEAD_DIM
    kvw = N_KV_HEADS * HEAD_DIM
    widths = (pw, qw, kvw, kvw, N_IDX_HEADS * IDX_DIM, IDX_DIM, N_IDX_HEADS, d, d)
    offs = np.concatenate([[0], np.cumsum(widths)])
    wi = w_in[l]
    piece = lambda a, b: wi[:, offs[a]:offs[b]].astype(BF16)
    wsm = jnp.concatenate([wi[:, offs[5]:offs[7]], jnp.zeros((d, LANES - IDX_DIM - N_IDX_HEADS), F32)], axis=1)
    wr = jnp.concatenate([w_router_expert[l], w_router_group[l],
                          jnp.zeros((d, LANES - N_EXPERTS - N_EXPERT_GROUPS), F32)], axis=1)
    wr_hi = wr.astype(BF16)
    br = jnp.concatenate([b_router_expert[l], b_router_group[l],
                          jnp.zeros((LANES - N_EXPERTS - N_EXPERT_GROUPS,), F32)])[None, :]
    flat_e = lambda a: a.transpose(1, 0, 2).reshape(d, N_EXPERTS * D_EXPERT).astype(BF16)
    return dict(
        gmix=norm_mix_g[l][None, :], gffn=norm_ffn_g[l][None, :], gfinal=norm_final_g[None, :],
        wz=piece(0, 1), wq=piece(1, 2), wkv=piece(2, 4), wqi=piece(4, 5), wsm=wsm.astype(BF16), wgab=piece(7, 9),
        wpool=w_pool[l].astype(BF16), pscale=pool_scale[l][None, :],
        wao=w_attn_out[l].astype(BF16), wout=w_out[l].astype(BF16),
        wr_hi=wr_hi, wr_lo=(wr - wr_hi.astype(F32)).astype(BF16), br=br,
        wg=flat_e(w_expert_gate[l]), wu=flat_e(w_expert_up[l]),
        wd=w_expert_down[l].reshape(N_EXPERTS * D_EXPERT, d).astype(BF16),
    )


def _dup_lanes(ki):
    return jnp.concatenate([ki, ki], axis=-1)


def kernel(x_prompt, x_sample, cache_k, cache_v, cache_idx_k, state_pool, meta_tokens, norm_mix_g, norm_ffn_g,
           norm_final_g, w_in, w_pool, pool_scale, w_attn_out, w_out, w_router_group, b_router_group,
           w_router_expert, b_router_expert, w_expert_gate, w_expert_up, w_expert_down):
    bsz, seq, d = x_prompt.shape
    dbsz, dec, _ = x_sample.shape
    assert w_in.shape[0] == 1, "single-layer step"
    l = 0
    past = cache_k.shape[2] - N_META
    n = N_META + seq
    topk_p = min(IDX_TOPK, seq // 4)
    topk_s = min(IDX_TOPK, (past + dec) // 4)
    kvw = N_KV_HEADS * HEAD_DIM

    tk = ATTN_TK
    tm = tk
    assert seq % tm == 0 and N_META <= tm
    sp = seq + tm
    nt = sp // tm
    tm_moe = _pick_tile(seq, 1100)
    w = _prep_weights(l, norm_mix_g, norm_ffn_g, norm_final_g, w_in, w_pool, pool_scale, w_attn_out, w_out,
                      w_router_group, b_router_group, w_router_expert, b_router_expert,
                      w_expert_gate, w_expert_up, w_expert_down)

    tab_p = _rope_table(jnp.arange(sp))
    xp, z, q, k_c, v_c, kb, vt, qis, sm, kid, sga, sgb = _inproj(
        x_prompt, meta_tokens.astype(F32), w["gmix"], tab_p, w, tm=tm, n_out=n, lead=N_META)
    b3 = lambda a: a.reshape(bsz, sp, a.shape[-1])
    o = _attention(b3(qis), b3(sm), b3(kid), b3(q), b3(kb), vt, tq=ATTN_TQ_PROMPT, n_q=seq, lead=N_META,
                   topk=topk_p, causal=True, n_keys=n)
    hs = POOL_STATE + 1
    z4 = z.reshape(bsz, nt, tm, d)
    halo = jnp.concatenate([jnp.zeros((bsz, 1, hs, d), F32), z4[:, :-1, tm - hs:, :]], axis=1).reshape(bsz * nt, hs, d)
    tbase = jnp.asarray(np.tile(np.arange(nt) * tm, bsz), I32)
    x1 = _mix(tbase, o, z, halo, sga, sgb, xp, w, tm=tm, lead=N_META)
    y_prompt = _moe(b3(x1), w, tm=tm_moe, n_out=seq, lead=N_META)
    k_prompt = k_c[None]
    v_prompt = v_c[None]
    idx_k_prompt = b3(sm)[:, :n, :IDX_DIM][None]
    pool_prompt = b3(z)[:, n - POOL_STATE:n][None]

    rows_s = dbsz * dec
    tab_s = jnp.tile(_rope_table(N_META + past + jnp.arange(dec)), (dbsz, 1))
    xs = x_sample.reshape(rows_s, d)
    _, zs, qs, ks_c, vs_c, kbs, _, qiss, sms, kids, sgas, sgbs = _inproj(
        xs[None], jnp.zeros((SUBLANES, d), F32), w["gmix"], tab_s, w, tm=rows_s, n_out=rows_s, lead=0)
    vs = vs_c.reshape(rows_s, kvw)
    nk = N_META + past + dec
    sk = _round_up(nk, tk)
    sq_s = _round_up(dec, ATTN_TQ)
    s3 = lambda a: a.reshape(dbsz, dec, a.shape[-1])
    padr = lambda a, rows: jnp.concatenate([a, jnp.zeros((dbsz, rows - a.shape[1], a.shape[-1]), a.dtype)], axis=1)
    k_all = padr(jnp.concatenate([cache_k[l].reshape(dbsz, N_META + past, kvw).astype(BF16), s3(kbs)], axis=1), sk)
    v_all = padr(jnp.concatenate([cache_v[l].reshape(dbsz, N_META + past, kvw).astype(BF16),
                                  s3(vs).astype(BF16)], axis=1), sk)
    vt_all = v_all.transpose(2, 0, 1).reshape(kvw, dbsz * sk)
    ki_all = padr(jnp.concatenate([_dup_lanes(cache_idx_k[l]).astype(BF16), s3(kids)], axis=1), sk)
    os_ = _attention(padr(s3(qiss), sq_s), padr(s3(sms), sq_s), ki_all, padr(s3(qs), sq_s), k_all, vt_all,
                     tq=ATTN_TQ, n_q=sq_s, lead=0, topk=topk_s, causal=False, n_keys=nk)[:, :dec]
    zcat = jnp.concatenate([state_pool[l].astype(F32), s3(zs)], axis=1)
    halo_s = jnp.concatenate([jnp.zeros((dbsz, 1, d), F32), zcat[:, :POOL_STATE]], axis=1)
    tbase_s = jnp.full((dbsz,), POOL_STATE, I32)
    x1s = _mix(tbase_s, os_, zs, halo_s, sgas, sgbs, xs, w, tm=dec, lead=0)
    y_sample = _moe(x1s[None], w, tm=rows_s, n_out=rows_s, lead=0).reshape(dbsz, dec, d)
    k_sample = ks_c.reshape(1, dbsz, dec, N_KV_HEADS, HEAD_DIM)
    v_sample = vs_c.reshape(1, dbsz, dec, N_KV_HEADS, HEAD_DIM)
    idx_k_sample = s3(sms)[:, :, :IDX_DIM][None]
    pool_sample = zcat[:, zcat.shape[1] - POOL_STATE:][None]

    return (y_prompt, y_sample, k_prompt, v_prompt, idx_k_prompt, pool_prompt,
            k_sample, v_sample, idx_k_sample, pool_sample)
```

```python
import functools
import math

import numpy as np
import jax
import jax.numpy as jnp
from jax import lax
from jax.experimental import pallas as pl
from jax.experimental.pallas import tpu as pltpu

CHUNK = 64
N_META = 16
EPS = 1e-6
POOL_WINDOWS = (2, 4, 8, 16)
POOL_STATE = max(POOL_WINDOWS) - 1
N_HEADS = 8
N_KV_HEADS = 2
HEAD_DIM = 128
KV_REP = N_HEADS // N_KV_HEADS
N_IDX_HEADS = 4
IDX_DIM = 64
IDX_TOPK = 256
IDX_W_SCALE = (N_IDX_HEADS * IDX_DIM) ** -0.5
ATTN_SCALE = HEAD_DIM ** -0.5
Q_PRESCALE = ATTN_SCALE * math.log2(math.e)
ROPE_THETA = 500000.0
ROT_FRACTION = 4
N_EXPERT_GROUPS = 4
EXPERTS_PER_GROUP = 8
N_EXPERTS = N_EXPERT_GROUPS * EXPERTS_PER_GROUP
D_EXPERT = 128

LANES = 128
SUBLANES = 8
BF16_SUBLANES = 16
VMEM_LIMIT_BYTES = 56 * 1024 * 1024

QK_HALF = HEAD_DIM // ROT_FRACTION // 2
IX_HALF = IDX_DIM // ROT_FRACTION // 2
TAB_COS_QK = 0
TAB_SIN_QK = QK_HALF
TAB_COS_IX = 2 * QK_HALF
TAB_SIN_IX = 2 * QK_HALF + IX_HALF

ATTN_TQ = LANES
ATTN_TQ_PROMPT = 2 * LANES
ATTN_TK = 512
TIE_BLOCK = 256
F32_TINY = float(np.finfo(np.float32).tiny)
MAX_PROBES = 64
NEG_BIG = -1e30

F32 = jnp.float32
BF16 = jnp.bfloat16
I32 = jnp.int32


def _round_up(x, m):
    return -(-x // m) * m


def _pick_tile(rows, cap):
    return max(t for t in range(BF16_SUBLANES, cap + 1, BF16_SUBLANES) if rows % t == 0)


def _rms(x, g):
    return x * lax.rsqrt(jnp.mean(x * x, axis=-1, keepdims=True) + EPS) * g


def _dot(a, b):
    return jnp.dot(a, b, preferred_element_type=F32)


def _dot_nt(a, b):
    return lax.dot_general(a, b, (((1,), (1,)), ((), ())), preferred_element_type=F32)


def _rope(y, c, a, b, shift_a, shift_b):
    outs = []
    for j in range(y.shape[1] // LANES):
        ys = y[:, j * LANES:(j + 1) * LANES]
        outs.append(ys * c + pltpu.roll(ys, shift_a, 1) * a + pltpu.roll(ys, shift_b, 1) * b)
    return outs[0] if len(outs) == 1 else jnp.concatenate(outs, axis=1)


def _inproj_kernel(x_ref, prev_ref, meta_ref, g_ref, tab_ref, wz_ref, wq_ref, wkv_ref, wqi_ref, wsm_ref, wgab_ref,
                   xs_ref, z_ref, q_ref, k4_ref, v4_ref, kb_ref, vt_ref, qis_ref, sm_ref, kid_ref, sga_ref, sgb_ref,
                   *, lead, frame_tiles):
    tm, d = x_ref.shape
    if lead:
        ti = pl.program_id(0) % (frame_tiles + 1)
        head = jnp.where(ti == 0, meta_ref[...], prev_ref[...])
        body = jnp.where(ti < frame_tiles, x_ref[0:tm - lead, :], 0.0)
        x = jnp.concatenate([head, body], axis=0)
    else:
        x = x_ref[...]
    xs_ref[...] = x
    h = _rms(x, g_ref[...]).astype(BF16)
    lane = lax.broadcasted_iota(I32, (tm, LANES), 1)
    t = tab_ref[...]

    def rl(s):
        return pltpu.roll(t, s % LANES, 1) if s % LANES else t

    c_qk = jnp.where(lane < QK_HALF, rl(-TAB_COS_QK), jnp.where(lane < 2 * QK_HALF, rl(QK_HALF - TAB_COS_QK), 1.0))
    a_qk = jnp.where(lane < QK_HALF, -rl(-TAB_SIN_QK), 0.0)
    b_qk = jnp.where((lane >= QK_HALF) & (lane < 2 * QK_HALF), rl(QK_HALF - TAB_SIN_QK), 0.0)
    l64 = lane & (IDX_DIM - 1)
    hi = lane >= IDX_DIM
    cos_lo = jnp.where(hi, rl(IDX_DIM - TAB_COS_IX), rl(-TAB_COS_IX))
    cos_hi = jnp.where(hi, rl(IDX_DIM + IX_HALF - TAB_COS_IX), rl(IX_HALF - TAB_COS_IX))
    sin_lo = jnp.where(hi, rl(IDX_DIM - TAB_SIN_IX), rl(-TAB_SIN_IX))
    sin_hi = jnp.where(hi, rl(IDX_DIM + IX_HALF - TAB_SIN_IX), rl(IX_HALF - TAB_SIN_IX))
    c_ix = jnp.where(l64 < IX_HALF, cos_lo, jnp.where(l64 < 2 * IX_HALF, cos_hi, 1.0))
    a_ix = jnp.where(l64 < IX_HALF, -sin_lo, 0.0)
    b_ix = jnp.where((l64 >= IX_HALF) & (l64 < 2 * IX_HALF), sin_hi, 0.0)

    z_ref[...] = _dot(h, wz_ref[...])

    q = _rope(_dot(h, wq_ref[...]), c_qk, a_qk, b_qk, LANES - QK_HALF, QK_HALF)
    q_ref[...] = (q * Q_PRESCALE).astype(BF16)

    kv = _dot(h, wkv_ref[...])
    kvw = kv.shape[1] // 2
    k = _rope(kv[:, :kvw], c_qk, a_qk, b_qk, LANES - QK_HALF, QK_HALF)
    v = kv[:, kvw:]
    for g in range(N_KV_HEADS):
        k4_ref[:, g, :] = k[:, g * HEAD_DIM:(g + 1) * HEAD_DIM]
        v4_ref[:, g, :] = v[:, g * HEAD_DIM:(g + 1) * HEAD_DIM]
    kb_ref[...] = k.astype(BF16)
    vt_ref[...] = v.T.astype(BF16)

    qi = _rope(_dot(h, wqi_ref[...]), c_ix, a_ix, b_ix, LANES - IX_HALF, IX_HALF)
    pieces = []
    for hd in range(N_IDX_HEADS):
        col = qi[:, (hd // 2) * LANES:(hd // 2 + 1) * LANES]
        keep = hi if hd % 2 else jnp.logical_not(hi)
        pieces.append(jnp.where(keep, col, 0.0))
    qis_ref[...] = jnp.concatenate(pieces, axis=1).astype(BF16)

    c_sm = jnp.where(hi, IDX_W_SCALE, c_ix)
    a_sm = jnp.where(hi, 0.0, a_ix)
    b_sm = jnp.where(hi, 0.0, b_ix)
    sm = _rope(_dot(h, wsm_ref[...]), c_sm, a_sm, b_sm, LANES - IX_HALF, IX_HALF)
    sm_ref[...] = sm
    kid_ref[...] = jnp.where(hi, pltpu.roll(sm, IDX_DIM, 1), sm).astype(BF16)

    gab = _dot(h, wgab_ref[...])
    sga_ref[...] = jax.nn.sigmoid(gab[:, :d])
    sgb_ref[...] = jax.nn.sigmoid(gab[:, d:])


def _inproj(x, meta, g, tab, w, *, tm, n_out, lead):
    bsz, s, d = x.shape
    assert s % tm == 0
    ft = s // tm
    nt = ft + 1 if lead else ft
    rows = bsz * nt * tm
    qw = w["wq"].shape[1]
    kvw = w["wkv"].shape[1] // 2
    lb = max(lead, SUBLANES)
    row_spec = lambda n: pl.BlockSpec((tm, n), lambda i: (i, 0))
    const = lambda a: pl.BlockSpec(a.shape, lambda i: (0,) * a.ndim)
    cache_spec = pl.BlockSpec((None, tm, N_KV_HEADS, HEAD_DIM), lambda i: (i // nt, i % nt, 0, 0))
    cache_shape = jax.ShapeDtypeStruct((bsz, n_out, N_KV_HEADS, HEAD_DIM), F32)
    outs = [
        (d, F32),
        (d, F32),
        (qw, BF16),
        "cache", "cache",
        (kvw, BF16),
        "vt",
        (N_IDX_HEADS * LANES, BF16),
        (LANES, F32),
        (LANES, BF16),
        (d, F32), (d, F32),
    ]
    spec_of = lambda o: (cache_spec if o == "cache" else pl.BlockSpec((kvw, tm), lambda i: (0, i)) if o == "vt"
                         else row_spec(o[0]))
    shape_of = lambda o: (cache_shape if o == "cache" else jax.ShapeDtypeStruct((kvw, rows), BF16) if o == "vt"
                          else jax.ShapeDtypeStruct((rows, o[0]), o[1]))
    return pl.pallas_call(
        functools.partial(_inproj_kernel, lead=lead, frame_tiles=ft),
        grid=(bsz * nt,),
        in_specs=[pl.BlockSpec((None, tm, d), lambda i: (i // nt, jnp.minimum(i % nt, ft - 1), 0)),
                  pl.BlockSpec((None, lb, d), lambda i: (i // nt, jnp.maximum((i % nt) * (tm // lb) - 1, 0), 0)),
                  const(meta), const(g),
                  pl.BlockSpec((tm, LANES), lambda i: (i % nt, 0)),
                  const(w["wz"]), const(w["wq"]), const(w["wkv"]), const(w["wqi"]), const(w["wsm"]), const(w["wgab"])],
        out_specs=[spec_of(o) for o in outs],
        out_shape=[shape_of(o) for o in outs],
        compiler_params=pltpu.CompilerParams(dimension_semantics=("arbitrary",), vmem_limit_bytes=VMEM_LIMIT_BYTES),
        name="inproj",
    )(x, x, meta, g, tab, w["wz"], w["wq"], w["wkv"], w["wqi"], w["wsm"], w["wgab"])


def _row_fold(x, op, group=SUBLANES):
    ways = 4
    accs = []
    for j in range(x.shape[0] // group):
        part = x[j * group:(j + 1) * group]
        if j < ways:
            accs.append(part)
        else:
            accs[j % ways] = op(accs[j % ways], part)
    while len(accs) > 1:
        accs = [op(accs[j], accs[j + 1]) for j in range(0, len(accs) - 1, 2)] + accs[len(accs) & ~1:]
    return accs[0]


def _attn_kernel(nch_ref, qis_ref, qis_nx_ref, sm_ref, sm_nx_ref, kid_ref, q_ref, q_nx_ref, k_ref, vt_ref, tri_ref,
                 o_ref, sc_scr, acc_scr, s_scr, *, tq, tk, topk, causal, n_keys, lead):
    i = pl.program_id(1)
    nch = nch_ref[i]
    sub = tri_ref.shape[0]
    nq4 = KV_REP * tq
    rows = lambda ref, nx: jnp.concatenate([ref[lead:, :], nx[...]], axis=0) if lead else ref[...]
    qis, sm, q = rows(qis_ref, qis_nx_ref), rows(sm_ref, sm_nx_ref), rows(q_ref, q_nx_ref)
    qpos = lead + i * tq + lax.broadcasted_iota(I32, (1, tq), 1)
    if causal:
        lim = N_META + CHUNK * (jnp.right_shift(qpos - N_META, int(math.log2(CHUNK))) + 1)
        lim = jnp.minimum(lim, n_keys)
    else:
        lim = jnp.full((1, tq), n_keys, I32)

    qstack = jnp.concatenate([qis[:, hd * LANES:(hd + 1) * LANES] for hd in range(N_IDX_HEADS)], axis=0)
    smt = sm.T
    wlane = jnp.concatenate([smt[IDX_DIM + hd:IDX_DIM + hd + 1, :] for hd in range(N_IDX_HEADS)], axis=1)

    key_row = lax.broadcasted_iota(I32, (tk, tq), 0)

    def score_body(c2, st):
        return score_chunk(jnp.minimum(2 * c2 + 1, nch - 1), score_chunk(2 * c2, st))

    def score_chunk(c, st):
        smax, smin = st
        off = pl.multiple_of(c * tk, tk)
        sc = _dot_nt(kid_ref[pl.ds(off, tk), :], qstack)
        sc = jnp.maximum(sc, 0.0) * wlane
        s = sc[:, 0:tq]
        for hd in range(1, N_IDX_HEADS):
            s = s + sc[:, hd * tq:(hd + 1) * tq]
        ok = key_row < lim - off
        masked = jnp.where(ok, s, -jnp.inf)
        sc_scr[pl.ds(off, tk), :] = masked
        return (jnp.maximum(smax, _row_fold(masked, jnp.maximum)),
                jnp.minimum(smin, _row_fold(jnp.where(ok, s, jnp.inf), jnp.minimum)))

    smax, smin = lax.fori_loop(0, (nch + 1) // 2, score_body, (jnp.full((SUBLANES, tq), -jnp.inf, F32),
                                                    jnp.full((SUBLANES, tq), jnp.inf, F32)))
    smax = jnp.max(smax, axis=0, keepdims=True)
    smin = jnp.min(smin, axis=0, keepdims=True)

    def count_ge(cand):
        def hits(c, cnd):
            kb = sc_scr[pl.ds(pl.multiple_of(c * tk, tk), tk), :]
            return _row_fold(jnp.where(kb >= cnd, 1.0, 0.0), jnp.add)

        def body(c2, acc):
            second = 2 * c2 + 1
            cnd2 = jnp.where(second < nch, cand, jnp.inf)
            return acc + (hits(2 * c2, cand) + hits(jnp.minimum(second, nch - 1), cnd2))
        acc = lax.fori_loop(0, (nch + 1) // 2, body, jnp.zeros((SUBLANES, tq), F32))
        return jnp.sum(acc, axis=0, keepdims=True)

    take_all = lim <= topk
    kf = float(topk)
    lo0 = jnp.where(take_all, NEG_BIG, smin)
    hi0 = jnp.where(take_all, NEG_BIG, smax + jnp.abs(smax) * 2.0 ** -10 + F32_TINY * 2.0 ** 24)
    c_lo0 = lim.astype(F32)
    c_hi0 = jnp.where(take_all, kf, 0.0)
    closed0 = jnp.where(take_all, 1.0, 0.0)

    def search_cond(st):
        return (st[0] < MAX_PROBES) & (jnp.max(1.0 - st[5]) > 0.5)

    def search_body(st):
        return search_step(search_step(st))

    def search_step(st):
        it, lo, hi, c_lo, c_hi, closed = st
        mid = lo + 0.5 * (hi - lo)
        interp = hi - (hi - lo) * ((kf - c_hi) / jnp.maximum(c_lo - c_hi, 1.0))
        inside = lambda v: (v > lo) & (v < hi)
        cand = jnp.where((it & 1) == 0, interp, mid)
        cand = jnp.where((it == 0) & inside(0.0), 0.0, cand)
        cand = jnp.where((it == 1) & inside(F32_TINY), F32_TINY, cand)
        cand = jnp.where(inside(cand), cand, mid)
        cnt = count_ge(cand)
        is_open = closed < 0.5
        up = is_open & (cnt > kf)
        down = is_open & (cnt <= kf)
        lo = jnp.where(up, cand, lo)
        c_lo = jnp.where(up, cnt, c_lo)
        hi = jnp.where(down, cand, hi)
        c_hi = jnp.where(down, cnt, c_hi)
        done = (c_hi == kf) | jnp.logical_not(inside(lo + 0.5 * (hi - lo))) | ((lo == 0.0) & (hi == F32_TINY))
        return it + 1, lo, hi, c_lo, c_hi, jnp.where(done, 1.0, closed)

    _, thr_lo, thr_hi, _, c_hi, _ = lax.while_loop(search_cond, search_body,
                                                   (jnp.int32(0), lo0, hi0, c_lo0, c_hi0, closed0))
    n_tie = kf - c_hi

    acc_scr[...] = jnp.zeros(acc_scr.shape, F32)
    qgs = [jnp.concatenate([q[:, (g * KV_REP + r) * HEAD_DIM:(g * KV_REP + r + 1) * HEAD_DIM]
                            for r in range(KV_REP)], axis=0) for g in range(N_KV_HEADS)]

    ones_rows = jnp.ones((BF16_SUBLANES, tk), BF16)

    def logits(c, slot, seen):
        off = pl.multiple_of(c * tk, tk)
        kb = sc_scr[pl.ds(off, tk), :]
        above = kb >= thr_hi
        at_least = kb >= thr_lo
        eqf = jnp.where(at_least, 1.0, 0.0) - jnp.where(above, 1.0, 0.0)
        ranks = []
        for hf in range(tk // sub):
            e = eqf[hf * sub:(hf + 1) * sub]
            ranks.append(seen + _dot(tri_ref[...], e.astype(BF16)))
            seen = seen + jnp.sum(_row_fold(e, jnp.add), axis=0, keepdims=True)
        rank = jnp.concatenate(ranks, axis=0)
        maskf = jnp.where(above, 0.0, jnp.where(at_least, jnp.where(rank < n_tie, 0.0, NEG_BIG), NEG_BIG))
        mask4 = jnp.concatenate([maskf] * KV_REP, axis=1)
        mcs = []
        for g in range(N_KV_HEADS):
            kc = k_ref[pl.ds(off, tk), g * HEAD_DIM:(g + 1) * HEAD_DIM]
            s = _dot_nt(kc, qgs[g]) + mask4
            s_scr[slot, g] = s
            mcs.append(jnp.max(_row_fold(s, jnp.maximum), axis=0, keepdims=True))
        return seen, tuple(mcs)

    def update(c, slot, ms, mcs):
        off = pl.multiple_of(c * tk, tk)
        ms_n = []
        for g in range(N_KV_HEADS):
            vtc = vt_ref[g * HEAD_DIM:(g + 1) * HEAD_DIM, pl.ds(off, tk)]
            m_new = jnp.maximum(ms[g], mcs[g])
            alpha = jnp.exp2(ms[g] - m_new)
            p = jnp.exp2(s_scr[slot, g] - m_new).astype(BF16)
            acc_scr[g] = alpha * acc_scr[g] + _dot(jnp.concatenate([vtc, ones_rows], axis=0), p)
            ms_n.append(m_new)
        return tuple(ms_n)

    def flash_body(c, st):
        seen, ms, mcs = st
        slot = c & 1
        ms = update(c, slot, ms, mcs)
        seen, mcs = logits(c + 1, 1 - slot, seen)
        return seen, ms, mcs

    m0 = tuple(jnp.full((1, nq4), NEG_BIG, F32) for _ in range(N_KV_HEADS))
    seen0, mc0 = logits(0, 0, jnp.zeros((1, tq), F32))
    _, ms, mcs = lax.fori_loop(0, nch - 1, flash_body, (seen0, m0, mc0))
    update(nch - 1, (nch - 1) & 1, ms, mcs)

    for g in range(N_KV_HEADS):
        og = acc_scr[g, 0:HEAD_DIM] / acc_scr[g, HEAD_DIM:HEAD_DIM + 1]
        for r in range(KV_REP):
            hd = g * KV_REP + r
            o_ref[:, hd * HEAD_DIM:(hd + 1) * HEAD_DIM] = og[:, r * tq:(r + 1) * tq].T.astype(o_ref.dtype)


def _attention(qis, sm, kid, q, kb, vt, *, tq, n_q, lead, topk, causal, n_keys):
    tk, sub = ATTN_TK, TIE_BLOCK
    bsz, sq, _ = q.shape
    sk = kb.shape[1]
    assert n_q % tq == 0 and sk % tk == 0 and n_q + lead <= sq
    nq = n_q // tq
    if causal:
        ends = np.minimum(N_META + CHUNK * ((lead + np.arange(nq) * tq + tq - 1 - N_META) // CHUNK + 1), n_keys)
    else:
        ends = np.full((nq,), n_keys)
    nch = jnp.asarray(-(-ends // tk), I32)
    tri = jnp.asarray(np.tril(np.ones((sub, sub), np.float32), -1), BF16)
    lb = max(lead, BF16_SUBLANES)
    qrow = lambda n: pl.BlockSpec((None, tq, n), lambda b, i, s: (b, i, 0))
    qnext = lambda n: pl.BlockSpec((None, lb, n), lambda b, i, s: (b, jnp.minimum((i + 1) * (tq // lb), sq // lb - 1), 0))
    krow = lambda n: pl.BlockSpec((None, sk, n), lambda b, i, s: (b, 0, 0), pipeline_mode=pl.Buffered(1))
    kern = functools.partial(_attn_kernel, tq=tq, tk=tk, topk=topk, causal=causal, n_keys=n_keys, lead=lead)
    return pl.pallas_call(
        kern,
        grid_spec=pltpu.PrefetchScalarGridSpec(
            num_scalar_prefetch=1,
            grid=(bsz, nq),
            in_specs=[qrow(qis.shape[2]), qnext(qis.shape[2]), qrow(sm.shape[2]), qnext(sm.shape[2]),
                      krow(kid.shape[2]), qrow(q.shape[2]), qnext(q.shape[2]),
                      krow(kb.shape[2]),
                      pl.BlockSpec((vt.shape[0], sk), lambda b, i, s: (0, b), pipeline_mode=pl.Buffered(1)),
                      pl.BlockSpec((sub, sub), lambda b, i, s: (0, 0))],
            out_specs=qrow(q.shape[2]),
            scratch_shapes=[pltpu.VMEM((sk, tq), F32),
                            pltpu.VMEM((N_KV_HEADS, HEAD_DIM + BF16_SUBLANES, KV_REP * tq), F32),
                            pltpu.VMEM((2, N_KV_HEADS, tk, KV_REP * tq), F32)]),
        out_shape=jax.ShapeDtypeStruct((bsz, n_q, q.shape[2]), BF16),
        compiler_params=pltpu.CompilerParams(dimension_semantics=("arbitrary", "arbitrary"),
                                             vmem_limit_bytes=VMEM_LIMIT_BYTES),
        name="attn",
    )(nch, qis, qis, sm, sm, kid, q, q, kb, vt, tri)


def _mix_kernel(tb_ref, o_ref, oprev_ref, z_ref, halo_ref, sga_ref, sgb_ref, x_ref, wpool_ref, ps_ref, wao_ref,
                wout_ref, x1_ref, zc_scr, *, tm, lead, tiles):
    if lead:
        head = jnp.where(pl.program_id(0) % tiles == 0, jnp.zeros_like(oprev_ref[...]), oprev_ref[...])
        o = jnp.concatenate([head, o_ref[0:tm - lead, :]], axis=0)
    else:
        o = o_ref[...]
    hs = halo_ref.shape[0]
    zc_scr[0:hs, :] = halo_ref[...]
    zc_scr[hs:hs + tm, :] = z_ref[...]
    avail = tb_ref[pl.program_id(0)] + lax.broadcasted_iota(I32, (tm, 1), 0) + 1
    gw = z_ref.shape[1] // len(POOL_WINDOWS)
    ys = []
    for g, win in enumerate(POOL_WINDOWS):
        cols = slice(g * gw, (g + 1) * gw)
        zt = zc_scr[hs:hs + tm, cols]
        s = zt
        for j in range(1, win):
            s = s + zc_scr[hs - j:hs - j + tm, cols]
        cnt = jnp.minimum(avail, win).astype(F32)
        dlt = s / cnt - zt
        ys.append(_dot(dlt.astype(BF16), wpool_ref[g]))
    a = jnp.concatenate(ys, axis=1) * ps_ref[...]
    b = _dot(o, wao_ref[...])
    mrg = sga_ref[...] * a + sgb_ref[...] * b
    x1_ref[...] = x_ref[...] + _dot(mrg.astype(BF16), wout_ref[...])


def _mix(tbase, o, z, halo, sga, sgb, x, w, *, tm, lead):
    rows, d = x.shape
    bsz, so, _ = o.shape
    nt = rows // tm
    tiles = nt // bsz
    ot = so // tm
    hs = halo.shape[1]
    lb = max(lead, BF16_SUBLANES)
    row = lambda n: pl.BlockSpec((tm, n), lambda i, s: (i, 0))
    const = lambda a: pl.BlockSpec(a.shape, lambda i, s: (0,) * a.ndim)
    o_cur = pl.BlockSpec((None, tm, d), lambda i, s: (i // tiles, jnp.minimum(i % tiles, ot - 1), 0))
    o_prev = pl.BlockSpec((None, lb, d), lambda i, s: (i // tiles, jnp.maximum((i % tiles) * (tm // lb) - 1, 0), 0))
    return pl.pallas_call(
        functools.partial(_mix_kernel, tm=tm, lead=lead, tiles=tiles),
        grid_spec=pltpu.PrefetchScalarGridSpec(
            num_scalar_prefetch=1,
            grid=(nt,),
            in_specs=[o_cur, o_prev, row(d), pl.BlockSpec((None, hs, d), lambda i, s: (i, 0, 0)), row(d), row(d), row(d),
                      const(w["wpool"]), const(w["pscale"]), const(w["wao"]), const(w["wout"])],
            out_specs=row(d),
            scratch_shapes=[pltpu.VMEM((hs + tm, d), F32)]),
        out_shape=jax.ShapeDtypeStruct((rows, d), F32),
        compiler_params=pltpu.CompilerParams(dimension_semantics=("arbitrary",), vmem_limit_bytes=VMEM_LIMIT_BYTES),
        name="mix",
    )(tbase, o, o, z, halo, sga, sgb, x, w["wpool"], w["pscale"], w["wao"], w["wout"])


ROUTE_GROUP_LANE0 = N_EXPERTS


def _moe_kernel(x_ref, nxt_ref, gf_ref, gl_ref, wr_ref, br_ref, wg_ref, wu_ref, wd_ref, y_ref,
                x_scr, h_scr, route_scr, acc_scr, *, ec, lead):
    j = pl.program_id(1)
    tm = x_ref.shape[0]

    @pl.when(j == 0)
    def _():
        x_scr[...] = jnp.concatenate([x_ref[lead:, :], nxt_ref[...]], axis=0) if lead else x_ref[...]
        hf = _rms(x_scr[...], gf_ref[...])
        h_hi = hf.astype(BF16)
        h_lo = (hf - h_hi.astype(F32)).astype(BF16)
        h_scr[...] = h_hi
        hh = _dot(h_hi, wr_ref[...])
        logits = hh[:, :LANES] + hh[:, LANES:] + _dot(h_lo, wr_ref[:, :LANES]) + br_ref[...]
        lane = lax.broadcasted_iota(I32, (tm, LANES), 1)
        lanef = lane.astype(F32)
        ninf = -jnp.inf
        isg = (lane >= ROUTE_GROUP_LANE0) & (lane < ROUTE_GROUP_LANE0 + N_EXPERT_GROUPS)
        gl = jnp.where(isg, logits, ninf)
        gmax = jnp.max(gl, axis=1, keepdims=True)
        gsel = jnp.min(jnp.where(gl == gmax, lanef, 1e9), axis=1, keepdims=True)
        gprob = 1.0 / jnp.sum(jnp.exp(gl - gmax), axis=1, keepdims=True)
        e0 = (gsel - ROUTE_GROUP_LANE0) * EXPERTS_PER_GROUP
        el = jnp.where((lanef >= e0) & (lanef < e0 + EXPERTS_PER_GROUP), logits, ninf)
        v1 = jnp.max(el, axis=1, keepdims=True)
        i1 = jnp.min(jnp.where(el == v1, lanef, 1e9), axis=1, keepdims=True)
        el2 = jnp.where(lanef == i1, ninf, el)
        v2 = jnp.max(el2, axis=1, keepdims=True)
        i2 = jnp.min(jnp.where(el2 == v2, lanef, 1e9), axis=1, keepdims=True)
        e = jnp.exp(v2 - v1)
        w1 = (1.0 / (1.0 + e)) * gprob
        w2 = (e / (1.0 + e)) * gprob
        for n, val in enumerate((i1, i2, w1, w2)):
            route_scr[n] = jnp.broadcast_to(val, (tm, LANES))
        acc_scr[...] = jnp.zeros(acc_scr.shape, F32)

    h = h_scr[...]
    i1, i2, w1, w2 = route_scr[0], route_scr[1], route_scr[2], route_scr[3]
    gate = _dot(h, wg_ref[...])
    up = _dot(h, wu_ref[...])
    act = gate * jax.nn.sigmoid(gate) * up
    parts = []
    for q in range(ec):
        ef = jnp.full((tm, LANES), j * ec + q, I32).astype(F32)
        comb = jnp.where(i1 == ef, w1, 0.0) + jnp.where(i2 == ef, w2, 0.0)
        parts.append(act[:, q * D_EXPERT:(q + 1) * D_EXPERT] * comb)
    acc_scr[...] += _dot(jnp.concatenate(parts, axis=1).astype(BF16), wd_ref[...])

    @pl.when(j == pl.num_programs(1) - 1)
    def _():
        y_ref[...] = _rms(x_scr[...] + acc_scr[...], gl_ref[...])


def _moe(x, w, *, tm, n_out, lead, ec=EXPERTS_PER_GROUP):
    bsz, s, d = x.shape
    assert n_out % tm == 0 and n_out + lead <= s
    nt = n_out // tm
    ne = N_EXPERTS // ec
    cw = ec * D_EXPERT
    lb = max(lead, SUBLANES)
    const = lambda a: pl.BlockSpec(a.shape, lambda i, j: (0,) * a.ndim)
    tile = pl.BlockSpec((None, tm, d), lambda i, j: (i // nt, i % nt, 0))
    after = pl.BlockSpec((None, lb, d), lambda i, j: (i // nt, jnp.minimum((i % nt + 1) * (tm // lb), s // lb - 1), 0))
    return pl.pallas_call(
        functools.partial(_moe_kernel, ec=ec, lead=lead),
        grid=(bsz * nt, ne),
        in_specs=[tile, after, const(w["gffn"]), const(w["gfinal"]), const(w["wr"]), const(w["br"]),
                  pl.BlockSpec((d, cw), lambda i, j: (0, j)), pl.BlockSpec((d, cw), lambda i, j: (0, j)),
                  pl.BlockSpec((cw, d), lambda i, j: (j, 0))],
        out_specs=tile,
        out_shape=jax.ShapeDtypeStruct((bsz, n_out, d), F32),
        scratch_shapes=[pltpu.VMEM((tm, d), F32), pltpu.VMEM((tm, d), BF16), pltpu.VMEM((4, tm, LANES), F32),
                        pltpu.VMEM((tm, d), F32)],
        compiler_params=pltpu.CompilerParams(dimension_semantics=("arbitrary", "arbitrary"),
                                             vmem_limit_bytes=VMEM_LIMIT_BYTES),
        name="moe",
    )(x, x, w["gffn"], w["gfinal"], w["wr"], w["br"], w["wg"], w["wu"], w["wd"])


def _rope_table(pos):
    posf = pos.astype(F32)[:, None]

    def cs(half):
        inv = jnp.exp(-math.log(ROPE_THETA) * jnp.arange(half, dtype=F32) * (1.0 / half))
        ang = posf * inv[None, :]
        return jnp.cos(ang), jnp.sin(ang)

    cq, sq = cs(QK_HALF)
    ci, si = cs(IX_HALF)
    used = 2 * QK_HALF + 2 * IX_HALF
    return jnp.concatenate([cq, sq, ci, si, jnp.zeros((pos.shape[0], LANES - used), F32)], axis=1)


def _prep_weights(l, norm_mix_g, norm_ffn_g, norm_final_g, w_in, w_pool, pool_scale, w_attn_out, w_out,
                  w_router_group, b_router_group, w_router_expert, b_router_expert,
                  w_expert_gate, w_expert_up, w_expert_down):
    d = w_in.shape[1]
    pw = w_pool.shape[1] * w_pool.shape[2]
    qw = N_HEADS * HEAD_DIM
    kvw = N_KV_HEADS * HEAD_DIM
    widths = (pw, qw, kvw, kvw, N_IDX_HEADS * IDX_DIM, IDX_DIM, N_IDX_HEADS, d, d)
    offs = np.concatenate([[0], np.cumsum(widths)])
    wi = w_in[l]
    piece = lambda a, b: wi[:, offs[a]:offs[b]].astype(BF16)
    wsm = jnp.concatenate([wi[:, offs[5]:offs[7]], jnp.zeros((d, LANES - IDX_DIM - N_IDX_HEADS), F32)], axis=1)
    wr = jnp.concatenate([w_router_expert[l], w_router_group[l],
                          jnp.zeros((d, LANES - N_EXPERTS - N_EXPERT_GROUPS), F32)], axis=1)
    wr_hi = wr.astype(BF16)
    br = jnp.concatenate([b_router_expert[l], b_router_group[l],
                          jnp.zeros((LANES - N_EXPERTS - N_EXPERT_GROUPS,), F32)])[None, :]
    flat_e = lambda a: a.transpose(1, 0, 2).reshape(d, N_EXPERTS * D_EXPERT).astype(BF16)
    return dict(
        gmix=norm_mix_g[l][None, :], gffn=norm_ffn_g[l][None, :], gfinal=norm_final_g[None, :],
        wz=piece(0, 1), wq=piece(1, 2), wkv=piece(2, 4), wqi=piece(4, 5), wsm=wsm.astype(BF16), wgab=piece(7, 9),
        wpool=w_pool[l].astype(BF16), pscale=pool_scale[l][None, :],
        wao=w_attn_out[l].astype(BF16), wout=w_out[l].astype(BF16),
        wr=jnp.concatenate([wr_hi, (wr - wr_hi.astype(F32)).astype(BF16)], axis=1), br=br,
        wg=flat_e(w_expert_gate[l]), wu=flat_e(w_expert_up[l]),
        wd=w_expert_down[l].reshape(N_EXPERTS * D_EXPERT, d).astype(BF16),
    )


def _dup_lanes(ki):
    return jnp.concatenate([ki, ki], axis=-1)


def kernel(x_prompt, x_sample, cache_k, cache_v, cache_idx_k, state_pool, meta_tokens, norm_mix_g, norm_ffn_g,
           norm_final_g, w_in, w_pool, pool_scale, w_attn_out, w_out, w_router_group, b_router_group,
           w_router_expert, b_router_expert, w_expert_gate, w_expert_up, w_expert_down):
    bsz, seq, d = x_prompt.shape
    dbsz, dec, _ = x_sample.shape
    assert w_in.shape[0] == 1, "single-layer step"
    l = 0
    past = cache_k.shape[2] - N_META
    n = N_META + seq
    topk_p = min(IDX_TOPK, seq // 4)
    topk_s = min(IDX_TOPK, (past + dec) // 4)
    kvw = N_KV_HEADS * HEAD_DIM

    tk = ATTN_TK
    tm = tk
    assert seq % tm == 0 and N_META <= tm
    sp = seq + tm
    nt = sp // tm
    tm_moe = _pick_tile(seq, 1100)
    w = _prep_weights(l, norm_mix_g, norm_ffn_g, norm_final_g, w_in, w_pool, pool_scale, w_attn_out, w_out,
                      w_router_group, b_router_group, w_router_expert, b_router_expert,
                      w_expert_gate, w_expert_up, w_expert_down)

    tab_p = _rope_table(jnp.arange(sp))
    xp, z, q, k_c, v_c, kb, vt, qis, sm, kid, sga, sgb = _inproj(
        x_prompt, meta_tokens.astype(F32), w["gmix"], tab_p, w, tm=tm, n_out=n, lead=N_META)
    b3 = lambda a: a.reshape(bsz, sp, a.shape[-1])
    o = _attention(b3(qis), b3(sm), b3(kid), b3(q), b3(kb), vt, tq=ATTN_TQ_PROMPT, n_q=seq, lead=N_META,
                   topk=topk_p, causal=True, n_keys=n)
    hs = POOL_STATE + 1
    z4 = z.reshape(bsz, nt, tm, d)
    halo = jnp.concatenate([jnp.zeros((bsz, 1, hs, d), F32), z4[:, :-1, tm - hs:, :]], axis=1).reshape(bsz * nt, hs, d)
    tbase = jnp.asarray(np.tile(np.arange(nt) * tm, bsz), I32)
    x1 = _mix(tbase, o, z, halo, sga, sgb, xp, w, tm=tm, lead=N_META)
    y_prompt = _moe(b3(x1), w, tm=tm_moe, n_out=seq, lead=N_META)
    k_prompt = k_c[None]
    v_prompt = v_c[None]
    idx_k_prompt = b3(sm)[:, :n, :IDX_DIM][None]
    pool_prompt = b3(z)[:, n - POOL_STATE:n][None]

    rows_s = dbsz * dec
    tab_s = jnp.tile(_rope_table(N_META + past + jnp.arange(dec)), (dbsz, 1))
    xs = x_sample.reshape(rows_s, d)
    _, zs, qs, ks_c, vs_c, kbs, _, qiss, sms, kids, sgas, sgbs = _inproj(
        xs[None], jnp.zeros((SUBLANES, d), F32), w["gmix"], tab_s, w, tm=rows_s, n_out=rows_s, lead=0)
    vs = vs_c.reshape(rows_s, kvw)
    nk = N_META + past + dec
    sk = _round_up(nk, tk)
    sq_s = _round_up(dec, ATTN_TQ)
    s3 = lambda a: a.reshape(dbsz, dec, a.shape[-1])
    padr = lambda a, rows: jnp.concatenate([a, jnp.zeros((dbsz, rows - a.shape[1], a.shape[-1]), a.dtype)], axis=1)
    k_all = padr(jnp.concatenate([cache_k[l].reshape(dbsz, N_META + past, kvw).astype(BF16), s3(kbs)], axis=1), sk)
    v_all = padr(jnp.concatenate([cache_v[l].reshape(dbsz, N_META + past, kvw).astype(BF16),
                                  s3(vs).astype(BF16)], axis=1), sk)
    vt_all = v_all.transpose(2, 0, 1).reshape(kvw, dbsz * sk)
    ki_all = padr(jnp.concatenate([_dup_lanes(cache_idx_k[l]).astype(BF16), s3(kids)], axis=1), sk)
    os_ = _attention(padr(s3(qiss), sq_s), padr(s3(sms), sq_s), ki_all, padr(s3(qs), sq_s), k_all, vt_all,
                     tq=ATTN_TQ, n_q=sq_s, lead=0, topk=topk_s, causal=False, n_keys=nk)[:, :dec]
    zcat = jnp.concatenate([state_pool[l].astype(F32), s3(zs)], axis=1)
    halo_s = jnp.concatenate([jnp.zeros((dbsz, 1, d), F32), zcat[:, :POOL_STATE]], axis=1)
    tbase_s = jnp.full((dbsz,), POOL_STATE, I32)
    x1s = _mix(tbase_s, os_, zs, halo_s, sgas, sgbs, xs, w, tm=dec, lead=0)
    y_sample = _moe(x1s[None], w, tm=rows_s, n_out=rows_s, lead=0).reshape(dbsz, dec, d)
    k_sample = ks_c.reshape(1, dbsz, dec, N_KV_HEADS, HEAD_DIM)
    v_sample = vs_c.reshape(1, dbsz, dec, N_KV_HEADS, HEAD_DIM)
    idx_k_sample = s3(sms)[:, :, :IDX_DIM][None]
    pool_sample = zcat[:, zcat.shape[1] - POOL_STATE:][None]

    return (y_prompt, y_sample, k_prompt, v_prompt, idx_k_prompt, pool_prompt,
            k_sample, v_sample, idx_k_sample, pool_sample)
```

```python
import functools
import math

import numpy as np
import jax
import jax.numpy as jnp
from jax import lax
from jax.experimental import pallas as pl
from jax.experimental.pallas import tpu as pltpu

CHUNK = 64
N_META = 16
EPS = 1e-6
POOL_WINDOWS = (2, 4, 8, 16)
POOL_STATE = max(POOL_WINDOWS) - 1
N_HEADS = 8
N_KV_HEADS = 2
HEAD_DIM = 128
KV_REP = N_HEADS // N_KV_HEADS
N_IDX_HEADS = 4
IDX_DIM = 64
IDX_TOPK = 256
IDX_W_SCALE = (N_IDX_HEADS * IDX_DIM) ** -0.5
ATTN_SCALE = HEAD_DIM ** -0.5
Q_PRESCALE = ATTN_SCALE * math.log2(math.e)
ROPE_THETA = 500000.0
ROT_FRACTION = 4
N_EXPERT_GROUPS = 4
EXPERTS_PER_GROUP = 8
N_EXPERTS = N_EXPERT_GROUPS * EXPERTS_PER_GROUP
D_EXPERT = 128

LANES = 128
SUBLANES = 8
BF16_SUBLANES = 16
VMEM_LIMIT_BYTES = 56 * 1024 * 1024

QK_HALF = HEAD_DIM // ROT_FRACTION // 2
IX_HALF = IDX_DIM // ROT_FRACTION // 2
TAB_COS_QK = 0
TAB_SIN_QK = QK_HALF
TAB_COS_IX = 2 * QK_HALF
TAB_SIN_IX = 2 * QK_HALF + IX_HALF

ATTN_TQ = LANES
ATTN_TQ_PROMPT = 2 * LANES
ATTN_TK = 512
TIE_BLOCK = 256
F32_TINY = float(np.finfo(np.float32).tiny)
MAX_PROBES = 64
NEG_BIG = -1e30

F32 = jnp.float32
BF16 = jnp.bfloat16
I32 = jnp.int32


def _round_up(x, m):
    return -(-x // m) * m


def _pick_tile(rows, cap):
    return max(t for t in range(BF16_SUBLANES, cap + 1, BF16_SUBLANES) if rows % t == 0)


def _rms(x, g):
    return x * lax.rsqrt(jnp.mean(x * x, axis=-1, keepdims=True) + EPS) * g


def _dot(a, b):
    return jnp.dot(a, b, preferred_element_type=F32)


def _dot_nt(a, b):
    return lax.dot_general(a, b, (((1,), (1,)), ((), ())), preferred_element_type=F32)


def _rope(y, c, a, b, shift_a, shift_b):
    outs = []
    for j in range(y.shape[1] // LANES):
        ys = y[:, j * LANES:(j + 1) * LANES]
        outs.append(ys * c + pltpu.roll(ys, shift_a, 1) * a + pltpu.roll(ys, shift_b, 1) * b)
    return outs[0] if len(outs) == 1 else jnp.concatenate(outs, axis=1)


def _inproj_kernel(x_ref, prev_ref, meta_ref, g_ref, tab_ref, wz_ref, wq_ref, wkv_ref, wqi_ref, wsm_ref, wgab_ref,
                   xs_ref, z_ref, q_ref, k4_ref, v4_ref, kb_ref, vt_ref, qis_ref, sm_ref, kid_ref, sga_ref, sgb_ref,
                   *, lead, frame_tiles):
    tm, d = x_ref.shape
    if lead:
        ti = pl.program_id(0) % (frame_tiles + 1)
        head = jnp.where(ti == 0, meta_ref[...], prev_ref[...])
        body = jnp.where(ti < frame_tiles, x_ref[0:tm - lead, :], 0.0)
        x = jnp.concatenate([head, body], axis=0)
    else:
        x = x_ref[...]
    xs_ref[...] = x
    h = _rms(x, g_ref[...]).astype(BF16)
    lane = lax.broadcasted_iota(I32, (tm, LANES), 1)
    t = tab_ref[...]

    def rl(s):
        return pltpu.roll(t, s % LANES, 1) if s % LANES else t

    c_qk = jnp.where(lane < QK_HALF, rl(-TAB_COS_QK), jnp.where(lane < 2 * QK_HALF, rl(QK_HALF - TAB_COS_QK), 1.0))
    a_qk = jnp.where(lane < QK_HALF, -rl(-TAB_SIN_QK), 0.0)
    b_qk = jnp.where((lane >= QK_HALF) & (lane < 2 * QK_HALF), rl(QK_HALF - TAB_SIN_QK), 0.0)
    l64 = lane & (IDX_DIM - 1)
    hi = lane >= IDX_DIM
    cos_lo = jnp.where(hi, rl(IDX_DIM - TAB_COS_IX), rl(-TAB_COS_IX))
    cos_hi = jnp.where(hi, rl(IDX_DIM + IX_HALF - TAB_COS_IX), rl(IX_HALF - TAB_COS_IX))
    sin_lo = jnp.where(hi, rl(IDX_DIM - TAB_SIN_IX), rl(-TAB_SIN_IX))
    sin_hi = jnp.where(hi, rl(IDX_DIM + IX_HALF - TAB_SIN_IX), rl(IX_HALF - TAB_SIN_IX))
    c_ix = jnp.where(l64 < IX_HALF, cos_lo, jnp.where(l64 < 2 * IX_HALF, cos_hi, 1.0))
    a_ix = jnp.where(l64 < IX_HALF, -sin_lo, 0.0)
    b_ix = jnp.where((l64 >= IX_HALF) & (l64 < 2 * IX_HALF), sin_hi, 0.0)

    z_ref[...] = _dot(h, wz_ref[...])

    q = _rope(_dot(h, wq_ref[...]), c_qk, a_qk, b_qk, LANES - QK_HALF, QK_HALF)
    q_ref[...] = (q * Q_PRESCALE).astype(BF16)

    kv = _dot(h, wkv_ref[...])
    kvw = kv.shape[1] // 2
    k = _rope(kv[:, :kvw], c_qk, a_qk, b_qk, LANES - QK_HALF, QK_HALF)
    v = kv[:, kvw:]
    for g in range(N_KV_HEADS):
        k4_ref[:, g, :] = k[:, g * HEAD_DIM:(g + 1) * HEAD_DIM]
        v4_ref[:, g, :] = v[:, g * HEAD_DIM:(g + 1) * HEAD_DIM]
    kb_ref[...] = k.astype(BF16)
    vt_ref[...] = v.T.astype(BF16)

    qi = _rope(_dot(h, wqi_ref[...]), c_ix, a_ix, b_ix, LANES - IX_HALF, IX_HALF)
    pieces = []
    for hd in range(N_IDX_HEADS):
        col = qi[:, (hd // 2) * LANES:(hd // 2 + 1) * LANES]
        keep = hi if hd % 2 else jnp.logical_not(hi)
        pieces.append(jnp.where(keep, col, 0.0))
    qis_ref[...] = jnp.concatenate(pieces, axis=1).astype(BF16)

    c_sm = jnp.where(hi, IDX_W_SCALE, c_ix)
    a_sm = jnp.where(hi, 0.0, a_ix)
    b_sm = jnp.where(hi, 0.0, b_ix)
    sm = _rope(_dot(h, wsm_ref[...]), c_sm, a_sm, b_sm, LANES - IX_HALF, IX_HALF)
    sm_ref[...] = sm
    kid_ref[...] = jnp.where(hi, pltpu.roll(sm, IDX_DIM, 1), sm).astype(BF16)

    gab = _dot(h, wgab_ref[...])
    sga_ref[...] = jax.nn.sigmoid(gab[:, :d])
    sgb_ref[...] = jax.nn.sigmoid(gab[:, d:])


def _inproj(x, meta, g, tab, w, *, tm, n_out, lead):
    bsz, s, d = x.shape
    assert s % tm == 0
    ft = s // tm
    nt = ft + 1 if lead else ft
    rows = bsz * nt * tm
    qw = w["wq"].shape[1]
    kvw = w["wkv"].shape[1] // 2
    lb = max(lead, SUBLANES)
    row_spec = lambda n: pl.BlockSpec((tm, n), lambda i: (i, 0))
    const = lambda a: pl.BlockSpec(a.shape, lambda i: (0,) * a.ndim)
    cache_spec = pl.BlockSpec((None, tm, N_KV_HEADS, HEAD_DIM), lambda i: (i // nt, i % nt, 0, 0))
    cache_shape = jax.ShapeDtypeStruct((bsz, n_out, N_KV_HEADS, HEAD_DIM), F32)
    outs = [
        (d, F32),
        (d, F32),
        (qw, BF16),
        "cache", "cache",
        (kvw, BF16),
        "vt",
        (N_IDX_HEADS * LANES, BF16),
        (LANES, F32),
        (LANES, BF16),
        (d, F32), (d, F32),
    ]
    spec_of = lambda o: (cache_spec if o == "cache" else pl.BlockSpec((kvw, tm), lambda i: (0, i)) if o == "vt"
                         else row_spec(o[0]))
    shape_of = lambda o: (cache_shape if o == "cache" else jax.ShapeDtypeStruct((kvw, rows), BF16) if o == "vt"
                          else jax.ShapeDtypeStruct((rows, o[0]), o[1]))
    return pl.pallas_call(
        functools.partial(_inproj_kernel, lead=lead, frame_tiles=ft),
        grid=(bsz * nt,),
        in_specs=[pl.BlockSpec((None, tm, d), lambda i: (i // nt, jnp.minimum(i % nt, ft - 1), 0)),
                  pl.BlockSpec((None, lb, d), lambda i: (i // nt, jnp.maximum((i % nt) * (tm // lb) - 1, 0), 0)),
                  const(meta), const(g),
                  pl.BlockSpec((tm, LANES), lambda i: (i % nt, 0)),
                  const(w["wz"]), const(w["wq"]), const(w["wkv"]), const(w["wqi"]), const(w["wsm"]), const(w["wgab"])],
        out_specs=[spec_of(o) for o in outs],
        out_shape=[shape_of(o) for o in outs],
        compiler_params=pltpu.CompilerParams(dimension_semantics=("arbitrary",), vmem_limit_bytes=VMEM_LIMIT_BYTES),
        name="inproj",
    )(x, x, meta, g, tab, w["wz"], w["wq"], w["wkv"], w["wqi"], w["wsm"], w["wgab"])


def _row_fold(x, op, group=SUBLANES):
    ways = 4
    accs = []
    for j in range(x.shape[0] // group):
        part = x[j * group:(j + 1) * group]
        if j < ways:
            accs.append(part)
        else:
            accs[j % ways] = op(accs[j % ways], part)
    while len(accs) > 1:
        accs = [op(accs[j], accs[j + 1]) for j in range(0, len(accs) - 1, 2)] + accs[len(accs) & ~1:]
    return accs[0]


def _attn_kernel(nch_ref, qis_ref, qis_nx_ref, sm_ref, sm_nx_ref, kid_ref, q_ref, q_nx_ref, k_ref, vt_ref, tri_ref,
                 o_ref, sc_scr, acc_scr, s_scr, *, tq, tk, topk, causal, n_keys, lead):
    i = pl.program_id(1)
    nch = nch_ref[i]
    sub = tri_ref.shape[0]
    nq4 = KV_REP * tq
    rows = lambda ref, nx: jnp.concatenate([ref[lead:, :], nx[...]], axis=0) if lead else ref[...]
    qis, sm, q = rows(qis_ref, qis_nx_ref), rows(sm_ref, sm_nx_ref), rows(q_ref, q_nx_ref)
    qpos = lead + i * tq + lax.broadcasted_iota(I32, (1, tq), 1)
    if causal:
        lim = N_META + CHUNK * (jnp.right_shift(qpos - N_META, int(math.log2(CHUNK))) + 1)
        lim = jnp.minimum(lim, n_keys)
    else:
        lim = jnp.full((1, tq), n_keys, I32)

    qstack = jnp.concatenate([qis[:, hd * LANES:(hd + 1) * LANES] for hd in range(N_IDX_HEADS)], axis=0)
    smt = sm.T
    wlane = jnp.concatenate([smt[IDX_DIM + hd:IDX_DIM + hd + 1, :] for hd in range(N_IDX_HEADS)], axis=1)

    key_row = lax.broadcasted_iota(I32, (tk, tq), 0)

    def score_body(c2, st):
        return score_chunk(jnp.minimum(2 * c2 + 1, nch - 1), score_chunk(2 * c2, st))

    def score_chunk(c, st):
        smax, smin = st
        off = pl.multiple_of(c * tk, tk)
        sc = _dot_nt(kid_ref[pl.ds(off, tk), :], qstack)
        sc = jnp.maximum(sc, 0.0) * wlane
        s = sc[:, 0:tq]
        for hd in range(1, N_IDX_HEADS):
            s = s + sc[:, hd * tq:(hd + 1) * tq]
        ok = key_row < lim - off
        masked = jnp.where(ok, s, -jnp.inf)
        sc_scr[pl.ds(off, tk), :] = masked
        return (jnp.maximum(smax, _row_fold(masked, jnp.maximum)),
                jnp.minimum(smin, _row_fold(jnp.where(ok, s, jnp.inf), jnp.minimum)))

    smax, smin = lax.fori_loop(0, (nch + 1) // 2, score_body, (jnp.full((SUBLANES, tq), -jnp.inf, F32),
                                                    jnp.full((SUBLANES, tq), jnp.inf, F32)))
    smax = jnp.max(smax, axis=0, keepdims=True)
    smin = jnp.min(smin, axis=0, keepdims=True)

    def count_ge(cand):
        def hits(c, cnd):
            kb = sc_scr[pl.ds(pl.multiple_of(c * tk, tk), tk), :]
            return _row_fold(jnp.where(kb >= cnd, 1.0, 0.0), jnp.add)

        def body(c2, acc):
            second = 2 * c2 + 1
            cnd2 = jnp.where(second < nch, cand, jnp.inf)
            return acc + (hits(2 * c2, cand) + hits(jnp.minimum(second, nch - 1), cnd2))
        acc = lax.fori_loop(0, (nch + 1) // 2, body, jnp.zeros((SUBLANES, tq), F32))
        return jnp.sum(acc, axis=0, keepdims=True)

    take_all = lim <= topk
    kf = float(topk)
    lo0 = jnp.where(take_all, NEG_BIG, smin)
    hi0 = jnp.where(take_all, NEG_BIG, smax + jnp.abs(smax) * 2.0 ** -10 + F32_TINY * 2.0 ** 24)
    c_lo0 = lim.astype(F32)
    c_hi0 = jnp.where(take_all, kf, 0.0)
    closed0 = jnp.where(take_all, 1.0, 0.0)

    def search_cond(st):
        return (st[0] < MAX_PROBES) & (jnp.max(1.0 - st[5]) > 0.5)

    def search_body(st):
        return search_step(search_step(st))

    def search_step(st):
        it, lo, hi, c_lo, c_hi, closed = st
        mid = lo + 0.5 * (hi - lo)
        interp = hi - (hi - lo) * ((kf - c_hi) / jnp.maximum(c_lo - c_hi, 1.0))
        inside = lambda v: (v > lo) & (v < hi)
        cand = jnp.where((it & 1) == 0, interp, mid)
        cand = jnp.where((it == 0) & inside(0.0), 0.0, cand)
        cand = jnp.where((it == 1) & inside(F32_TINY), F32_TINY, cand)
        cand = jnp.where(inside(cand), cand, mid)
        cnt = count_ge(cand)
        is_open = closed < 0.5
        up = is_open & (cnt > kf)
        down = is_open & (cnt <= kf)
        lo = jnp.where(up, cand, lo)
        c_lo = jnp.where(up, cnt, c_lo)
        hi = jnp.where(down, cand, hi)
        c_hi = jnp.where(down, cnt, c_hi)
        done = (c_hi == kf) | jnp.logical_not(inside(lo + 0.5 * (hi - lo))) | ((lo == 0.0) & (hi == F32_TINY))
        return it + 1, lo, hi, c_lo, c_hi, jnp.where(done, 1.0, closed)

    _, thr_lo, thr_hi, _, c_hi, _ = lax.while_loop(search_cond, search_body,
                                                   (jnp.int32(0), lo0, hi0, c_lo0, c_hi0, closed0))
    n_tie = kf - c_hi

    acc_scr[...] = jnp.zeros(acc_scr.shape, F32)
    qgs = [jnp.concatenate([q[:, (g * KV_REP + r) * HEAD_DIM:(g * KV_REP + r + 1) * HEAD_DIM]
                            for r in range(KV_REP)], axis=0) for g in range(N_KV_HEADS)]

    ones_rows = jnp.ones((BF16_SUBLANES, tk), BF16)

    def logits(c, slot, seen):
        off = pl.multiple_of(c * tk, tk)
        kb = sc_scr[pl.ds(off, tk), :]
        above = kb >= thr_hi
        at_least = kb >= thr_lo
        eqf = jnp.where(at_least, 1.0, 0.0) - jnp.where(above, 1.0, 0.0)
        ranks = []
        for hf in range(tk // sub):
            e = eqf[hf * sub:(hf + 1) * sub]
            ranks.append(seen + _dot(tri_ref[...], e.astype(BF16)))
            seen = seen + jnp.sum(_row_fold(e, jnp.add), axis=0, keepdims=True)
        rank = jnp.concatenate(ranks, axis=0)
        maskf = jnp.where(above, 0.0, jnp.where(at_least, jnp.where(rank < n_tie, 0.0, NEG_BIG), NEG_BIG))
        mask4 = jnp.concatenate([maskf] * KV_REP, axis=1)
        mcs = []
        for g in range(N_KV_HEADS):
            kc = k_ref[pl.ds(off, tk), g * HEAD_DIM:(g + 1) * HEAD_DIM]
            s = _dot_nt(kc, qgs[g]) + mask4
            s_scr[slot, g] = s
            mcs.append(jnp.max(_row_fold(s, jnp.maximum), axis=0, keepdims=True))
        return seen, tuple(mcs)

    def update(c, slot, ms, mcs):
        off = pl.multiple_of(c * tk, tk)
        ms_n = []
        for g in range(N_KV_HEADS):
            vtc = vt_ref[g * HEAD_DIM:(g + 1) * HEAD_DIM, pl.ds(off, tk)]
            m_new = jnp.maximum(ms[g], mcs[g])
            alpha = jnp.exp2(ms[g] - m_new)
            p = jnp.exp2(s_scr[slot, g] - m_new).astype(BF16)
            acc_scr[g] = alpha * acc_scr[g] + _dot(jnp.concatenate([vtc, ones_rows], axis=0), p)
            ms_n.append(m_new)
        return tuple(ms_n)

    def flash_body(c, st):
        seen, ms, mcs = st
        slot = c & 1
        ms = update(c, slot, ms, mcs)
        seen, mcs = logits(c + 1, 1 - slot, seen)
        return seen, ms, mcs

    m0 = tuple(jnp.full((1, nq4), NEG_BIG, F32) for _ in range(N_KV_HEADS))
    seen0, mc0 = logits(0, 0, jnp.zeros((1, tq), F32))
    _, ms, mcs = lax.fori_loop(0, nch - 1, flash_body, (seen0, m0, mc0))
    update(nch - 1, (nch - 1) & 1, ms, mcs)

    for g in range(N_KV_HEADS):
        og = acc_scr[g, 0:HEAD_DIM] / acc_scr[g, HEAD_DIM:HEAD_DIM + 1]
        for r in range(KV_REP):
            hd = g * KV_REP + r
            o_ref[:, hd * HEAD_DIM:(hd + 1) * HEAD_DIM] = og[:, r * tq:(r + 1) * tq].T.astype(o_ref.dtype)


def _attention(qis, sm, kid, q, kb, vt, *, tq, n_q, lead, topk, causal, n_keys):
    tk, sub = ATTN_TK, TIE_BLOCK
    bsz, sq, _ = q.shape
    sk = kb.shape[1]
    assert n_q % tq == 0 and sk % tk == 0 and n_q + lead <= sq
    nq = n_q // tq
    if causal:
        ends = np.minimum(N_META + CHUNK * ((lead + np.arange(nq) * tq + tq - 1 - N_META) // CHUNK + 1), n_keys)
    else:
        ends = np.full((nq,), n_keys)
    nch = jnp.asarray(-(-ends // tk), I32)
    tri = jnp.asarray(np.tril(np.ones((sub, sub), np.float32), -1), BF16)
    lb = max(lead, BF16_SUBLANES)
    qrow = lambda n: pl.BlockSpec((None, tq, n), lambda b, i, s: (b, i, 0))
    qnext = lambda n: pl.BlockSpec((None, lb, n), lambda b, i, s: (b, jnp.minimum((i + 1) * (tq // lb), sq // lb - 1), 0))
    krow = lambda n: pl.BlockSpec((None, sk, n), lambda b, i, s: (b, 0, 0), pipeline_mode=pl.Buffered(1))
    kern = functools.partial(_attn_kernel, tq=tq, tk=tk, topk=topk, causal=causal, n_keys=n_keys, lead=lead)
    return pl.pallas_call(
        kern,
        grid_spec=pltpu.PrefetchScalarGridSpec(
            num_scalar_prefetch=1,
            grid=(bsz, nq),
            in_specs=[qrow(qis.shape[2]), qnext(qis.shape[2]), qrow(sm.shape[2]), qnext(sm.shape[2]),
                      krow(kid.shape[2]), qrow(q.shape[2]), qnext(q.shape[2]),
                      krow(kb.shape[2]),
                      pl.BlockSpec((vt.shape[0], sk), lambda b, i, s: (0, b), pipeline_mode=pl.Buffered(1)),
                      pl.BlockSpec((sub, sub), lambda b, i, s: (0, 0))],
            out_specs=qrow(q.shape[2]),
            scratch_shapes=[pltpu.VMEM((sk, tq), F32),
                            pltpu.VMEM((N_KV_HEADS, HEAD_DIM + BF16_SUBLANES, KV_REP * tq), F32),
                            pltpu.VMEM((2, N_KV_HEADS, tk, KV_REP * tq), F32)]),
        out_shape=jax.ShapeDtypeStruct((bsz, n_q, q.shape[2]), BF16),
        compiler_params=pltpu.CompilerParams(dimension_semantics=("arbitrary", "arbitrary"),
                                             vmem_limit_bytes=VMEM_LIMIT_BYTES),
        name="attn",
    )(nch, qis, qis, sm, sm, kid, q, q, kb, vt, tri)


def _mix_kernel(tb_ref, o_ref, oprev_ref, z_ref, halo_ref, sga_ref, sgb_ref, x_ref, wpool_ref, ps_ref, wao_ref,
                wout_ref, x1_ref, zc_scr, *, tm, lead, tiles):
    if lead:
        head = jnp.where(pl.program_id(0) % tiles == 0, jnp.zeros_like(oprev_ref[...]), oprev_ref[...])
        o = jnp.concatenate([head, o_ref[0:tm - lead, :]], axis=0)
    else:
        o = o_ref[...]
    hs = halo_ref.shape[0]
    zc_scr[0:hs, :] = halo_ref[...]
    zc_scr[hs:hs + tm, :] = z_ref[...]
    avail = tb_ref[pl.program_id(0)] + lax.broadcasted_iota(I32, (tm, 1), 0) + 1
    gw = z_ref.shape[1] // len(POOL_WINDOWS)
    ys = []
    for g, win in enumerate(POOL_WINDOWS):
        cols = slice(g * gw, (g + 1) * gw)
        zt = zc_scr[hs:hs + tm, cols]
        s = zt
        for j in range(1, win):
            s = s + zc_scr[hs - j:hs - j + tm, cols]
        cnt = jnp.minimum(avail, win).astype(F32)
        dlt = s / cnt - zt
        ys.append(_dot(dlt.astype(BF16), wpool_ref[g]))
    a = jnp.concatenate(ys, axis=1) * ps_ref[...]
    b = _dot(o, wao_ref[...])
    mrg = sga_ref[...] * a + sgb_ref[...] * b
    x1_ref[...] = x_ref[...] + _dot(mrg.astype(BF16), wout_ref[...])


def _mix(tbase, o, z, halo, sga, sgb, x, w, *, tm, lead):
    rows, d = x.shape
    bsz, so, _ = o.shape
    nt = rows // tm
    tiles = nt // bsz
    ot = so // tm
    hs = halo.shape[1]
    lb = max(lead, BF16_SUBLANES)
    row = lambda n: pl.BlockSpec((tm, n), lambda i, s: (i, 0))
    const = lambda a: pl.BlockSpec(a.shape, lambda i, s: (0,) * a.ndim)
    o_cur = pl.BlockSpec((None, tm, d), lambda i, s: (i // tiles, jnp.minimum(i % tiles, ot - 1), 0))
    o_prev = pl.BlockSpec((None, lb, d), lambda i, s: (i // tiles, jnp.maximum((i % tiles) * (tm // lb) - 1, 0), 0))
    return pl.pallas_call(
        functools.partial(_mix_kernel, tm=tm, lead=lead, tiles=tiles),
        grid_spec=pltpu.PrefetchScalarGridSpec(
            num_scalar_prefetch=1,
            grid=(nt,),
            in_specs=[o_cur, o_prev, row(d), pl.BlockSpec((None, hs, d), lambda i, s: (i, 0, 0)), row(d), row(d), row(d),
                      const(w["wpool"]), const(w["pscale"]), const(w["wao"]), const(w["wout"])],
            out_specs=row(d),
            scratch_shapes=[pltpu.VMEM((hs + tm, d), F32)]),
        out_shape=jax.ShapeDtypeStruct((rows, d), F32),
        compiler_params=pltpu.CompilerParams(dimension_semantics=("arbitrary",), vmem_limit_bytes=VMEM_LIMIT_BYTES),
        name="mix",
    )(tbase, o, o, z, halo, sga, sgb, x, w["wpool"], w["pscale"], w["wao"], w["wout"])


ROUTE_GROUP_LANE0 = N_EXPERTS


def _moe_kernel(x_ref, nxt_ref, gf_ref, gl_ref, wr_ref, br_ref, wg_ref, wu_ref, wd_ref, y_ref,
                x_scr, h_scr, route_scr, acc_scr, *, ec, lead):
    j = pl.program_id(1)
    tm = x_ref.shape[0]

    @pl.when(j == 0)
    def _():
        x_scr[...] = jnp.concatenate([x_ref[lead:, :], nxt_ref[...]], axis=0) if lead else x_ref[...]
        hf = _rms(x_scr[...], gf_ref[...])
        h_hi = hf.astype(BF16)
        h_lo = (hf - h_hi.astype(F32)).astype(BF16)
        h_scr[...] = h_hi
        hh = _dot(h_hi, wr_ref[...])
        logits = hh[:, :LANES] + hh[:, LANES:] + _dot(h_lo, wr_ref[:, :LANES]) + br_ref[...]
        lane = lax.broadcasted_iota(I32, (tm, LANES), 1)
        lanef = lane.astype(F32)
        ninf = -jnp.inf
        isg = (lane >= ROUTE_GROUP_LANE0) & (lane < ROUTE_GROUP_LANE0 + N_EXPERT_GROUPS)
        gl = jnp.where(isg, logits, ninf)
        gmax = jnp.max(gl, axis=1, keepdims=True)
        gsel = jnp.min(jnp.where(gl == gmax, lanef, 1e9), axis=1, keepdims=True)
        gprob = 1.0 / jnp.sum(jnp.exp(gl - gmax), axis=1, keepdims=True)
        e0 = (gsel - ROUTE_GROUP_LANE0) * EXPERTS_PER_GROUP
        el = jnp.where((lanef >= e0) & (lanef < e0 + EXPERTS_PER_GROUP), logits, ninf)
        v1 = jnp.max(el, axis=1, keepdims=True)
        i1 = jnp.min(jnp.where(el == v1, lanef, 1e9), axis=1, keepdims=True)
        el2 = jnp.where(lanef == i1, ninf, el)
        v2 = jnp.max(el2, axis=1, keepdims=True)
        i2 = jnp.min(jnp.where(el2 == v2, lanef, 1e9), axis=1, keepdims=True)
        e = jnp.exp(v2 - v1)
        w1 = (1.0 / (1.0 + e)) * gprob
        w2 = (e / (1.0 + e)) * gprob
        for n, val in enumerate((i1, i2, w1, w2)):
            route_scr[n] = jnp.broadcast_to(val, (tm, LANES))
        acc_scr[...] = jnp.zeros(acc_scr.shape, F32)

    h = h_scr[...]
    i1, i2, w1, w2 = route_scr[0], route_scr[1], route_scr[2], route_scr[3]
    gate = _dot(h, wg_ref[...])
    up = _dot(h, wu_ref[...])
    act = gate * jax.nn.sigmoid(gate) * up
    parts = []
    for q in range(ec):
        ef = jnp.full((tm, LANES), j * ec + q, I32).astype(F32)
        comb = jnp.where(i1 == ef, w1, 0.0) + jnp.where(i2 == ef, w2, 0.0)
        parts.append(act[:, q * D_EXPERT:(q + 1) * D_EXPERT] * comb)
    acc_scr[...] += _dot(jnp.concatenate(parts, axis=1).astype(BF16), wd_ref[...])

    @pl.when(j == pl.num_programs(1) - 1)
    def _():
        y_ref[...] = _rms(x_scr[...] + acc_scr[...], gl_ref[...])


def _moe(x, w, *, tm, n_out, lead, ec=EXPERTS_PER_GROUP):
    bsz, s, d = x.shape
    assert n_out % tm == 0 and n_out + lead <= s
    nt = n_out // tm
    ne = N_EXPERTS // ec
    cw = ec * D_EXPERT
    lb = max(lead, SUBLANES)
    const = lambda a: pl.BlockSpec(a.shape, lambda i, j: (0,) * a.ndim)
    tile = pl.BlockSpec((None, tm, d), lambda i, j: (i // nt, i % nt, 0))
    after = pl.BlockSpec((None, lb, d), lambda i, j: (i // nt, jnp.minimum((i % nt + 1) * (tm // lb), s // lb - 1), 0))
    return pl.pallas_call(
        functools.partial(_moe_kernel, ec=ec, lead=lead),
        grid=(bsz * nt, ne),
        in_specs=[tile, after, const(w["gffn"]), const(w["gfinal"]), const(w["wr"]), const(w["br"]),
                  pl.BlockSpec((d, cw), lambda i, j: (0, j)), pl.BlockSpec((d, cw), lambda i, j: (0, j)),
                  pl.BlockSpec((cw, d), lambda i, j: (j, 0))],
        out_specs=tile,
        out_shape=jax.ShapeDtypeStruct((bsz, n_out, d), F32),
        scratch_shapes=[pltpu.VMEM((tm, d), F32), pltpu.VMEM((tm, d), BF16), pltpu.VMEM((4, tm, LANES), F32),
                        pltpu.VMEM((tm, d), F32)],
        compiler_params=pltpu.CompilerParams(dimension_semantics=("arbitrary", "arbitrary"),
                                             vmem_limit_bytes=VMEM_LIMIT_BYTES),
        name="moe",
    )(x, x, w["gffn"], w["gfinal"], w["wr"], w["br"], w["wg"], w["wu"], w["wd"])


def _rope_table(pos):
    posf = pos.astype(F32)[None, :]

    def cs(half):
        inv = jnp.exp(-math.log(ROPE_THETA) * jnp.arange(half, dtype=F32) * (1.0 / half))
        ang = posf * inv[:, None]
        return jnp.cos(ang), jnp.sin(ang)

    cq, sq = cs(QK_HALF)
    ci, si = cs(IX_HALF)
    used = 2 * QK_HALF + 2 * IX_HALF
    return jnp.concatenate([cq, sq, ci, si, jnp.zeros((LANES - used, pos.shape[0]), F32)], axis=0).T


def _prep_weights(l, norm_mix_g, norm_ffn_g, norm_final_g, w_in, w_pool, pool_scale, w_attn_out, w_out,
                  w_router_group, b_router_group, w_router_expert, b_router_expert,
                  w_expert_gate, w_expert_up, w_expert_down):
    d = w_in.shape[1]
    pw = w_pool.shape[1] * w_pool.shape[2]
    qw = N_HEADS * HEAD_DIM
    kvw = N_KV_HEADS * HEAD_DIM
    widths = (pw, qw, kvw, kvw, N_IDX_HEADS * IDX_DIM, IDX_DIM, N_IDX_HEADS, d, d)
    offs = np.concatenate([[0], np.cumsum(widths)])
    wi = w_in[l]
    piece = lambda a, b: wi[:, offs[a]:offs[b]].astype(BF16)
    wsm = jnp.concatenate([wi[:, offs[5]:offs[7]], jnp.zeros((d, LANES - IDX_DIM - N_IDX_HEADS), F32)], axis=1)
    wr = jnp.concatenate([w_router_expert[l], w_router_group[l],
                          jnp.zeros((d, LANES - N_EXPERTS - N_EXPERT_GROUPS), F32)], axis=1)
    wr_hi = wr.astype(BF16)
    br = jnp.concatenate([b_router_expert[l], b_router_group[l],
                          jnp.zeros((LANES - N_EXPERTS - N_EXPERT_GROUPS,), F32)])[None, :]
    flat_e = lambda a: a.transpose(1, 0, 2).reshape(d, N_EXPERTS * D_EXPERT).astype(BF16)
    return dict(
        gmix=norm_mix_g[l][None, :], gffn=norm_ffn_g[l][None, :], gfinal=norm_final_g[None, :],
        wz=piece(0, 1), wq=piece(1, 2), wkv=piece(2, 4), wqi=piece(4, 5), wsm=wsm.astype(BF16), wgab=piece(7, 9),
        wpool=w_pool[l].astype(BF16), pscale=pool_scale[l][None, :],
        wao=w_attn_out[l].astype(BF16), wout=w_out[l].astype(BF16),
        wr=jnp.concatenate([wr_hi, (wr - wr_hi.astype(F32)).astype(BF16)], axis=1), br=br,
        wg=flat_e(w_expert_gate[l]), wu=flat_e(w_expert_up[l]),
        wd=w_expert_down[l].reshape(N_EXPERTS * D_EXPERT, d).astype(BF16),
    )


def _dup_lanes(ki):
    return jnp.concatenate([ki, ki], axis=-1)


def kernel(x_prompt, x_sample, cache_k, cache_v, cache_idx_k, state_pool, meta_tokens, norm_mix_g, norm_ffn_g,
           norm_final_g, w_in, w_pool, pool_scale, w_attn_out, w_out, w_router_group, b_router_group,
           w_router_expert, b_router_expert, w_expert_gate, w_expert_up, w_expert_down):
    bsz, seq, d = x_prompt.shape
    dbsz, dec, _ = x_sample.shape
    assert w_in.shape[0] == 1, "single-layer step"
    l = 0
    past = cache_k.shape[2] - N_META
    n = N_META + seq
    topk_p = min(IDX_TOPK, seq // 4)
    topk_s = min(IDX_TOPK, (past + dec) // 4)
    kvw = N_KV_HEADS * HEAD_DIM

    tk = ATTN_TK
    tm = tk
    assert seq % tm == 0 and N_META <= tm
    sp = seq + tm
    nt = sp // tm
    tm_moe = _pick_tile(seq, 1100)
    w = _prep_weights(l, norm_mix_g, norm_ffn_g, norm_final_g, w_in, w_pool, pool_scale, w_attn_out, w_out,
                      w_router_group, b_router_group, w_router_expert, b_router_expert,
                      w_expert_gate, w_expert_up, w_expert_down)

    tab_p = _rope_table(jnp.arange(sp))
    xp, z, q, k_c, v_c, kb, vt, qis, sm, kid, sga, sgb = _inproj(
        x_prompt, meta_tokens.astype(F32), w["gmix"], tab_p, w, tm=tm, n_out=n, lead=N_META)
    b3 = lambda a: a.reshape(bsz, sp, a.shape[-1])
    o = _attention(b3(qis), b3(sm), b3(kid), b3(q), b3(kb), vt, tq=ATTN_TQ_PROMPT, n_q=seq, lead=N_META,
                   topk=topk_p, causal=True, n_keys=n)
    hs = POOL_STATE + 1
    z4 = z.reshape(bsz, nt, tm, d)
    halo = jnp.concatenate([jnp.zeros((bsz, 1, hs, d), F32), z4[:, :-1, tm - hs:, :]], axis=1).reshape(bsz * nt, hs, d)
    tbase = jnp.asarray(np.tile(np.arange(nt) * tm, bsz), I32)
    x1 = _mix(tbase, o, z, halo, sga, sgb, xp, w, tm=tm, lead=N_META)
    y_prompt = _moe(b3(x1), w, tm=tm_moe, n_out=seq, lead=N_META)
    k_prompt = k_c[None]
    v_prompt = v_c[None]
    idx_k_prompt = b3(sm)[:, :n, :IDX_DIM][None]
    pool_prompt = b3(z)[:, n - POOL_STATE:n][None]

    rows_s = dbsz * dec
    tab_s = jnp.tile(_rope_table(N_META + past + jnp.arange(dec)), (dbsz, 1))
    xs = x_sample.reshape(rows_s, d)
    _, zs, qs, ks_c, vs_c, kbs, _, qiss, sms, kids, sgas, sgbs = _inproj(
        xs[None], jnp.zeros((SUBLANES, d), F32), w["gmix"], tab_s, w, tm=rows_s, n_out=rows_s, lead=0)
    vs = vs_c.reshape(rows_s, kvw)
    nk = N_META + past + dec
    sk = _round_up(nk, tk)
    sq_s = _round_up(dec, ATTN_TQ)
    s3 = lambda a: a.reshape(dbsz, dec, a.shape[-1])
    padr = lambda a, rows: jnp.concatenate([a, jnp.zeros((dbsz, rows - a.shape[1], a.shape[-1]), a.dtype)], axis=1)
    k_all = padr(jnp.concatenate([cache_k[l].reshape(dbsz, N_META + past, kvw).astype(BF16), s3(kbs)], axis=1), sk)
    v_all = padr(jnp.concatenate([cache_v[l].reshape(dbsz, N_META + past, kvw).astype(BF16),
                                  s3(vs).astype(BF16)], axis=1), sk)
    vt_all = v_all.transpose(2, 0, 1).reshape(kvw, dbsz * sk)
    ki_all = padr(jnp.concatenate([_dup_lanes(cache_idx_k[l]).astype(BF16), s3(kids)], axis=1), sk)
    os_ = _attention(padr(s3(qiss), sq_s), padr(s3(sms), sq_s), ki_all, padr(s3(qs), sq_s), k_all, vt_all,
                     tq=ATTN_TQ, n_q=sq_s, lead=0, topk=topk_s, causal=False, n_keys=nk)[:, :dec]
    zcat = jnp.concatenate([state_pool[l].astype(F32), s3(zs)], axis=1)
    halo_s = jnp.concatenate([jnp.zeros((dbsz, 1, d), F32), zcat[:, :POOL_STATE]], axis=1)
    tbase_s = jnp.full((dbsz,), POOL_STATE, I32)
    x1s = _mix(tbase_s, os_, zs, halo_s, sgas, sgbs, xs, w, tm=dec, lead=0)
    y_sample = _moe(x1s[None], w, tm=rows_s, n_out=rows_s, lead=0).reshape(dbsz, dec, d)
    k_sample = ks_c.reshape(1, dbsz, dec, N_KV_HEADS, HEAD_DIM)
    v_sample = vs_c.reshape(1, dbsz, dec, N_KV_HEADS, HEAD_DIM)
    idx_k_sample = s3(sms)[:, :, :IDX_DIM][None]
    pool_sample = zcat[:, zcat.shape[1] - POOL_STATE:][None]

    return (y_prompt, y_sample, k_prompt, v_prompt, idx_k_prompt, pool_prompt,
            k_sample, v_sample, idx_k_sample, pool_sample)
```

```python
import functools
import math

import numpy as np
import jax
import jax.numpy as jnp
from jax import lax
from jax.experimental import pallas as pl
from jax.experimental.pallas import tpu as pltpu

CHUNK = 64
N_META = 16
EPS = 1e-6
POOL_WINDOWS = (2, 4, 8, 16)
POOL_STATE = max(POOL_WINDOWS) - 1
N_HEADS = 8
N_KV_HEADS = 2
HEAD_DIM = 128
KV_REP = N_HEADS // N_KV_HEADS
N_IDX_HEADS = 4
IDX_DIM = 64
IDX_TOPK = 256
IDX_W_SCALE = (N_IDX_HEADS * IDX_DIM) ** -0.5
ATTN_SCALE = HEAD_DIM ** -0.5
Q_PRESCALE = ATTN_SCALE * math.log2(math.e)
ROPE_THETA = 500000.0
ROT_FRACTION = 4
N_EXPERT_GROUPS = 4
EXPERTS_PER_GROUP = 8
N_EXPERTS = N_EXPERT_GROUPS * EXPERTS_PER_GROUP
D_EXPERT = 128

LANES = 128
SUBLANES = 8
BF16_SUBLANES = 16
VMEM_LIMIT_BYTES = 56 * 1024 * 1024

QK_HALF = HEAD_DIM // ROT_FRACTION // 2
IX_HALF = IDX_DIM // ROT_FRACTION // 2
TAB_COS_QK = 0
TAB_SIN_QK = QK_HALF
TAB_COS_IX = 2 * QK_HALF
TAB_SIN_IX = 2 * QK_HALF + IX_HALF

ATTN_TQ = LANES
ATTN_TQ_PROMPT = 2 * LANES
ATTN_TK = 512
TIE_BLOCK = 256
F32_TINY = float(np.finfo(np.float32).tiny)
MAX_PROBES = 64
NEG_BIG = -1e30

F32 = jnp.float32
BF16 = jnp.bfloat16
I32 = jnp.int32


def _round_up(x, m):
    return -(-x // m) * m


def _pick_tile(rows, cap):
    return max(t for t in range(BF16_SUBLANES, cap + 1, BF16_SUBLANES) if rows % t == 0)


def _rms(x, g):
    return x * lax.rsqrt(jnp.mean(x * x, axis=-1, keepdims=True) + EPS) * g


def _dot(a, b):
    return jnp.dot(a, b, preferred_element_type=F32)


def _dot_nt(a, b):
    return lax.dot_general(a, b, (((1,), (1,)), ((), ())), preferred_element_type=F32)


def _rope(y, c, a, b, shift_a, shift_b):
    outs = []
    for j in range(y.shape[1] // LANES):
        ys = y[:, j * LANES:(j + 1) * LANES]
        outs.append(ys * c + pltpu.roll(ys, shift_a, 1) * a + pltpu.roll(ys, shift_b, 1) * b)
    return outs[0] if len(outs) == 1 else jnp.concatenate(outs, axis=1)


def _inproj_kernel(x_ref, prev_ref, meta_ref, g_ref, tab_ref, wz_ref, wq_ref, wkv_ref, wqi_ref, wsm_ref, wgab_ref,
                   xs_ref, z_ref, q_ref, k4_ref, v4_ref, kb_ref, vt_ref, qis_ref, sm_ref, kid_ref, sga_ref, sgb_ref,
                   *, lead, frame_tiles):
    tm, d = x_ref.shape
    if lead:
        ti = pl.program_id(0) % (frame_tiles + 1)
        head = jnp.where(ti == 0, meta_ref[...], prev_ref[...])
        body = jnp.where(ti < frame_tiles, x_ref[0:tm - lead, :], 0.0)
        x = jnp.concatenate([head, body], axis=0)
    else:
        x = x_ref[...]
    xs_ref[...] = x
    h = _rms(x, g_ref[...]).astype(BF16)
    lane = lax.broadcasted_iota(I32, (tm, LANES), 1)
    t = tab_ref[...]

    def rl(s):
        return pltpu.roll(t, s % LANES, 1) if s % LANES else t

    c_qk = jnp.where(lane < QK_HALF, rl(-TAB_COS_QK), jnp.where(lane < 2 * QK_HALF, rl(QK_HALF - TAB_COS_QK), 1.0))
    a_qk = jnp.where(lane < QK_HALF, -rl(-TAB_SIN_QK), 0.0)
    b_qk = jnp.where((lane >= QK_HALF) & (lane < 2 * QK_HALF), rl(QK_HALF - TAB_SIN_QK), 0.0)
    l64 = lane & (IDX_DIM - 1)
    hi = lane >= IDX_DIM
    cos_lo = jnp.where(hi, rl(IDX_DIM - TAB_COS_IX), rl(-TAB_COS_IX))
    cos_hi = jnp.where(hi, rl(IDX_DIM + IX_HALF - TAB_COS_IX), rl(IX_HALF - TAB_COS_IX))
    sin_lo = jnp.where(hi, rl(IDX_DIM - TAB_SIN_IX), rl(-TAB_SIN_IX))
    sin_hi = jnp.where(hi, rl(IDX_DIM + IX_HALF - TAB_SIN_IX), rl(IX_HALF - TAB_SIN_IX))
    c_ix = jnp.where(l64 < IX_HALF, cos_lo, jnp.where(l64 < 2 * IX_HALF, cos_hi, 1.0))
    a_ix = jnp.where(l64 < IX_HALF, -sin_lo, 0.0)
    b_ix = jnp.where((l64 >= IX_HALF) & (l64 < 2 * IX_HALF), sin_hi, 0.0)

    z_ref[...] = _dot(h, wz_ref[...])

    q = _rope(_dot(h, wq_ref[...]), c_qk, a_qk, b_qk, LANES - QK_HALF, QK_HALF)
    q_ref[...] = (q * Q_PRESCALE).astype(BF16)

    kv = _dot(h, wkv_ref[...])
    kvw = kv.shape[1] // 2
    k = _rope(kv[:, :kvw], c_qk, a_qk, b_qk, LANES - QK_HALF, QK_HALF)
    v = kv[:, kvw:]
    for g in range(N_KV_HEADS):
        k4_ref[:, g, :] = k[:, g * HEAD_DIM:(g + 1) * HEAD_DIM]
        v4_ref[:, g, :] = v[:, g * HEAD_DIM:(g + 1) * HEAD_DIM]
    kb_ref[...] = k.astype(BF16)
    vt_ref[...] = v.T.astype(BF16)

    qi = _rope(_dot(h, wqi_ref[...]), c_ix, a_ix, b_ix, LANES - IX_HALF, IX_HALF)
    pieces = []
    for hd in range(N_IDX_HEADS):
        col = qi[:, (hd // 2) * LANES:(hd // 2 + 1) * LANES]
        keep = hi if hd % 2 else jnp.logical_not(hi)
        pieces.append(jnp.where(keep, col, 0.0))
    qis_ref[...] = jnp.concatenate(pieces, axis=1).astype(BF16)

    c_sm = jnp.where(hi, IDX_W_SCALE, c_ix)
    a_sm = jnp.where(hi, 0.0, a_ix)
    b_sm = jnp.where(hi, 0.0, b_ix)
    sm = _rope(_dot(h, wsm_ref[...]), c_sm, a_sm, b_sm, LANES - IX_HALF, IX_HALF)
    sm_ref[...] = sm
    kid_ref[...] = jnp.where(hi, pltpu.roll(sm, IDX_DIM, 1), sm).astype(BF16)

    gab = _dot(h, wgab_ref[...])
    sga_ref[...] = jax.nn.sigmoid(gab[:, :d])
    sgb_ref[...] = jax.nn.sigmoid(gab[:, d:])


def _inproj(x, meta, g, tab, w, *, tm, n_out, lead):
    bsz, s, d = x.shape
    assert s % tm == 0
    ft = s // tm
    nt = ft + 1 if lead else ft
    rows = bsz * nt * tm
    qw = w["wq"].shape[1]
    kvw = w["wkv"].shape[1] // 2
    lb = max(lead, SUBLANES)
    row_spec = lambda n: pl.BlockSpec((tm, n), lambda i: (i, 0))
    const = lambda a: pl.BlockSpec(a.shape, lambda i: (0,) * a.ndim)
    cache_spec = pl.BlockSpec((None, tm, N_KV_HEADS, HEAD_DIM), lambda i: (i // nt, i % nt, 0, 0))
    cache_shape = jax.ShapeDtypeStruct((bsz, n_out, N_KV_HEADS, HEAD_DIM), F32)
    outs = [
        (d, F32),
        (d, F32),
        (qw, BF16),
        "cache", "cache",
        (kvw, BF16),
        "vt",
        (N_IDX_HEADS * LANES, BF16),
        (LANES, F32),
        (LANES, BF16),
        (d, F32), (d, F32),
    ]
    spec_of = lambda o: (cache_spec if o == "cache" else pl.BlockSpec((kvw, tm), lambda i: (0, i)) if o == "vt"
                         else row_spec(o[0]))
    shape_of = lambda o: (cache_shape if o == "cache" else jax.ShapeDtypeStruct((kvw, rows), BF16) if o == "vt"
                          else jax.ShapeDtypeStruct((rows, o[0]), o[1]))
    return pl.pallas_call(
        functools.partial(_inproj_kernel, lead=lead, frame_tiles=ft),
        grid=(bsz * nt,),
        in_specs=[pl.BlockSpec((None, tm, d), lambda i: (i // nt, jnp.minimum(i % nt, ft - 1), 0)),
                  pl.BlockSpec((None, lb, d), lambda i: (i // nt, jnp.maximum((i % nt) * (tm // lb) - 1, 0), 0)),
                  const(meta), const(g),
                  pl.BlockSpec((tm, LANES), lambda i: (i % nt, 0)),
                  const(w["wz"]), const(w["wq"]), const(w["wkv"]), const(w["wqi"]), const(w["wsm"]), const(w["wgab"])],
        out_specs=[spec_of(o) for o in outs],
        out_shape=[shape_of(o) for o in outs],
        compiler_params=pltpu.CompilerParams(dimension_semantics=("arbitrary",), vmem_limit_bytes=VMEM_LIMIT_BYTES),
        name="inproj",
    )(x, x, meta, g, tab, w["wz"], w["wq"], w["wkv"], w["wqi"], w["wsm"], w["wgab"])


def _row_fold(x, op, group=SUBLANES):
    ways = 4
    accs = []
    for j in range(x.shape[0] // group):
        part = x[j * group:(j + 1) * group]
        if j < ways:
            accs.append(part)
        else:
            accs[j % ways] = op(accs[j % ways], part)
    while len(accs) > 1:
        accs = [op(accs[j], accs[j + 1]) for j in range(0, len(accs) - 1, 2)] + accs[len(accs) & ~1:]
    return accs[0]


def _attn_kernel(nch_ref, qis_ref, qis_nx_ref, sm_ref, sm_nx_ref, kid_ref, q_ref, q_nx_ref, k_ref, vt_ref, tri_ref,
                 o_ref, sc_scr, acc_scr, s_scr, *, tq, tk, topk, causal, n_keys, lead):
    i = pl.program_id(1)
    nch = nch_ref[i]
    sub = tri_ref.shape[0]
    nq4 = KV_REP * tq
    rows = lambda ref, nx: jnp.concatenate([ref[lead:, :], nx[...]], axis=0) if lead else ref[...]
    qis, sm, q = rows(qis_ref, qis_nx_ref), rows(sm_ref, sm_nx_ref), rows(q_ref, q_nx_ref)
    qpos = lead + i * tq + lax.broadcasted_iota(I32, (1, tq), 1)
    if causal:
        lim = N_META + CHUNK * (jnp.right_shift(qpos - N_META, int(math.log2(CHUNK))) + 1)
        lim = jnp.minimum(lim, n_keys)
    else:
        lim = jnp.full((1, tq), n_keys, I32)

    qstack = jnp.concatenate([qis[:, hd * LANES:(hd + 1) * LANES] for hd in range(N_IDX_HEADS)], axis=0)
    smt = sm.T
    wlane = jnp.concatenate([smt[IDX_DIM + hd:IDX_DIM + hd + 1, :] for hd in range(N_IDX_HEADS)], axis=1)

    key_row = lax.broadcasted_iota(I32, (tk, tq), 0)

    def score_body(c2, st):
        return score_chunk(jnp.minimum(2 * c2 + 1, nch - 1), score_chunk(2 * c2, st))

    def score_chunk(c, st):
        smax, smin = st
        off = pl.multiple_of(c * tk, tk)
        sc = _dot_nt(kid_ref[pl.ds(off, tk), :], qstack)
        sc = jnp.maximum(sc, 0.0) * wlane
        s = sc[:, 0:tq]
        for hd in range(1, N_IDX_HEADS):
            s = s + sc[:, hd * tq:(hd + 1) * tq]
        ok = key_row < lim - off
        masked = jnp.where(ok, s, -jnp.inf)
        sc_scr[pl.ds(off, tk), :] = masked
        return (jnp.maximum(smax, _row_fold(masked, jnp.maximum)),
                jnp.minimum(smin, _row_fold(jnp.where(ok, s, jnp.inf), jnp.minimum)))

    smax, smin = lax.fori_loop(0, (nch + 1) // 2, score_body, (jnp.full((SUBLANES, tq), -jnp.inf, F32),
                                                    jnp.full((SUBLANES, tq), jnp.inf, F32)))
    smax = jnp.max(smax, axis=0, keepdims=True)
    smin = jnp.min(smin, axis=0, keepdims=True)

    def count_ge(cand):
        def hits(c, cnd):
            kb = sc_scr[pl.ds(pl.multiple_of(c * tk, tk), tk), :]
            return _row_fold(jnp.where(kb >= cnd, 1.0, 0.0), jnp.add)

        def body(c2, acc):
            second = 2 * c2 + 1
            cnd2 = jnp.where(second < nch, cand, jnp.inf)
            return acc + (hits(2 * c2, cand) + hits(jnp.minimum(second, nch - 1), cnd2))
        acc = lax.fori_loop(0, (nch + 1) // 2, body, jnp.zeros((SUBLANES, tq), F32))
        return jnp.sum(acc, axis=0, keepdims=True)

    take_all = lim <= topk
    kf = float(topk)
    lo0 = jnp.where(take_all, NEG_BIG, smin)
    hi0 = jnp.where(take_all, NEG_BIG, smax + jnp.abs(smax) * 2.0 ** -10 + F32_TINY * 2.0 ** 24)
    c_lo0 = lim.astype(F32)
    c_hi0 = jnp.where(take_all, kf, 0.0)
    closed0 = jnp.where(take_all, 1.0, 0.0)

    def search_cond(st):
        return (st[0] < MAX_PROBES) & (jnp.max(1.0 - st[5]) > 0.5)

    def search_body(st):
        return search_step(search_step(st))

    def search_step(st):
        it, lo, hi, c_lo, c_hi, closed = st
        mid = lo + 0.5 * (hi - lo)
        interp = hi - (hi - lo) * ((kf - c_hi) / jnp.maximum(c_lo - c_hi, 1.0))
        inside = lambda v: (v > lo) & (v < hi)
        cand = jnp.where((it & 1) == 0, interp, mid)
        cand = jnp.where((it == 0) & inside(0.0), 0.0, cand)
        cand = jnp.where((it == 1) & inside(F32_TINY), F32_TINY, cand)
        cand = jnp.where(inside(cand), cand, mid)
        cnt = count_ge(cand)
        is_open = closed < 0.5
        up = is_open & (cnt > kf)
        down = is_open & (cnt <= kf)
        lo = jnp.where(up, cand, lo)
        c_lo = jnp.where(up, cnt, c_lo)
        hi = jnp.where(down, cand, hi)
        c_hi = jnp.where(down, cnt, c_hi)
        done = (c_hi == kf) | jnp.logical_not(inside(lo + 0.5 * (hi - lo))) | ((lo == 0.0) & (hi == F32_TINY))
        return it + 1, lo, hi, c_lo, c_hi, jnp.where(done, 1.0, closed)

    _, thr_lo, thr_hi, _, c_hi, _ = lax.while_loop(search_cond, search_body,
                                                   (jnp.int32(0), lo0, hi0, c_lo0, c_hi0, closed0))
    n_tie = kf - c_hi

    acc_scr[...] = jnp.zeros(acc_scr.shape, F32)
    qgs = [jnp.concatenate([q[:, (g * KV_REP + r) * HEAD_DIM:(g * KV_REP + r + 1) * HEAD_DIM]
                            for r in range(KV_REP)], axis=0) for g in range(N_KV_HEADS)]

    ones_rows = jnp.ones((BF16_SUBLANES, tk), BF16)

    def logits(c, slot, seen):
        off = pl.multiple_of(c * tk, tk)
        kb = sc_scr[pl.ds(off, tk), :]
        above = kb >= thr_hi
        at_least = kb >= thr_lo
        eqf = jnp.where(at_least, 1.0, 0.0) - jnp.where(above, 1.0, 0.0)
        ranks = []
        for hf in range(tk // sub):
            e = eqf[hf * sub:(hf + 1) * sub]
            ranks.append(seen + _dot(tri_ref[...], e.astype(BF16)))
            seen = seen + jnp.sum(_row_fold(e, jnp.add), axis=0, keepdims=True)
        rank = jnp.concatenate(ranks, axis=0)
        maskf = jnp.where(above, 0.0, jnp.where(at_least, jnp.where(rank < n_tie, 0.0, NEG_BIG), NEG_BIG))
        mask4 = jnp.concatenate([maskf] * KV_REP, axis=1)
        mcs = []
        for g in range(N_KV_HEADS):
            kc = k_ref[pl.ds(off, tk), g * HEAD_DIM:(g + 1) * HEAD_DIM]
            s = _dot_nt(kc, qgs[g]) + mask4
            s_scr[slot, g] = s
            mcs.append(jnp.max(_row_fold(s, jnp.maximum), axis=0, keepdims=True))
        return seen, tuple(mcs)

    def update(c, slot, ms, mcs):
        off = pl.multiple_of(c * tk, tk)
        ms_n = []
        for g in range(N_KV_HEADS):
            vtc = vt_ref[g * HEAD_DIM:(g + 1) * HEAD_DIM, pl.ds(off, tk)]
            m_new = jnp.maximum(ms[g], mcs[g])
            alpha = jnp.exp2(ms[g] - m_new)
            p = jnp.exp2(s_scr[slot, g] - m_new).astype(BF16)
            acc_scr[g] = alpha * acc_scr[g] + _dot(jnp.concatenate([vtc, ones_rows], axis=0), p)
            ms_n.append(m_new)
        return tuple(ms_n)

    def flash_body(c, st):
        seen, ms, mcs = st
        slot = c & 1
        ms = update(c, slot, ms, mcs)
        seen, mcs = logits(c + 1, 1 - slot, seen)
        return seen, ms, mcs

    m0 = tuple(jnp.full((1, nq4), NEG_BIG, F32) for _ in range(N_KV_HEADS))
    seen0, mc0 = logits(0, 0, jnp.zeros((1, tq), F32))
    _, ms, mcs = lax.fori_loop(0, nch - 1, flash_body, (seen0, m0, mc0))
    update(nch - 1, (nch - 1) & 1, ms, mcs)

    for g in range(N_KV_HEADS):
        og = acc_scr[g, 0:HEAD_DIM] / acc_scr[g, HEAD_DIM:HEAD_DIM + 1]
        for r in range(KV_REP):
            hd = g * KV_REP + r
            o_ref[:, hd * HEAD_DIM:(hd + 1) * HEAD_DIM] = og[:, r * tq:(r + 1) * tq].T.astype(o_ref.dtype)


def _attention(qis, sm, kid, q, kb, vt, *, tq, n_q, lead, topk, causal, n_keys):
    tk, sub = ATTN_TK, TIE_BLOCK
    bsz, sq, _ = q.shape
    sk = kb.shape[1]
    assert n_q % tq == 0 and sk % tk == 0 and n_q + lead <= sq
    nq = n_q // tq
    if causal:
        ends = np.minimum(N_META + CHUNK * ((lead + np.arange(nq) * tq + tq - 1 - N_META) // CHUNK + 1), n_keys)
    else:
        ends = np.full((nq,), n_keys)
    nch = jnp.asarray(-(-ends // tk), I32)
    tri = jnp.asarray(np.tril(np.ones((sub, sub), np.float32), -1), BF16)
    lb = max(lead, BF16_SUBLANES)
    qrow = lambda n: pl.BlockSpec((None, tq, n), lambda b, i, s: (b, i, 0))
    qnext = lambda n: pl.BlockSpec((None, lb, n), lambda b, i, s: (b, jnp.minimum((i + 1) * (tq // lb), sq // lb - 1), 0))
    krow = lambda n: pl.BlockSpec((None, sk, n), lambda b, i, s: (b, 0, 0), pipeline_mode=pl.Buffered(1))
    kern = functools.partial(_attn_kernel, tq=tq, tk=tk, topk=topk, causal=causal, n_keys=n_keys, lead=lead)
    return pl.pallas_call(
        kern,
        grid_spec=pltpu.PrefetchScalarGridSpec(
            num_scalar_prefetch=1,
            grid=(bsz, nq),
            in_specs=[qrow(qis.shape[2]), qnext(qis.shape[2]), qrow(sm.shape[2]), qnext(sm.shape[2]),
                      krow(kid.shape[2]), qrow(q.shape[2]), qnext(q.shape[2]),
                      krow(kb.shape[2]),
                      pl.BlockSpec((vt.shape[0], sk), lambda b, i, s: (0, b), pipeline_mode=pl.Buffered(1)),
                      pl.BlockSpec((sub, sub), lambda b, i, s: (0, 0))],
            out_specs=qrow(q.shape[2]),
            scratch_shapes=[pltpu.VMEM((sk, tq), F32),
                            pltpu.VMEM((N_KV_HEADS, HEAD_DIM + BF16_SUBLANES, KV_REP * tq), F32),
                            pltpu.VMEM((2, N_KV_HEADS, tk, KV_REP * tq), F32)]),
        out_shape=jax.ShapeDtypeStruct((bsz, n_q, q.shape[2]), BF16),
        compiler_params=pltpu.CompilerParams(dimension_semantics=("arbitrary", "arbitrary"),
                                             vmem_limit_bytes=VMEM_LIMIT_BYTES),
        name="attn",
    )(nch, qis, qis, sm, sm, kid, q, q, kb, vt, tri)


def _keys_kernel(ck_ref, ckt_ref, cv_ref, cvt_ref, kn_ref, vn_ref, k_ref, vt_ref, *, full_tiles):
    t = pl.program_id(1)
    tk = k_ref.shape[0]
    flat = lambda ref: jnp.concatenate([ref[:, g, :] for g in range(N_KV_HEADS)], axis=1)

    def emit(k, v):
        k_ref[...] = k.astype(BF16)
        vt_ref[...] = v.T.astype(BF16)

    @pl.when(t < full_tiles)
    def _():
        emit(flat(ck_ref), flat(cv_ref))

    @pl.when(t == full_tiles)
    def _():
        def tile(tail_ref, new_ref):
            tail, new = flat(tail_ref), new_ref[...].astype(F32)
            rest = jnp.zeros((tk - tail.shape[0] - new.shape[0], tail.shape[1]), F32)
            return jnp.concatenate([tail, new, rest], axis=0)
        emit(tile(ckt_ref, kn_ref), tile(cvt_ref, vn_ref))


def _sample_keys(cache_k, cache_v, k_new, v_new, *, tk):
    bsz, nc, nh, hd = cache_k.shape
    dec = k_new.shape[1]
    full, tail = nc // tk, nc % tk
    assert tail % BF16_SUBLANES == 0 and tail > 0 and tk % tail == 0 and tail + dec <= tk
    nt = full + 1
    kvw = nh * hd
    main = pl.BlockSpec((None, tk, nh, hd), lambda b, t: (b, jnp.minimum(t, full - 1), 0, 0))
    last = pl.BlockSpec((None, tail, nh, hd), lambda b, t: (b, full * tk // tail, 0, 0))
    new = pl.BlockSpec((None, dec, kvw), lambda b, t: (b, 0, 0))
    return pl.pallas_call(
        functools.partial(_keys_kernel, full_tiles=full),
        grid=(bsz, nt),
        in_specs=[main, last, main, last, new, new],
        out_specs=[pl.BlockSpec((None, tk, kvw), lambda b, t: (b, t, 0)),
                   pl.BlockSpec((kvw, tk), lambda b, t: (0, b * nt + t))],
        out_shape=[jax.ShapeDtypeStruct((bsz, nt * tk, kvw), BF16), jax.ShapeDtypeStruct((kvw, bsz * nt * tk), BF16)],
        compiler_params=pltpu.CompilerParams(dimension_semantics=("arbitrary", "arbitrary"),
                                             vmem_limit_bytes=VMEM_LIMIT_BYTES),
        name="keys",
    )(cache_k, cache_k, cache_v, cache_v, k_new, v_new)


def _mix_kernel(tb_ref, o_ref, oprev_ref, z_ref, halo_ref, sga_ref, sgb_ref, x_ref, wpool_ref, ps_ref, wao_ref,
                wout_ref, x1_ref, zc_scr, *, tm, lead, tiles):
    if lead:
        head = jnp.where(pl.program_id(0) % tiles == 0, jnp.zeros_like(oprev_ref[...]), oprev_ref[...])
        o = jnp.concatenate([head, o_ref[0:tm - lead, :]], axis=0)
    else:
        o = o_ref[...]
    hs = halo_ref.shape[0]
    zc_scr[0:hs, :] = halo_ref[...]
    zc_scr[hs:hs + tm, :] = z_ref[...]
    avail = tb_ref[pl.program_id(0)] + lax.broadcasted_iota(I32, (tm, 1), 0) + 1
    gw = z_ref.shape[1] // len(POOL_WINDOWS)
    ys = []
    for g, win in enumerate(POOL_WINDOWS):
        cols = slice(g * gw, (g + 1) * gw)
        zt = zc_scr[hs:hs + tm, cols]
        s = zt
        for j in range(1, win):
            s = s + zc_scr[hs - j:hs - j + tm, cols]
        cnt = jnp.minimum(avail, win).astype(F32)
        dlt = s / cnt - zt
        ys.append(_dot(dlt.astype(BF16), wpool_ref[g]))
    a = jnp.concatenate(ys, axis=1) * ps_ref[...]
    b = _dot(o, wao_ref[...])
    mrg = sga_ref[...] * a + sgb_ref[...] * b
    x1_ref[...] = x_ref[...] + _dot(mrg.astype(BF16), wout_ref[...])


def _mix(tbase, o, z, halo, sga, sgb, x, w, *, tm, lead):
    rows, d = x.shape
    bsz, so, _ = o.shape
    nt = rows // tm
    tiles = nt // bsz
    ot = so // tm
    hs = halo.shape[1]
    lb = max(lead, BF16_SUBLANES)
    row = lambda n: pl.BlockSpec((tm, n), lambda i, s: (i, 0))
    const = lambda a: pl.BlockSpec(a.shape, lambda i, s: (0,) * a.ndim)
    o_cur = pl.BlockSpec((None, tm, d), lambda i, s: (i // tiles, jnp.minimum(i % tiles, ot - 1), 0))
    o_prev = pl.BlockSpec((None, lb, d), lambda i, s: (i // tiles, jnp.maximum((i % tiles) * (tm // lb) - 1, 0), 0))
    return pl.pallas_call(
        functools.partial(_mix_kernel, tm=tm, lead=lead, tiles=tiles),
        grid_spec=pltpu.PrefetchScalarGridSpec(
            num_scalar_prefetch=1,
            grid=(nt,),
            in_specs=[o_cur, o_prev, row(d), pl.BlockSpec((None, hs, d), lambda i, s: (i, 0, 0)), row(d), row(d), row(d),
                      const(w["wpool"]), const(w["pscale"]), const(w["wao"]), const(w["wout"])],
            out_specs=row(d),
            scratch_shapes=[pltpu.VMEM((hs + tm, d), F32)]),
        out_shape=jax.ShapeDtypeStruct((rows, d), F32),
        compiler_params=pltpu.CompilerParams(dimension_semantics=("arbitrary",), vmem_limit_bytes=VMEM_LIMIT_BYTES),
        name="mix",
    )(tbase, o, o, z, halo, sga, sgb, x, w["wpool"], w["pscale"], w["wao"], w["wout"])


ROUTE_GROUP_LANE0 = N_EXPERTS


def _moe_kernel(x_ref, nxt_ref, gf_ref, gl_ref, wr_ref, br_ref, wg_ref, wu_ref, wd_ref, y_ref,
                x_scr, h_scr, route_scr, acc_scr, *, ec, lead):
    j = pl.program_id(1)
    tm = x_ref.shape[0]

    @pl.when(j == 0)
    def _():
        x_scr[...] = jnp.concatenate([x_ref[lead:, :], nxt_ref[...]], axis=0) if lead else x_ref[...]
        hf = _rms(x_scr[...], gf_ref[...])
        h_hi = hf.astype(BF16)
        h_lo = (hf - h_hi.astype(F32)).astype(BF16)
        h_scr[...] = h_hi
        hh = _dot(h_hi, wr_ref[...])
        logits = hh[:, :LANES] + hh[:, LANES:] + _dot(h_lo, wr_ref[:, :LANES]) + br_ref[...]
        lane = lax.broadcasted_iota(I32, (tm, LANES), 1)
        lanef = lane.astype(F32)
        ninf = -jnp.inf
        isg = (lane >= ROUTE_GROUP_LANE0) & (lane < ROUTE_GROUP_LANE0 + N_EXPERT_GROUPS)
        gl = jnp.where(isg, logits, ninf)
        gmax = jnp.max(gl, axis=1, keepdims=True)
        gsel = jnp.min(jnp.where(gl == gmax, lanef, 1e9), axis=1, keepdims=True)
        gprob = 1.0 / jnp.sum(jnp.exp(gl - gmax), axis=1, keepdims=True)
        e0 = (gsel - ROUTE_GROUP_LANE0) * EXPERTS_PER_GROUP
        el = jnp.where((lanef >= e0) & (lanef < e0 + EXPERTS_PER_GROUP), logits, ninf)
        v1 = jnp.max(el, axis=1, keepdims=True)
        i1 = jnp.min(jnp.where(el == v1, lanef, 1e9), axis=1, keepdims=True)
        el2 = jnp.where(lanef == i1, ninf, el)
        v2 = jnp.max(el2, axis=1, keepdims=True)
        i2 = jnp.min(jnp.where(el2 == v2, lanef, 1e9), axis=1, keepdims=True)
        e = jnp.exp(v2 - v1)
        w1 = (1.0 / (1.0 + e)) * gprob
        w2 = (e / (1.0 + e)) * gprob
        for n, val in enumerate((i1, i2, w1, w2)):
            route_scr[n] = jnp.broadcast_to(val, (tm, LANES))
        acc_scr[...] = jnp.zeros(acc_scr.shape, F32)

    h = h_scr[...]
    i1, i2, w1, w2 = route_scr[0], route_scr[1], route_scr[2], route_scr[3]
    gate = _dot(h, wg_ref[...])
    up = _dot(h, wu_ref[...])
    act = gate * jax.nn.sigmoid(gate) * up
    parts = []
    for q in range(ec):
        ef = jnp.full((tm, LANES), j * ec + q, I32).astype(F32)
        comb = jnp.where(i1 == ef, w1, 0.0) + jnp.where(i2 == ef, w2, 0.0)
        parts.append(act[:, q * D_EXPERT:(q + 1) * D_EXPERT] * comb)
    acc_scr[...] += _dot(jnp.concatenate(parts, axis=1).astype(BF16), wd_ref[...])

    @pl.when(j == pl.num_programs(1) - 1)
    def _():
        y_ref[...] = _rms(x_scr[...] + acc_scr[...], gl_ref[...])


def _moe(x, w, *, tm, n_out, lead, ec=EXPERTS_PER_GROUP):
    bsz, s, d = x.shape
    assert n_out % tm == 0 and n_out + lead <= s
    nt = n_out // tm
    ne = N_EXPERTS // ec
    cw = ec * D_EXPERT
    lb = max(lead, SUBLANES)
    const = lambda a: pl.BlockSpec(a.shape, lambda i, j: (0,) * a.ndim)
    tile = pl.BlockSpec((None, tm, d), lambda i, j: (i // nt, i % nt, 0))
    after = pl.BlockSpec((None, lb, d), lambda i, j: (i // nt, jnp.minimum((i % nt + 1) * (tm // lb), s // lb - 1), 0))
    return pl.pallas_call(
        functools.partial(_moe_kernel, ec=ec, lead=lead),
        grid=(bsz * nt, ne),
        in_specs=[tile, after, const(w["gffn"]), const(w["gfinal"]), const(w["wr"]), const(w["br"]),
                  pl.BlockSpec((d, cw), lambda i, j: (0, j)), pl.BlockSpec((d, cw), lambda i, j: (0, j)),
                  pl.BlockSpec((cw, d), lambda i, j: (j, 0))],
        out_specs=tile,
        out_shape=jax.ShapeDtypeStruct((bsz, n_out, d), F32),
        scratch_shapes=[pltpu.VMEM((tm, d), F32), pltpu.VMEM((tm, d), BF16), pltpu.VMEM((4, tm, LANES), F32),
                        pltpu.VMEM((tm, d), F32)],
        compiler_params=pltpu.CompilerParams(dimension_semantics=("arbitrary", "arbitrary"),
                                             vmem_limit_bytes=VMEM_LIMIT_BYTES),
        name="moe",
    )(x, x, w["gffn"], w["gfinal"], w["wr"], w["br"], w["wg"], w["wu"], w["wd"])


def _rope_table(pos):
    posf = pos.astype(F32)[:, None]

    def cs(half):
        inv = jnp.exp(-math.log(ROPE_THETA) * jnp.arange(half, dtype=F32) * (1.0 / half))
        ang = posf * inv[None, :]
        return jnp.cos(ang), jnp.sin(ang)

    cq, sq = cs(QK_HALF)
    ci, si = cs(IX_HALF)
    used = 2 * QK_HALF + 2 * IX_HALF
    return jnp.concatenate([cq, sq, ci, si, jnp.zeros((pos.shape[0], LANES - used), F32)], axis=1)


def _prep_weights(l, norm_mix_g, norm_ffn_g, norm_final_g, w_in, w_pool, pool_scale, w_attn_out, w_out,
                  w_router_group, b_router_group, w_router_expert, b_router_expert,
                  w_expert_gate, w_expert_up, w_expert_down):
    d = w_in.shape[1]
    pw = w_pool.shape[1] * w_pool.shape[2]
    qw = N_HEADS * HEAD_DIM
    kvw = N_KV_HEADS * HEAD_DIM
    widths = (pw, qw, kvw, kvw, N_IDX_HEADS * IDX_DIM, IDX_DIM, N_IDX_HEADS, d, d)
    offs = np.concatenate([[0], np.cumsum(widths)])
    wi = w_in[l]
    piece = lambda a, b: wi[:, offs[a]:offs[b]].astype(BF16)
    wsm = jnp.concatenate([wi[:, offs[5]:offs[7]], jnp.zeros((d, LANES - IDX_DIM - N_IDX_HEADS), F32)], axis=1)
    wr = jnp.concatenate([w_router_expert[l], w_router_group[l],
                          jnp.zeros((d, LANES - N_EXPERTS - N_EXPERT_GROUPS), F32)], axis=1)
    wr_hi = wr.astype(BF16)
    br = jnp.concatenate([b_router_expert[l], b_router_group[l],
                          jnp.zeros((LANES - N_EXPERTS - N_EXPERT_GROUPS,), F32)])[None, :]
    flat_e = lambda a: a.transpose(1, 0, 2).reshape(d, N_EXPERTS * D_EXPERT).astype(BF16)
    return dict(
        gmix=norm_mix_g[l][None, :], gffn=norm_ffn_g[l][None, :], gfinal=norm_final_g[None, :],
        wz=piece(0, 1), wq=piece(1, 2), wkv=piece(2, 4), wqi=piece(4, 5), wsm=wsm.astype(BF16), wgab=piece(7, 9),
        wpool=w_pool[l].astype(BF16), pscale=pool_scale[l][None, :],
        wao=w_attn_out[l].astype(BF16), wout=w_out[l].astype(BF16),
        wr=jnp.concatenate([wr_hi, (wr - wr_hi.astype(F32)).astype(BF16)], axis=1), br=br,
        wg=flat_e(w_expert_gate[l]), wu=flat_e(w_expert_up[l]),
        wd=w_expert_down[l].reshape(N_EXPERTS * D_EXPERT, d).astype(BF16),
    )


def _dup_lanes(ki):
    return jnp.concatenate([ki, ki], axis=-1)


def kernel(x_prompt, x_sample, cache_k, cache_v, cache_idx_k, state_pool, meta_tokens, norm_mix_g, norm_ffn_g,
           norm_final_g, w_in, w_pool, pool_scale, w_attn_out, w_out, w_router_group, b_router_group,
           w_router_expert, b_router_expert, w_expert_gate, w_expert_up, w_expert_down):
    bsz, seq, d = x_prompt.shape
    dbsz, dec, _ = x_sample.shape
    assert w_in.shape[0] == 1, "single-layer step"
    l = 0
    past = cache_k.shape[2] - N_META
    n = N_META + seq
    topk_p = min(IDX_TOPK, seq // 4)
    topk_s = min(IDX_TOPK, (past + dec) // 4)
    kvw = N_KV_HEADS * HEAD_DIM

    tk = ATTN_TK
    tm = tk
    assert seq % tm == 0 and N_META <= tm
    sp = seq + tm
    nt = sp // tm
    tm_moe = _pick_tile(seq, 1100)
    w = _prep_weights(l, norm_mix_g, norm_ffn_g, norm_final_g, w_in, w_pool, pool_scale, w_attn_out, w_out,
                      w_router_group, b_router_group, w_router_expert, b_router_expert,
                      w_expert_gate, w_expert_up, w_expert_down)

    tab_p = _rope_table(jnp.arange(sp))
    xp, z, q, k_c, v_c, kb, vt, qis, sm, kid, sga, sgb = _inproj(
        x_prompt, meta_tokens.astype(F32), w["gmix"], tab_p, w, tm=tm, n_out=n, lead=N_META)
    b3 = lambda a: a.reshape(bsz, sp, a.shape[-1])
    o = _attention(b3(qis), b3(sm), b3(kid), b3(q), b3(kb), vt, tq=ATTN_TQ_PROMPT, n_q=seq, lead=N_META,
                   topk=topk_p, causal=True, n_keys=n)
    hs = POOL_STATE + 1
    z4 = z.reshape(bsz, nt, tm, d)
    halo = jnp.concatenate([jnp.zeros((bsz, 1, hs, d), F32), z4[:, :-1, tm - hs:, :]], axis=1).reshape(bsz * nt, hs, d)
    tbase = jnp.asarray(np.tile(np.arange(nt) * tm, bsz), I32)
    x1 = _mix(tbase, o, z, halo, sga, sgb, xp, w, tm=tm, lead=N_META)
    y_prompt = _moe(b3(x1), w, tm=tm_moe, n_out=seq, lead=N_META)
    k_prompt = k_c[None]
    v_prompt = v_c[None]
    idx_k_prompt = b3(sm)[:, :n, :IDX_DIM][None]
    pool_prompt = b3(z)[:, n - POOL_STATE:n][None]

    rows_s = dbsz * dec
    tab_s = jnp.tile(_rope_table(N_META + past + jnp.arange(dec)), (dbsz, 1))
    xs = x_sample.reshape(rows_s, d)
    _, zs, qs, ks_c, vs_c, kbs, _, qiss, sms, kids, sgas, sgbs = _inproj(
        xs[None], jnp.zeros((SUBLANES, d), F32), w["gmix"], tab_s, w, tm=rows_s, n_out=rows_s, lead=0)
    vs = vs_c.reshape(rows_s, kvw)
    nk = N_META + past + dec
    sk = _round_up(nk, tk)
    sq_s = _round_up(dec, ATTN_TQ)
    s3 = lambda a: a.reshape(dbsz, dec, a.shape[-1])
    padr = lambda a, rows: jnp.concatenate([a, jnp.zeros((dbsz, rows - a.shape[1], a.shape[-1]), a.dtype)], axis=1)
    k_all, vt_all = _sample_keys(cache_k[l], cache_v[l], s3(kbs), s3(vs), tk=tk)
    assert k_all.shape[1] == sk
    ki_all = padr(jnp.concatenate([_dup_lanes(cache_idx_k[l]).astype(BF16), s3(kids)], axis=1), sk)
    os_ = _attention(padr(s3(qiss), sq_s), padr(s3(sms), sq_s), ki_all, padr(s3(qs), sq_s), k_all, vt_all,
                     tq=ATTN_TQ, n_q=sq_s, lead=0, topk=topk_s, causal=False, n_keys=nk)[:, :dec]
    zcat = jnp.concatenate([state_pool[l].astype(F32), s3(zs)], axis=1)
    halo_s = jnp.concatenate([jnp.zeros((dbsz, 1, d), F32), zcat[:, :POOL_STATE]], axis=1)
    tbase_s = jnp.full((dbsz,), POOL_STATE, I32)
    x1s = _mix(tbase_s, os_, zs, halo_s, sgas, sgbs, xs, w, tm=dec, lead=0)
    y_sample = _moe(x1s[None], w, tm=rows_s, n_out=rows_s, lead=0).reshape(dbsz, dec, d)
    k_sample = ks_c.reshape(1, dbsz, dec, N_KV_HEADS, HEAD_DIM)
    v_sample = vs_c.reshape(1, dbsz, dec, N_KV_HEADS, HEAD_DIM)
    idx_k_sample = s3(sms)[:, :, :IDX_DIM][None]
    pool_sample = zcat[:, zcat.shape[1] - POOL_STATE:][None]

    return (y_prompt, y_sample, k_prompt, v_prompt, idx_k_prompt, pool_prompt,
            k_sample, v_sample, idx_k_sample, pool_sample)
```

```python
import functools
import math

import numpy as np
import jax
import jax.numpy as jnp
from jax import lax
from jax.experimental import pallas as pl
from jax.experimental.pallas import tpu as pltpu

CHUNK = 64
N_META = 16
EPS = 1e-6
POOL_WINDOWS = (2, 4, 8, 16)
POOL_STATE = max(POOL_WINDOWS) - 1
N_HEADS = 8
N_KV_HEADS = 2
HEAD_DIM = 128
KV_REP = N_HEADS // N_KV_HEADS
N_IDX_HEADS = 4
IDX_DIM = 64
IDX_TOPK = 256
IDX_W_SCALE = (N_IDX_HEADS * IDX_DIM) ** -0.5
ATTN_SCALE = HEAD_DIM ** -0.5
Q_PRESCALE = ATTN_SCALE * math.log2(math.e)
ROPE_THETA = 500000.0
ROT_FRACTION = 4
N_EXPERT_GROUPS = 4
EXPERTS_PER_GROUP = 8
N_EXPERTS = N_EXPERT_GROUPS * EXPERTS_PER_GROUP
D_EXPERT = 128

LANES = 128
SUBLANES = 8
BF16_SUBLANES = 16
VMEM_LIMIT_BYTES = 56 * 1024 * 1024

QK_HALF = HEAD_DIM // ROT_FRACTION // 2
IX_HALF = IDX_DIM // ROT_FRACTION // 2
TAB_COS_QK = 0
TAB_SIN_QK = QK_HALF
TAB_COS_IX = 2 * QK_HALF
TAB_SIN_IX = 2 * QK_HALF + IX_HALF

ATTN_TQ = LANES
ATTN_TQ_PROMPT = 2 * LANES
ATTN_TK = 512
TIE_BLOCK = 256
F32_TINY = float(np.finfo(np.float32).tiny)
MAX_PROBES = 64
NEG_BIG = -1e30

F32 = jnp.float32
BF16 = jnp.bfloat16
I32 = jnp.int32


def _round_up(x, m):
    return -(-x // m) * m


def _pick_tile(rows, cap):
    return max(t for t in range(BF16_SUBLANES, cap + 1, BF16_SUBLANES) if rows % t == 0)


def _rms(x, g):
    return x * lax.rsqrt(jnp.mean(x * x, axis=-1, keepdims=True) + EPS) * g


def _dot(a, b):
    return jnp.dot(a, b, preferred_element_type=F32)


def _dot_nt(a, b):
    return lax.dot_general(a, b, (((1,), (1,)), ((), ())), preferred_element_type=F32)


def _rope(y, c, a, b, shift_a, shift_b):
    outs = []
    for j in range(y.shape[1] // LANES):
        ys = y[:, j * LANES:(j + 1) * LANES]
        outs.append(ys * c + pltpu.roll(ys, shift_a, 1) * a + pltpu.roll(ys, shift_b, 1) * b)
    return outs[0] if len(outs) == 1 else jnp.concatenate(outs, axis=1)


def _inproj_kernel(x_ref, prev_ref, meta_ref, g_ref, freq_ref, wz_ref, wq_ref, wkv_ref, wqi_ref, wsm_ref, wgab_ref,
                   xs_ref, z_ref, q_ref, k4_ref, v4_ref, kb_ref, vt_ref, qis_ref, sm_ref, kid_ref, sga_ref, sgb_ref,
                   *, lead, frame_tiles, pos_base, pos_period):
    tm, d = x_ref.shape
    ti = pl.program_id(0) % (frame_tiles + 1 if lead else frame_tiles)
    if lead:
        head = jnp.where(ti == 0, meta_ref[...], prev_ref[...])
        body = jnp.where(ti < frame_tiles, x_ref[0:tm - lead, :], 0.0)
        x = jnp.concatenate([head, body], axis=0)
    else:
        x = x_ref[...]
    xs_ref[...] = x
    h = _rms(x, g_ref[...]).astype(BF16)
    lane = lax.broadcasted_iota(I32, (tm, LANES), 1)
    row = lax.broadcasted_iota(I32, (tm, LANES), 0)
    pos = pos_base + (lax.rem(row, pos_period) if pos_period else ti * tm + row)
    ang = pos.astype(F32) * freq_ref[...]
    is_cos = (lane < TAB_SIN_QK) | ((lane >= TAB_COS_IX) & (lane < TAB_SIN_IX))
    t = jnp.where(is_cos, jnp.cos(ang), jnp.sin(ang))

    def rl(s):
        return pltpu.roll(t, s % LANES, 1) if s % LANES else t

    c_qk = jnp.where(lane < QK_HALF, rl(-TAB_COS_QK), jnp.where(lane < 2 * QK_HALF, rl(QK_HALF - TAB_COS_QK), 1.0))
    a_qk = jnp.where(lane < QK_HALF, -rl(-TAB_SIN_QK), 0.0)
    b_qk = jnp.where((lane >= QK_HALF) & (lane < 2 * QK_HALF), rl(QK_HALF - TAB_SIN_QK), 0.0)
    l64 = lane & (IDX_DIM - 1)
    hi = lane >= IDX_DIM
    cos_lo = jnp.where(hi, rl(IDX_DIM - TAB_COS_IX), rl(-TAB_COS_IX))
    cos_hi = jnp.where(hi, rl(IDX_DIM + IX_HALF - TAB_COS_IX), rl(IX_HALF - TAB_COS_IX))
    sin_lo = jnp.where(hi, rl(IDX_DIM - TAB_SIN_IX), rl(-TAB_SIN_IX))
    sin_hi = jnp.where(hi, rl(IDX_DIM + IX_HALF - TAB_SIN_IX), rl(IX_HALF - TAB_SIN_IX))
    c_ix = jnp.where(l64 < IX_HALF, cos_lo, jnp.where(l64 < 2 * IX_HALF, cos_hi, 1.0))
    a_ix = jnp.where(l64 < IX_HALF, -sin_lo, 0.0)
    b_ix = jnp.where((l64 >= IX_HALF) & (l64 < 2 * IX_HALF), sin_hi, 0.0)

    z_ref[...] = _dot(h, wz_ref[...])

    q = _rope(_dot(h, wq_ref[...]), c_qk, a_qk, b_qk, LANES - QK_HALF, QK_HALF)
    q_ref[...] = (q * Q_PRESCALE).astype(BF16)

    kv = _dot(h, wkv_ref[...])
    kvw = kv.shape[1] // 2
    k = _rope(kv[:, :kvw], c_qk, a_qk, b_qk, LANES - QK_HALF, QK_HALF)
    v = kv[:, kvw:]
    for g in range(N_KV_HEADS):
        k4_ref[:, g, :] = k[:, g * HEAD_DIM:(g + 1) * HEAD_DIM]
        v4_ref[:, g, :] = v[:, g * HEAD_DIM:(g + 1) * HEAD_DIM]
    kb_ref[...] = k.astype(BF16)
    vt_ref[...] = v.T.astype(BF16)

    qi = _rope(_dot(h, wqi_ref[...]), c_ix, a_ix, b_ix, LANES - IX_HALF, IX_HALF)
    pieces = []
    for hd in range(N_IDX_HEADS):
        col = qi[:, (hd // 2) * LANES:(hd // 2 + 1) * LANES]
        keep = hi if hd % 2 else jnp.logical_not(hi)
        pieces.append(jnp.where(keep, col, 0.0))
    qis_ref[...] = jnp.concatenate(pieces, axis=1).astype(BF16)

    c_sm = jnp.where(hi, IDX_W_SCALE, c_ix)
    a_sm = jnp.where(hi, 0.0, a_ix)
    b_sm = jnp.where(hi, 0.0, b_ix)
    sm = _rope(_dot(h, wsm_ref[...]), c_sm, a_sm, b_sm, LANES - IX_HALF, IX_HALF)
    sm_ref[...] = sm
    kid_ref[...] = jnp.where(hi, pltpu.roll(sm, IDX_DIM, 1), sm).astype(BF16)

    gab = _dot(h, wgab_ref[...])
    sga_ref[...] = jax.nn.sigmoid(gab[:, :d])
    sgb_ref[...] = jax.nn.sigmoid(gab[:, d:])


def _inproj(x, meta, g, freq, w, *, tm, n_out, lead, pos_base, pos_period):
    bsz, s, d = x.shape
    assert s % tm == 0
    ft = s // tm
    nt = ft + 1 if lead else ft
    rows = bsz * nt * tm
    qw = w["wq"].shape[1]
    kvw = w["wkv"].shape[1] // 2
    lb = max(lead, SUBLANES)
    row_spec = lambda n: pl.BlockSpec((tm, n), lambda i: (i, 0))
    const = lambda a: pl.BlockSpec(a.shape, lambda i: (0,) * a.ndim)
    cache_spec = pl.BlockSpec((None, tm, N_KV_HEADS, HEAD_DIM), lambda i: (i // nt, i % nt, 0, 0))
    cache_shape = jax.ShapeDtypeStruct((bsz, n_out, N_KV_HEADS, HEAD_DIM), F32)
    outs = [
        (d, F32),
        (d, F32),
        (qw, BF16),
        "cache", "cache",
        (kvw, BF16),
        "vt",
        (N_IDX_HEADS * LANES, BF16),
        (LANES, F32),
        (LANES, BF16),
        (d, F32), (d, F32),
    ]
    spec_of = lambda o: (cache_spec if o == "cache" else pl.BlockSpec((kvw, tm), lambda i: (0, i)) if o == "vt"
                         else row_spec(o[0]))
    shape_of = lambda o: (cache_shape if o == "cache" else jax.ShapeDtypeStruct((kvw, rows), BF16) if o == "vt"
                          else jax.ShapeDtypeStruct((rows, o[0]), o[1]))
    return pl.pallas_call(
        functools.partial(_inproj_kernel, lead=lead, frame_tiles=ft, pos_base=pos_base, pos_period=pos_period),
        grid=(bsz * nt,),
        in_specs=[pl.BlockSpec((None, tm, d), lambda i: (i // nt, jnp.minimum(i % nt, ft - 1), 0)),
                  pl.BlockSpec((None, lb, d), lambda i: (i // nt, jnp.maximum((i % nt) * (tm // lb) - 1, 0), 0)),
                  const(meta), const(g), const(freq),
                  const(w["wz"]), const(w["wq"]), const(w["wkv"]), const(w["wqi"]), const(w["wsm"]), const(w["wgab"])],
        out_specs=[spec_of(o) for o in outs],
        out_shape=[shape_of(o) for o in outs],
        compiler_params=pltpu.CompilerParams(dimension_semantics=("arbitrary",), vmem_limit_bytes=VMEM_LIMIT_BYTES),
        name="inproj",
    )(x, x, meta, g, freq, w["wz"], w["wq"], w["wkv"], w["wqi"], w["wsm"], w["wgab"])


def _row_fold(x, op, group=SUBLANES):
    ways = 4
    accs = []
    for j in range(x.shape[0] // group):
        part = x[j * group:(j + 1) * group]
        if j < ways:
            accs.append(part)
        else:
            accs[j % ways] = op(accs[j % ways], part)
    while len(accs) > 1:
        accs = [op(accs[j], accs[j + 1]) for j in range(0, len(accs) - 1, 2)] + accs[len(accs) & ~1:]
    return accs[0]


def _attn_kernel(nch_ref, qis_ref, qis_nx_ref, sm_ref, sm_nx_ref, kid_ref, q_ref, q_nx_ref, k_ref, vt_ref, tri_ref,
                 o_ref, sc_scr, acc_scr, s_scr, *, tq, tk, topk, causal, n_keys, lead):
    i = pl.program_id(1)
    nch = nch_ref[i]
    sub = tri_ref.shape[0]
    nq4 = KV_REP * tq
    rows = lambda ref, nx: jnp.concatenate([ref[lead:, :], nx[...]], axis=0) if lead else ref[...]
    qis, sm, q = rows(qis_ref, qis_nx_ref), rows(sm_ref, sm_nx_ref), rows(q_ref, q_nx_ref)
    qpos = lead + i * tq + lax.broadcasted_iota(I32, (1, tq), 1)
    if causal:
        lim = N_META + CHUNK * (jnp.right_shift(qpos - N_META, int(math.log2(CHUNK))) + 1)
        lim = jnp.minimum(lim, n_keys)
    else:
        lim = jnp.full((1, tq), n_keys, I32)

    qstack = jnp.concatenate([qis[:, hd * LANES:(hd + 1) * LANES] for hd in range(N_IDX_HEADS)], axis=0)
    smt = sm.T
    wlane = jnp.concatenate([smt[IDX_DIM + hd:IDX_DIM + hd + 1, :] for hd in range(N_IDX_HEADS)], axis=1)

    key_row = lax.broadcasted_iota(I32, (tk, tq), 0)

    def score_body(c2, st):
        return score_chunk(jnp.minimum(2 * c2 + 1, nch - 1), score_chunk(2 * c2, st))

    def score_chunk(c, st):
        smax, smin = st
        off = pl.multiple_of(c * tk, tk)
        sc = _dot_nt(kid_ref[pl.ds(off, tk), :], qstack)
        sc = jnp.maximum(sc, 0.0) * wlane
        s = sc[:, 0:tq]
        for hd in range(1, N_IDX_HEADS):
            s = s + sc[:, hd * tq:(hd + 1) * tq]
        ok = key_row < lim - off
        masked = jnp.where(ok, s, -jnp.inf)
        sc_scr[pl.ds(off, tk), :] = masked
        return (jnp.maximum(smax, _row_fold(masked, jnp.maximum)),
                jnp.minimum(smin, _row_fold(jnp.where(ok, s, jnp.inf), jnp.minimum)))

    smax, smin = lax.fori_loop(0, (nch + 1) // 2, score_body, (jnp.full((SUBLANES, tq), -jnp.inf, F32),
                                                    jnp.full((SUBLANES, tq), jnp.inf, F32)))
    smax = jnp.max(smax, axis=0, keepdims=True)
    smin = jnp.min(smin, axis=0, keepdims=True)

    def count_ge(cand):
        def hits(c, cnd):
            kb = sc_scr[pl.ds(pl.multiple_of(c * tk, tk), tk), :]
            return _row_fold(jnp.where(kb >= cnd, 1.0, 0.0), jnp.add)

        def body(c2, acc):
            second = 2 * c2 + 1
            cnd2 = jnp.where(second < nch, cand, jnp.inf)
            return acc + (hits(2 * c2, cand) + hits(jnp.minimum(second, nch - 1), cnd2))
        acc = lax.fori_loop(0, (nch + 1) // 2, body, jnp.zeros((SUBLANES, tq), F32))
        return jnp.sum(acc, axis=0, keepdims=True)

    take_all = lim <= topk
    kf = float(topk)
    lo0 = jnp.where(take_all, NEG_BIG, smin)
    hi0 = jnp.where(take_all, NEG_BIG, smax + jnp.abs(smax) * 2.0 ** -10 + F32_TINY * 2.0 ** 24)
    c_lo0 = lim.astype(F32)
    c_hi0 = jnp.where(take_all, kf, 0.0)
    closed0 = jnp.where(take_all, 1.0, 0.0)

    def search_cond(st):
        return (st[0] < MAX_PROBES) & (jnp.max(1.0 - st[5]) > 0.5)

    def search_body(st):
        return search_step(search_step(st))

    def search_step(st):
        it, lo, hi, c_lo, c_hi, closed = st
        mid = lo + 0.5 * (hi - lo)
        interp = hi - (hi - lo) * ((kf - c_hi) / jnp.maximum(c_lo - c_hi, 1.0))
        inside = lambda v: (v > lo) & (v < hi)
        cand = jnp.where((it & 1) == 0, interp, mid)
        cand = jnp.where((it == 0) & inside(0.0), 0.0, cand)
        cand = jnp.where((it == 1) & inside(F32_TINY), F32_TINY, cand)
        cand = jnp.where(inside(cand), cand, mid)
        cnt = count_ge(cand)
        is_open = closed < 0.5
        up = is_open & (cnt > kf)
        down = is_open & (cnt <= kf)
        lo = jnp.where(up, cand, lo)
        c_lo = jnp.where(up, cnt, c_lo)
        hi = jnp.where(down, cand, hi)
        c_hi = jnp.where(down, cnt, c_hi)
        done = (c_hi == kf) | jnp.logical_not(inside(lo + 0.5 * (hi - lo))) | ((lo == 0.0) & (hi == F32_TINY))
        return it + 1, lo, hi, c_lo, c_hi, jnp.where(done, 1.0, closed)

    _, thr_lo, thr_hi, _, c_hi, _ = lax.while_loop(search_cond, search_body,
                                                   (jnp.int32(0), lo0, hi0, c_lo0, c_hi0, closed0))
    n_tie = kf - c_hi

    acc_scr[...] = jnp.zeros(acc_scr.shape, F32)
    qgs = [jnp.concatenate([q[:, (g * KV_REP + r) * HEAD_DIM:(g * KV_REP + r + 1) * HEAD_DIM]
                            for r in range(KV_REP)], axis=0) for g in range(N_KV_HEADS)]

    ones_rows = jnp.ones((BF16_SUBLANES, tk), BF16)

    def logits(c, slot, seen):
        off = pl.multiple_of(c * tk, tk)
        kb = sc_scr[pl.ds(off, tk), :]
        above = kb >= thr_hi
        at_least = kb >= thr_lo
        eqf = jnp.where(at_least, 1.0, 0.0) - jnp.where(above, 1.0, 0.0)
        ranks = []
        for hf in range(tk // sub):
            e = eqf[hf * sub:(hf + 1) * sub]
            ranks.append(seen + _dot(tri_ref[...], e.astype(BF16)))
            seen = seen + jnp.sum(_row_fold(e, jnp.add), axis=0, keepdims=True)
        rank = jnp.concatenate(ranks, axis=0)
        maskf = jnp.where(above, 0.0, jnp.where(at_least, jnp.where(rank < n_tie, 0.0, NEG_BIG), NEG_BIG))
        mask4 = jnp.concatenate([maskf] * KV_REP, axis=1)
        mcs = []
        for g in range(N_KV_HEADS):
            kc = k_ref[pl.ds(off, tk), g * HEAD_DIM:(g + 1) * HEAD_DIM]
            s = _dot_nt(kc, qgs[g]) + mask4
            s_scr[slot, g] = s
            mcs.append(jnp.max(_row_fold(s, jnp.maximum), axis=0, keepdims=True))
        return seen, tuple(mcs)

    def update(c, slot, ms, mcs):
        off = pl.multiple_of(c * tk, tk)
        ms_n = []
        for g in range(N_KV_HEADS):
            vtc = vt_ref[g * HEAD_DIM:(g + 1) * HEAD_DIM, pl.ds(off, tk)]
            m_new = jnp.maximum(ms[g], mcs[g])
            alpha = jnp.exp2(ms[g] - m_new)
            p = jnp.exp2(s_scr[slot, g] - m_new).astype(BF16)
            acc_scr[g] = alpha * acc_scr[g] + _dot(jnp.concatenate([vtc, ones_rows], axis=0), p)
            ms_n.append(m_new)
        return tuple(ms_n)

    def flash_body(c, st):
        seen, ms, mcs = st
        slot = c & 1
        ms = update(c, slot, ms, mcs)
        seen, mcs = logits(c + 1, 1 - slot, seen)
        return seen, ms, mcs

    m0 = tuple(jnp.full((1, nq4), NEG_BIG, F32) for _ in range(N_KV_HEADS))
    seen0, mc0 = logits(0, 0, jnp.zeros((1, tq), F32))
    _, ms, mcs = lax.fori_loop(0, nch - 1, flash_body, (seen0, m0, mc0))
    update(nch - 1, (nch - 1) & 1, ms, mcs)

    for g in range(N_KV_HEADS):
        og = acc_scr[g, 0:HEAD_DIM] / acc_scr[g, HEAD_DIM:HEAD_DIM + 1]
        for r in range(KV_REP):
            hd = g * KV_REP + r
            o_ref[:, hd * HEAD_DIM:(hd + 1) * HEAD_DIM] = og[:, r * tq:(r + 1) * tq].T.astype(o_ref.dtype)


def _attention(qis, sm, kid, q, kb, vt, *, tq, n_q, lead, topk, causal, n_keys):
    tk, sub = ATTN_TK, TIE_BLOCK
    bsz, sq, _ = q.shape
    sk = kb.shape[1]
    assert n_q % tq == 0 and sk % tk == 0 and n_q + lead <= sq
    nq = n_q // tq
    if causal:
        ends = np.minimum(N_META + CHUNK * ((lead + np.arange(nq) * tq + tq - 1 - N_META) // CHUNK + 1), n_keys)
    else:
        ends = np.full((nq,), n_keys)
    nch = jnp.asarray(-(-ends // tk), I32)
    tri = jnp.asarray(np.tril(np.ones((sub, sub), np.float32), -1), BF16)
    lb = max(lead, BF16_SUBLANES)
    qrow = lambda n: pl.BlockSpec((None, tq, n), lambda b, i, s: (b, i, 0))
    qnext = lambda n: pl.BlockSpec((None, lb, n), lambda b, i, s: (b, jnp.minimum((i + 1) * (tq // lb), sq // lb - 1), 0))
    krow = lambda n: pl.BlockSpec((None, sk, n), lambda b, i, s: (b, 0, 0), pipeline_mode=pl.Buffered(1))
    kern = functools.partial(_attn_kernel, tq=tq, tk=tk, topk=topk, causal=causal, n_keys=n_keys, lead=lead)
    return pl.pallas_call(
        kern,
        grid_spec=pltpu.PrefetchScalarGridSpec(
            num_scalar_prefetch=1,
            grid=(bsz, nq),
            in_specs=[qrow(qis.shape[2]), qnext(qis.shape[2]), qrow(sm.shape[2]), qnext(sm.shape[2]),
                      krow(kid.shape[2]), qrow(q.shape[2]), qnext(q.shape[2]),
                      krow(kb.shape[2]),
                      pl.BlockSpec((vt.shape[0], sk), lambda b, i, s: (0, b), pipeline_mode=pl.Buffered(1)),
                      pl.BlockSpec((sub, sub), lambda b, i, s: (0, 0))],
            out_specs=qrow(q.shape[2]),
            scratch_shapes=[pltpu.VMEM((sk, tq), F32),
                            pltpu.VMEM((N_KV_HEADS, HEAD_DIM + BF16_SUBLANES, KV_REP * tq), F32),
                            pltpu.VMEM((2, N_KV_HEADS, tk, KV_REP * tq), F32)]),
        out_shape=jax.ShapeDtypeStruct((bsz, n_q, q.shape[2]), BF16),
        compiler_params=pltpu.CompilerParams(dimension_semantics=("arbitrary", "arbitrary"),
                                             vmem_limit_bytes=VMEM_LIMIT_BYTES),
        name="attn",
    )(nch, qis, qis, sm, sm, kid, q, q, kb, vt, tri)


def _keys_kernel(ck_ref, ckt_ref, cv_ref, cvt_ref, kn_ref, vn_ref, k_ref, vt_ref, *, full_tiles):
    t = pl.program_id(1)
    tk = k_ref.shape[0]
    flat = lambda ref: jnp.concatenate([ref[:, g, :] for g in range(N_KV_HEADS)], axis=1)

    def emit(k, v):
        k_ref[...] = k.astype(BF16)
        vt_ref[...] = v.T.astype(BF16)

    @pl.when(t < full_tiles)
    def _():
        emit(flat(ck_ref), flat(cv_ref))

    @pl.when(t == full_tiles)
    def _():
        def tile(tail_ref, new_ref):
            tail, new = flat(tail_ref), new_ref[...].astype(F32)
            rest = jnp.zeros((tk - tail.shape[0] - new.shape[0], tail.shape[1]), F32)
            return jnp.concatenate([tail, new, rest], axis=0)
        emit(tile(ckt_ref, kn_ref), tile(cvt_ref, vn_ref))


def _sample_keys(cache_k, cache_v, k_new, v_new, *, tk):
    bsz, nc, nh, hd = cache_k.shape
    dec = k_new.shape[1]
    full, tail = nc // tk, nc % tk
    assert tail % BF16_SUBLANES == 0 and tail > 0 and tk % tail == 0 and tail + dec <= tk
    nt = full + 1
    kvw = nh * hd
    main = pl.BlockSpec((None, tk, nh, hd), lambda b, t: (b, jnp.minimum(t, full - 1), 0, 0))
    last = pl.BlockSpec((None, tail, nh, hd), lambda b, t: (b, full * tk // tail, 0, 0))
    new = pl.BlockSpec((None, dec, kvw), lambda b, t: (b, 0, 0))
    return pl.pallas_call(
        functools.partial(_keys_kernel, full_tiles=full),
        grid=(bsz, nt),
        in_specs=[main, last, main, last, new, new],
        out_specs=[pl.BlockSpec((None, tk, kvw), lambda b, t: (b, t, 0)),
                   pl.BlockSpec((kvw, tk), lambda b, t: (0, b * nt + t))],
        out_shape=[jax.ShapeDtypeStruct((bsz, nt * tk, kvw), BF16), jax.ShapeDtypeStruct((kvw, bsz * nt * tk), BF16)],
        compiler_params=pltpu.CompilerParams(dimension_semantics=("arbitrary", "arbitrary"),
                                             vmem_limit_bytes=VMEM_LIMIT_BYTES),
        name="keys",
    )(cache_k, cache_k, cache_v, cache_v, k_new, v_new)


def _mix_kernel(tb_ref, o_ref, oprev_ref, z_ref, halo_ref, sga_ref, sgb_ref, x_ref, wpool_ref, ps_ref, wao_ref,
                wout_ref, x1_ref, zc_scr, *, tm, lead, tiles):
    if lead:
        head = jnp.where(pl.program_id(0) % tiles == 0, jnp.zeros_like(oprev_ref[...]), oprev_ref[...])
        o = jnp.concatenate([head, o_ref[0:tm - lead, :]], axis=0)
    else:
        o = o_ref[...]
    hs = halo_ref.shape[0]
    zc_scr[0:hs, :] = halo_ref[...]
    zc_scr[hs:hs + tm, :] = z_ref[...]
    avail = tb_ref[pl.program_id(0)] + lax.broadcasted_iota(I32, (tm, 1), 0) + 1
    gw = z_ref.shape[1] // len(POOL_WINDOWS)
    ys = []
    for g, win in enumerate(POOL_WINDOWS):
        cols = slice(g * gw, (g + 1) * gw)
        zt = zc_scr[hs:hs + tm, cols]
        s = zt
        for j in range(1, win):
            s = s + zc_scr[hs - j:hs - j + tm, cols]
        cnt = jnp.minimum(avail, win).astype(F32)
        dlt = s / cnt - zt
        ys.append(_dot(dlt.astype(BF16), wpool_ref[g]))
    a = jnp.concatenate(ys, axis=1) * ps_ref[...]
    b = _dot(o, wao_ref[...])
    mrg = sga_ref[...] * a + sgb_ref[...] * b
    x1_ref[...] = x_ref[...] + _dot(mrg.astype(BF16), wout_ref[...])


def _mix(tbase, o, z, halo, sga, sgb, x, w, *, tm, lead):
    rows, d = x.shape
    bsz, so, _ = o.shape
    nt = rows // tm
    tiles = nt // bsz
    ot = so // tm
    hs = halo.shape[1]
    lb = max(lead, BF16_SUBLANES)
    row = lambda n: pl.BlockSpec((tm, n), lambda i, s: (i, 0))
    const = lambda a: pl.BlockSpec(a.shape, lambda i, s: (0,) * a.ndim)
    o_cur = pl.BlockSpec((None, tm, d), lambda i, s: (i // tiles, jnp.minimum(i % tiles, ot - 1), 0))
    o_prev = pl.BlockSpec((None, lb, d), lambda i, s: (i // tiles, jnp.maximum((i % tiles) * (tm // lb) - 1, 0), 0))
    return pl.pallas_call(
        functools.partial(_mix_kernel, tm=tm, lead=lead, tiles=tiles),
        grid_spec=pltpu.PrefetchScalarGridSpec(
            num_scalar_prefetch=1,
            grid=(nt,),
            in_specs=[o_cur, o_prev, row(d), pl.BlockSpec((None, hs, d), lambda i, s: (i, 0, 0)), row(d), row(d), row(d),
                      const(w["wpool"]), const(w["pscale"]), const(w["wao"]), const(w["wout"])],
            out_specs=row(d),
            scratch_shapes=[pltpu.VMEM((hs + tm, d), F32)]),
        out_shape=jax.ShapeDtypeStruct((rows, d), F32),
        compiler_params=pltpu.CompilerParams(dimension_semantics=("arbitrary",), vmem_limit_bytes=VMEM_LIMIT_BYTES),
        name="mix",
    )(tbase, o, o, z, halo, sga, sgb, x, w["wpool"], w["pscale"], w["wao"], w["wout"])


ROUTE_GROUP_LANE0 = N_EXPERTS


def _moe_kernel(x_ref, nxt_ref, gf_ref, gl_ref, wr_ref, br_ref, wg_ref, wu_ref, wd_ref, y_ref,
                x_scr, h_scr, route_scr, acc_scr, *, ec, lead):
    j = pl.program_id(1)
    tm = x_ref.shape[0]

    @pl.when(j == 0)
    def _():
        x_scr[...] = jnp.concatenate([x_ref[lead:, :], nxt_ref[...]], axis=0) if lead else x_ref[...]
        hf = _rms(x_scr[...], gf_ref[...])
        h_hi = hf.astype(BF16)
        h_lo = (hf - h_hi.astype(F32)).astype(BF16)
        h_scr[...] = h_hi
        hh = _dot(h_hi, wr_ref[...])
        logits = hh[:, :LANES] + hh[:, LANES:] + _dot(h_lo, wr_ref[:, :LANES]) + br_ref[...]
        lane = lax.broadcasted_iota(I32, (tm, LANES), 1)
        lanef = lane.astype(F32)
        ninf = -jnp.inf
        isg = (lane >= ROUTE_GROUP_LANE0) & (lane < ROUTE_GROUP_LANE0 + N_EXPERT_GROUPS)
        gl = jnp.where(isg, logits, ninf)
        gmax = jnp.max(gl, axis=1, keepdims=True)
        gsel = jnp.min(jnp.where(gl == gmax, lanef, 1e9), axis=1, keepdims=True)
        gprob = 1.0 / jnp.sum(jnp.exp(gl - gmax), axis=1, keepdims=True)
        e0 = (gsel - ROUTE_GROUP_LANE0) * EXPERTS_PER_GROUP
        el = jnp.where((lanef >= e0) & (lanef < e0 + EXPERTS_PER_GROUP), logits, ninf)
        v1 = jnp.max(el, axis=1, keepdims=True)
        i1 = jnp.min(jnp.where(el == v1, lanef, 1e9), axis=1, keepdims=True)
        el2 = jnp.where(lanef == i1, ninf, el)
        v2 = jnp.max(el2, axis=1, keepdims=True)
        i2 = jnp.min(jnp.where(el2 == v2, lanef, 1e9), axis=1, keepdims=True)
        e = jnp.exp(v2 - v1)
        w1 = (1.0 / (1.0 + e)) * gprob
        w2 = (e / (1.0 + e)) * gprob
        for n, val in enumerate((i1, i2, w1, w2)):
            route_scr[n] = jnp.broadcast_to(val, (tm, LANES))
        acc_scr[...] = jnp.zeros(acc_scr.shape, F32)

    h = h_scr[...]
    i1, i2, w1, w2 = route_scr[0], route_scr[1], route_scr[2], route_scr[3]
    gate = _dot(h, wg_ref[...])
    up = _dot(h, wu_ref[...])
    act = gate * jax.nn.sigmoid(gate) * up
    parts = []
    for q in range(ec):
        ef = jnp.full((tm, LANES), j * ec + q, I32).astype(F32)
        comb = jnp.where(i1 == ef, w1, 0.0) + jnp.where(i2 == ef, w2, 0.0)
        parts.append(act[:, q * D_EXPERT:(q + 1) * D_EXPERT] * comb)
    acc_scr[...] += _dot(jnp.concatenate(parts, axis=1).astype(BF16), wd_ref[...])

    @pl.when(j == pl.num_programs(1) - 1)
    def _():
        y_ref[...] = _rms(x_scr[...] + acc_scr[...], gl_ref[...])


def _moe(x, w, *, tm, n_out, lead, ec=EXPERTS_PER_GROUP):
    bsz, s, d = x.shape
    assert n_out % tm == 0 and n_out + lead <= s
    nt = n_out // tm
    ne = N_EXPERTS // ec
    cw = ec * D_EXPERT
    lb = max(lead, SUBLANES)
    const = lambda a: pl.BlockSpec(a.shape, lambda i, j: (0,) * a.ndim)
    tile = pl.BlockSpec((None, tm, d), lambda i, j: (i // nt, i % nt, 0))
    after = pl.BlockSpec((None, lb, d), lambda i, j: (i // nt, jnp.minimum((i % nt + 1) * (tm // lb), s // lb - 1), 0))
    return pl.pallas_call(
        functools.partial(_moe_kernel, ec=ec, lead=lead),
        grid=(bsz * nt, ne),
        in_specs=[tile, after, const(w["gffn"]), const(w["gfinal"]), const(w["wr"]), const(w["br"]),
                  pl.BlockSpec((d, cw), lambda i, j: (0, j)), pl.BlockSpec((d, cw), lambda i, j: (0, j)),
                  pl.BlockSpec((cw, d), lambda i, j: (j, 0))],
        out_specs=tile,
        out_shape=jax.ShapeDtypeStruct((bsz, n_out, d), F32),
        scratch_shapes=[pltpu.VMEM((tm, d), F32), pltpu.VMEM((tm, d), BF16), pltpu.VMEM((4, tm, LANES), F32),
                        pltpu.VMEM((tm, d), F32)],
        compiler_params=pltpu.CompilerParams(dimension_semantics=("arbitrary", "arbitrary"),
                                             vmem_limit_bytes=VMEM_LIMIT_BYTES),
        name="moe",
    )(x, x, w["gffn"], w["gfinal"], w["wr"], w["br"], w["wg"], w["wu"], w["wd"])


def _rope_freqs():
    inv = lambda half: jnp.exp(-math.log(ROPE_THETA) * jnp.arange(half, dtype=F32) * (1.0 / half))
    used = 2 * QK_HALF + 2 * IX_HALF
    return jnp.concatenate([inv(QK_HALF), inv(QK_HALF), inv(IX_HALF), inv(IX_HALF), jnp.zeros((LANES - used,), F32)])[None]


def _prep_weights(l, norm_mix_g, norm_ffn_g, norm_final_g, w_in, w_pool, pool_scale, w_attn_out, w_out,
                  w_router_group, b_router_group, w_router_expert, b_router_expert,
                  w_expert_gate, w_expert_up, w_expert_down):
    d = w_in.shape[1]
    pw = w_pool.shape[1] * w_pool.shape[2]
    qw = N_HEADS * HEAD_DIM
    kvw = N_KV_HEADS * HEAD_DIM
    widths = (pw, qw, kvw, kvw, N_IDX_HEADS * IDX_DIM, IDX_DIM, N_IDX_HEADS, d, d)
    offs = np.concatenate([[0], np.cumsum(widths)])
    wi = w_in[l]
    piece = lambda a, b: wi[:, offs[a]:offs[b]].astype(BF16)
    wsm = jnp.concatenate([wi[:, offs[5]:offs[7]], jnp.zeros((d, LANES - IDX_DIM - N_IDX_HEADS), F32)], axis=1)
    wr = jnp.concatenate([w_router_expert[l], w_router_group[l],
                          jnp.zeros((d, LANES - N_EXPERTS - N_EXPERT_GROUPS), F32)], axis=1)
    wr_hi = wr.astype(BF16)
    br = jnp.concatenate([b_router_expert[l], b_router_group[l],
                          jnp.zeros((LANES - N_EXPERTS - N_EXPERT_GROUPS,), F32)])[None, :]
    flat_e = lambda a: a.transpose(1, 0, 2).reshape(d, N_EXPERTS * D_EXPERT).astype(BF16)
    return dict(
        gmix=norm_mix_g[l][None, :], gffn=norm_ffn_g[l][None, :], gfinal=norm_final_g[None, :],
        wz=piece(0, 1), wq=piece(1, 2), wkv=piece(2, 4), wqi=piece(4, 5), wsm=wsm.astype(BF16), wgab=piece(7, 9),
        wpool=w_pool[l].astype(BF16), pscale=pool_scale[l][None, :],
        wao=w_attn_out[l].astype(BF16), wout=w_out[l].astype(BF16),
        wr=jnp.concatenate([wr_hi, (wr - wr_hi.astype(F32)).astype(BF16)], axis=1), br=br,
        wg=flat_e(w_expert_gate[l]), wu=flat_e(w_expert_up[l]),
        wd=w_expert_down[l].reshape(N_EXPERTS * D_EXPERT, d).astype(BF16),
    )


def _dup_lanes(ki):
    return jnp.concatenate([ki, ki], axis=-1)


def kernel(x_prompt, x_sample, cache_k, cache_v, cache_idx_k, state_pool, meta_tokens, norm_mix_g, norm_ffn_g,
           norm_final_g, w_in, w_pool, pool_scale, w_attn_out, w_out, w_router_group, b_router_group,
           w_router_expert, b_router_expert, w_expert_gate, w_expert_up, w_expert_down):
    bsz, seq, d = x_prompt.shape
    dbsz, dec, _ = x_sample.shape
    assert w_in.shape[0] == 1, "single-layer step"
    l = 0
    past = cache_k.shape[2] - N_META
    n = N_META + seq
    topk_p = min(IDX_TOPK, seq // 4)
    topk_s = min(IDX_TOPK, (past + dec) // 4)
    kvw = N_KV_HEADS * HEAD_DIM

    tk = ATTN_TK
    tm = tk
    assert seq % tm == 0 and N_META <= tm
    sp = seq + tm
    nt = sp // tm
    tm_moe = _pick_tile(seq, 1100)
    w = _prep_weights(l, norm_mix_g, norm_ffn_g, norm_final_g, w_in, w_pool, pool_scale, w_attn_out, w_out,
                      w_router_group, b_router_group, w_router_expert, b_router_expert,
                      w_expert_gate, w_expert_up, w_expert_down)

    freq = _rope_freqs()
    xp, z, q, k_c, v_c, kb, vt, qis, sm, kid, sga, sgb = _inproj(
        x_prompt, meta_tokens.astype(F32), w["gmix"], freq, w, tm=tm, n_out=n, lead=N_META, pos_base=0, pos_period=0)
    b3 = lambda a: a.reshape(bsz, sp, a.shape[-1])
    o = _attention(b3(qis), b3(sm), b3(kid), b3(q), b3(kb), vt, tq=ATTN_TQ_PROMPT, n_q=seq, lead=N_META,
                   topk=topk_p, causal=True, n_keys=n)
    hs = POOL_STATE + 1
    z4 = z.reshape(bsz, nt, tm, d)
    halo = jnp.concatenate([jnp.zeros((bsz, 1, hs, d), F32), z4[:, :-1, tm - hs:, :]], axis=1).reshape(bsz * nt, hs, d)
    tbase = jnp.asarray(np.tile(np.arange(nt) * tm, bsz), I32)
    x1 = _mix(tbase, o, z, halo, sga, sgb, xp, w, tm=tm, lead=N_META)
    y_prompt = _moe(b3(x1), w, tm=tm_moe, n_out=seq, lead=N_META)
    k_prompt = k_c[None]
    v_prompt = v_c[None]
    idx_k_prompt = b3(sm)[:, :n, :IDX_DIM][None]
    pool_prompt = b3(z)[:, n - POOL_STATE:n][None]

    rows_s = dbsz * dec
    xs = x_sample.reshape(rows_s, d)
    _, zs, qs, ks_c, vs_c, kbs, _, qiss, sms, kids, sgas, sgbs = _inproj(
        xs[None], jnp.zeros((SUBLANES, d), F32), w["gmix"], freq, w, tm=rows_s, n_out=rows_s, lead=0,
        pos_base=N_META + past, pos_period=dec)
    vs = vs_c.reshape(rows_s, kvw)
    nk = N_META + past + dec
    sk = _round_up(nk, tk)
    sq_s = _round_up(dec, ATTN_TQ)
    s3 = lambda a: a.reshape(dbsz, dec, a.shape[-1])
    padr = lambda a, rows: jnp.concatenate([a, jnp.zeros((dbsz, rows - a.shape[1], a.shape[-1]), a.dtype)], axis=1)
    k_all, vt_all = _sample_keys(cache_k[l], cache_v[l], s3(kbs), s3(vs), tk=tk)
    assert k_all.shape[1] == sk
    ki_all = padr(jnp.concatenate([_dup_lanes(cache_idx_k[l]).astype(BF16), s3(kids)], axis=1), sk)
    os_ = _attention(padr(s3(qiss), sq_s), padr(s3(sms), sq_s), ki_all, padr(s3(qs), sq_s), k_all, vt_all,
                     tq=ATTN_TQ, n_q=sq_s, lead=0, topk=topk_s, causal=False, n_keys=nk)[:, :dec]
    zcat = jnp.concatenate([state_pool[l].astype(F32), s3(zs)], axis=1)
    halo_s = jnp.concatenate([jnp.zeros((dbsz, 1, d), F32), zcat[:, :POOL_STATE]], axis=1)
    tbase_s = jnp.full((dbsz,), POOL_STATE, I32)
    x1s = _mix(tbase_s, os_, zs, halo_s, sgas, sgbs, xs, w, tm=dec, lead=0)
    y_sample = _moe(x1s[None], w, tm=rows_s, n_out=rows_s, lead=0).reshape(dbsz, dec, d)
    k_sample = ks_c.reshape(1, dbsz, dec, N_KV_HEADS, HEAD_DIM)
    v_sample = vs_c.reshape(1, dbsz, dec, N_KV_HEADS, HEAD_DIM)
    idx_k_sample = s3(sms)[:, :, :IDX_DIM][None]
    pool_sample = zcat[:, zcat.shape[1] - POOL_STATE:][None]

    return (y_prompt, y_sample, k_prompt, v_prompt, idx_k_prompt, pool_prompt,
            k_sample, v_sample, idx_k_sample, pool_sample)
```

```python
import functools
import math

import numpy as np
import jax
import jax.numpy as jnp
from jax import lax
from jax.experimental import pallas as pl
from jax.experimental.pallas import tpu as pltpu

CHUNK = 64
N_META = 16
EPS = 1e-6
POOL_WINDOWS = (2, 4, 8, 16)
POOL_STATE = max(POOL_WINDOWS) - 1
N_HEADS = 8
N_KV_HEADS = 2
HEAD_DIM = 128
KV_REP = N_HEADS // N_KV_HEADS
N_IDX_HEADS = 4
IDX_DIM = 64
IDX_TOPK = 256
IDX_W_SCALE = (N_IDX_HEADS * IDX_DIM) ** -0.5
ATTN_SCALE = HEAD_DIM ** -0.5
Q_PRESCALE = ATTN_SCALE * math.log2(math.e)
ROPE_THETA = 500000.0
ROT_FRACTION = 4
N_EXPERT_GROUPS = 4
EXPERTS_PER_GROUP = 8
N_EXPERTS = N_EXPERT_GROUPS * EXPERTS_PER_GROUP
D_EXPERT = 128

LANES = 128
SUBLANES = 8
BF16_SUBLANES = 16
VMEM_LIMIT_BYTES = 56 * 1024 * 1024

QK_HALF = HEAD_DIM // ROT_FRACTION // 2
IX_HALF = IDX_DIM // ROT_FRACTION // 2
TAB_COS_QK = 0
TAB_SIN_QK = QK_HALF
TAB_COS_IX = 2 * QK_HALF
TAB_SIN_IX = 2 * QK_HALF + IX_HALF

ATTN_TQ = LANES
ATTN_TQ_PROMPT = 2 * LANES
ATTN_TK = 512
TIE_BLOCK = 256
F32_TINY = float(np.finfo(np.float32).tiny)
MAX_PROBES = 64
NEG_BIG = -1e30

F32 = jnp.float32
BF16 = jnp.bfloat16
I32 = jnp.int32


def _round_up(x, m):
    return -(-x // m) * m


def _pick_tile(rows, cap):
    return max(t for t in range(BF16_SUBLANES, cap + 1, BF16_SUBLANES) if rows % t == 0)


def _rms(x, g):
    return x * lax.rsqrt(jnp.mean(x * x, axis=-1, keepdims=True) + EPS) * g


def _dot(a, b):
    return jnp.dot(a, b, preferred_element_type=F32)


def _dot_nt(a, b):
    return lax.dot_general(a, b, (((1,), (1,)), ((), ())), preferred_element_type=F32)


def _rope(y, c, a, b, shift_a, shift_b):
    outs = []
    for j in range(y.shape[1] // LANES):
        ys = y[:, j * LANES:(j + 1) * LANES]
        outs.append(ys * c + pltpu.roll(ys, shift_a, 1) * a + pltpu.roll(ys, shift_b, 1) * b)
    return outs[0] if len(outs) == 1 else jnp.concatenate(outs, axis=1)


def _inproj_kernel(x_ref, prev_ref, meta_ref, g_ref, freq_ref, wz_ref, wq_ref, wkv_ref, wqi_ref, wsm_ref, wgab_ref,
                   xs_ref, z_ref, q_ref, k4_ref, v4_ref, kb_ref, vt_ref, qis_ref, sm_ref, kid_ref, sga_ref, sgb_ref,
                   *, lead, frame_tiles, pos_base, pos_period):
    tm, d = x_ref.shape
    ti = pl.program_id(0) % (frame_tiles + 1 if lead else frame_tiles)
    if lead:
        head = jnp.where(ti == 0, meta_ref[...], prev_ref[...])
        body = jnp.where(ti < frame_tiles, x_ref[0:tm - lead, :], 0.0)
        x = jnp.concatenate([head, body], axis=0)
    else:
        x = x_ref[...]
    xs_ref[...] = x
    h = _rms(x, g_ref[...]).astype(BF16)
    lane = lax.broadcasted_iota(I32, (tm, LANES), 1)
    row = lax.broadcasted_iota(I32, (tm, LANES), 0)
    pos = pos_base + (lax.rem(row, pos_period) if pos_period else ti * tm + row)
    ang = pos.astype(F32) * freq_ref[...]
    is_cos = (lane < TAB_SIN_QK) | ((lane >= TAB_COS_IX) & (lane < TAB_SIN_IX))
    t = jnp.where(is_cos, jnp.cos(ang), jnp.sin(ang))

    def rl(s):
        return pltpu.roll(t, s % LANES, 1) if s % LANES else t

    c_qk = jnp.where(lane < QK_HALF, rl(-TAB_COS_QK), jnp.where(lane < 2 * QK_HALF, rl(QK_HALF - TAB_COS_QK), 1.0))
    a_qk = jnp.where(lane < QK_HALF, -rl(-TAB_SIN_QK), 0.0)
    b_qk = jnp.where((lane >= QK_HALF) & (lane < 2 * QK_HALF), rl(QK_HALF - TAB_SIN_QK), 0.0)
    l64 = lane & (IDX_DIM - 1)
    hi = lane >= IDX_DIM
    cos_lo = jnp.where(hi, rl(IDX_DIM - TAB_COS_IX), rl(-TAB_COS_IX))
    cos_hi = jnp.where(hi, rl(IDX_DIM + IX_HALF - TAB_COS_IX), rl(IX_HALF - TAB_COS_IX))
    sin_lo = jnp.where(hi, rl(IDX_DIM - TAB_SIN_IX), rl(-TAB_SIN_IX))
    sin_hi = jnp.where(hi, rl(IDX_DIM + IX_HALF - TAB_SIN_IX), rl(IX_HALF - TAB_SIN_IX))
    c_ix = jnp.where(l64 < IX_HALF, cos_lo, jnp.where(l64 < 2 * IX_HALF, cos_hi, 1.0))
    a_ix = jnp.where(l64 < IX_HALF, -sin_lo, 0.0)
    b_ix = jnp.where((l64 >= IX_HALF) & (l64 < 2 * IX_HALF), sin_hi, 0.0)

    z_ref[...] = _dot(h, wz_ref[...])

    q = _rope(_dot(h, wq_ref[...]), c_qk, a_qk, b_qk, LANES - QK_HALF, QK_HALF)
    q_ref[...] = (q * Q_PRESCALE).astype(BF16)

    kv = _dot(h, wkv_ref[...])
    kvw = kv.shape[1] // 2
    k = _rope(kv[:, :kvw], c_qk, a_qk, b_qk, LANES - QK_HALF, QK_HALF)
    v = kv[:, kvw:]
    for g in range(N_KV_HEADS):
        k4_ref[:, g, :] = k[:, g * HEAD_DIM:(g + 1) * HEAD_DIM]
        v4_ref[:, g, :] = v[:, g * HEAD_DIM:(g + 1) * HEAD_DIM]
    kb_ref[...] = k.astype(BF16)
    vt_ref[...] = v.T.astype(BF16)

    qi = _rope(_dot(h, wqi_ref[...]), c_ix, a_ix, b_ix, LANES - IX_HALF, IX_HALF)
    pieces = []
    for hd in range(N_IDX_HEADS):
        col = qi[:, (hd // 2) * LANES:(hd // 2 + 1) * LANES]
        keep = hi if hd % 2 else jnp.logical_not(hi)
        pieces.append(jnp.where(keep, col, 0.0))
    qis_ref[...] = jnp.concatenate(pieces, axis=1).astype(BF16)

    c_sm = jnp.where(hi, IDX_W_SCALE, c_ix)
    a_sm = jnp.where(hi, 0.0, a_ix)
    b_sm = jnp.where(hi, 0.0, b_ix)
    sm = _rope(_dot(h, wsm_ref[...]), c_sm, a_sm, b_sm, LANES - IX_HALF, IX_HALF)
    sm_ref[...] = sm
    kid_ref[...] = jnp.where(hi, pltpu.roll(sm, IDX_DIM, 1), sm).astype(BF16)

    gab = _dot(h, wgab_ref[...])
    sga_ref[...] = jax.nn.sigmoid(gab[:, :d])
    sgb_ref[...] = jax.nn.sigmoid(gab[:, d:])


def _inproj(x, meta, g, freq, w, *, tm, n_out, lead, pos_base, pos_period):
    bsz, s, d = x.shape
    assert s % tm == 0
    ft = s // tm
    nt = ft + 1 if lead else ft
    rows = bsz * nt * tm
    qw = w["wq"].shape[1]
    kvw = w["wkv"].shape[1] // 2
    lb = max(lead, SUBLANES)
    row_spec = lambda n: pl.BlockSpec((tm, n), lambda i: (i, 0))
    const = lambda a: pl.BlockSpec(a.shape, lambda i: (0,) * a.ndim)
    cache_spec = pl.BlockSpec((None, tm, N_KV_HEADS, HEAD_DIM), lambda i: (i // nt, i % nt, 0, 0))
    cache_shape = jax.ShapeDtypeStruct((bsz, n_out, N_KV_HEADS, HEAD_DIM), F32)
    outs = [
        (d, F32),
        (d, F32),
        (qw, BF16),
        "cache", "cache",
        (kvw, BF16),
        "vt",
        (N_IDX_HEADS * LANES, BF16),
        (LANES, F32),
        (LANES, BF16),
        (d, F32), (d, F32),
    ]
    spec_of = lambda o: (cache_spec if o == "cache" else pl.BlockSpec((kvw, tm), lambda i: (0, i)) if o == "vt"
                         else row_spec(o[0]))
    shape_of = lambda o: (cache_shape if o == "cache" else jax.ShapeDtypeStruct((kvw, rows), BF16) if o == "vt"
                          else jax.ShapeDtypeStruct((rows, o[0]), o[1]))
    return pl.pallas_call(
        functools.partial(_inproj_kernel, lead=lead, frame_tiles=ft, pos_base=pos_base, pos_period=pos_period),
        grid=(bsz * nt,),
        in_specs=[pl.BlockSpec((None, tm, d), lambda i: (i // nt, jnp.minimum(i % nt, ft - 1), 0)),
                  pl.BlockSpec((None, lb, d), lambda i: (i // nt, jnp.maximum((i % nt) * (tm // lb) - 1, 0), 0)),
                  const(meta), const(g), const(freq),
                  const(w["wz"]), const(w["wq"]), const(w["wkv"]), const(w["wqi"]), const(w["wsm"]), const(w["wgab"])],
        out_specs=[spec_of(o) for o in outs],
        out_shape=[shape_of(o) for o in outs],
        compiler_params=pltpu.CompilerParams(dimension_semantics=("arbitrary",), vmem_limit_bytes=VMEM_LIMIT_BYTES),
        name="inproj",
    )(x, x, meta, g, freq, w["wz"], w["wq"], w["wkv"], w["wqi"], w["wsm"], w["wgab"])


def _row_fold(x, op, group=SUBLANES):
    ways = 4
    accs = []
    for j in range(x.shape[0] // group):
        part = x[j * group:(j + 1) * group]
        if j < ways:
            accs.append(part)
        else:
            accs[j % ways] = op(accs[j % ways], part)
    while len(accs) > 1:
        accs = [op(accs[j], accs[j + 1]) for j in range(0, len(accs) - 1, 2)] + accs[len(accs) & ~1:]
    return accs[0]


def _attn_kernel(nch_ref, qis_ref, qis_nx_ref, sm_ref, sm_nx_ref, kid_ref, q_ref, q_nx_ref, k_ref, vt_ref, tri_ref,
                 o_ref, sc_scr, acc_scr, s_scr, *, tq, tk, topk, causal, n_keys, lead):
    i = pl.program_id(1)
    nch = nch_ref[i]
    sub = tri_ref.shape[0]
    nq4 = KV_REP * tq
    rows = lambda ref, nx: jnp.concatenate([ref[lead:, :], nx[...]], axis=0) if lead else ref[...]
    qis, sm, q = rows(qis_ref, qis_nx_ref), rows(sm_ref, sm_nx_ref), rows(q_ref, q_nx_ref)
    qpos = lead + i * tq + lax.broadcasted_iota(I32, (1, tq), 1)
    if causal:
        lim = N_META + CHUNK * (jnp.right_shift(qpos - N_META, int(math.log2(CHUNK))) + 1)
        lim = jnp.minimum(lim, n_keys)
    else:
        lim = jnp.full((1, tq), n_keys, I32)

    qstack = jnp.concatenate([qis[:, hd * LANES:(hd + 1) * LANES] for hd in range(N_IDX_HEADS)], axis=0)
    smt = sm.T
    wlane = jnp.concatenate([smt[IDX_DIM + hd:IDX_DIM + hd + 1, :] for hd in range(N_IDX_HEADS)], axis=1)

    key_row = lax.broadcasted_iota(I32, (tk, tq), 0)

    def score_body(c2, st):
        return score_chunk(jnp.minimum(2 * c2 + 1, nch - 1), score_chunk(2 * c2, st))

    def score_chunk(c, st):
        smax, smin = st
        off = pl.multiple_of(c * tk, tk)
        sc = _dot_nt(kid_ref[pl.ds(off, tk), :], qstack)
        sc = jnp.maximum(sc, 0.0) * wlane
        s = sc[:, 0:tq]
        for hd in range(1, N_IDX_HEADS):
            s = s + sc[:, hd * tq:(hd + 1) * tq]
        ok = key_row < lim - off
        masked = jnp.where(ok, s, -jnp.inf)
        sc_scr[pl.ds(off, tk), :] = masked
        return (jnp.maximum(smax, _row_fold(masked, jnp.maximum)),
                jnp.minimum(smin, _row_fold(jnp.where(ok, s, jnp.inf), jnp.minimum)))

    smax, smin = lax.fori_loop(0, (nch + 1) // 2, score_body, (jnp.full((SUBLANES, tq), -jnp.inf, F32),
                                                    jnp.full((SUBLANES, tq), jnp.inf, F32)))
    smax = jnp.max(smax, axis=0, keepdims=True)
    smin = jnp.min(smin, axis=0, keepdims=True)

    def count_ge(cand):
        def hits(c, cnd):
            kb = sc_scr[pl.ds(pl.multiple_of(c * tk, tk), tk), :]
            return _row_fold(jnp.where(kb >= cnd, 1.0, 0.0), jnp.add)

        def body(c2, acc):
            second = 2 * c2 + 1
            cnd2 = jnp.where(second < nch, cand, jnp.inf)
            return acc + (hits(2 * c2, cand) + hits(jnp.minimum(second, nch - 1), cnd2))
        acc = lax.fori_loop(0, (nch + 1) // 2, body, jnp.zeros((SUBLANES, tq), F32))
        return jnp.sum(acc, axis=0, keepdims=True)

    take_all = lim <= topk
    kf = float(topk)
    lo0 = jnp.where(take_all, NEG_BIG, smin)
    hi0 = jnp.where(take_all, NEG_BIG, smax + jnp.abs(smax) * 2.0 ** -10 + F32_TINY * 2.0 ** 24)
    c_lo0 = lim.astype(F32)
    c_hi0 = jnp.where(take_all, kf, 0.0)
    closed0 = jnp.where(take_all, 1.0, 0.0)

    def search_cond(st):
        return (st[0] < MAX_PROBES) & (jnp.max(1.0 - st[5]) > 0.5)

    def search_body(st):
        return search_step(search_step(st))

    def search_step(st):
        it, lo, hi, c_lo, c_hi, closed = st
        mid = lo + 0.5 * (hi - lo)
        interp = hi - (hi - lo) * ((kf - c_hi) / jnp.maximum(c_lo - c_hi, 1.0))
        inside = lambda v: (v > lo) & (v < hi)
        cand = jnp.where((it & 1) == 0, interp, mid)
        cand = jnp.where((it == 0) & inside(0.0), 0.0, cand)
        cand = jnp.where((it == 1) & inside(F32_TINY), F32_TINY, cand)
        cand = jnp.where(inside(cand), cand, mid)
        cnt = count_ge(cand)
        is_open = closed < 0.5
        up = is_open & (cnt > kf)
        down = is_open & (cnt <= kf)
        lo = jnp.where(up, cand, lo)
        c_lo = jnp.where(up, cnt, c_lo)
        hi = jnp.where(down, cand, hi)
        c_hi = jnp.where(down, cnt, c_hi)
        done = (c_hi == kf) | jnp.logical_not(inside(lo + 0.5 * (hi - lo))) | ((lo == 0.0) & (hi == F32_TINY))
        return it + 1, lo, hi, c_lo, c_hi, jnp.where(done, 1.0, closed)

    _, thr_lo, thr_hi, _, c_hi, _ = lax.while_loop(search_cond, search_body,
                                                   (jnp.int32(0), lo0, hi0, c_lo0, c_hi0, closed0))
    n_tie = kf - c_hi

    acc_scr[...] = jnp.zeros(acc_scr.shape, F32)
    qgs = [jnp.concatenate([q[:, (g * KV_REP + r) * HEAD_DIM:(g * KV_REP + r + 1) * HEAD_DIM]
                            for r in range(KV_REP)], axis=0) for g in range(N_KV_HEADS)]

    ones_rows = jnp.ones((BF16_SUBLANES, tk), BF16)

    def logits(c, slot, seen):
        off = pl.multiple_of(c * tk, tk)
        kb = sc_scr[pl.ds(off, tk), :]
        above = kb >= thr_hi
        at_least = kb >= thr_lo
        eqf = jnp.where(at_least, 1.0, 0.0) - jnp.where(above, 1.0, 0.0)
        ranks = []
        for hf in range(tk // sub):
            e = eqf[hf * sub:(hf + 1) * sub]
            ranks.append(seen + _dot(tri_ref[...], e.astype(BF16)))
            seen = seen + jnp.sum(_row_fold(e, jnp.add), axis=0, keepdims=True)
        rank = jnp.concatenate(ranks, axis=0)
        maskf = jnp.where(above, 0.0, jnp.where(at_least, jnp.where(rank < n_tie, 0.0, NEG_BIG), NEG_BIG))
        mask4 = jnp.concatenate([maskf] * KV_REP, axis=1)
        mcs = []
        for g in range(N_KV_HEADS):
            kc = k_ref[pl.ds(off, tk), g * HEAD_DIM:(g + 1) * HEAD_DIM]
            s = _dot_nt(kc, qgs[g]) + mask4
            s_scr[slot, g] = s
            mcs.append(jnp.max(_row_fold(s, jnp.maximum), axis=0, keepdims=True))
        return seen, tuple(mcs)

    def update(c, slot, ms, mcs):
        off = pl.multiple_of(c * tk, tk)
        ms_n = []
        for g in range(N_KV_HEADS):
            vtc = vt_ref[g * HEAD_DIM:(g + 1) * HEAD_DIM, pl.ds(off, tk)]
            m_new = jnp.maximum(ms[g], mcs[g])
            alpha = jnp.exp2(ms[g] - m_new)
            p = jnp.exp2(s_scr[slot, g] - m_new).astype(BF16)
            acc_scr[g] = alpha * acc_scr[g] + _dot(jnp.concatenate([vtc, ones_rows], axis=0), p)
            ms_n.append(m_new)
        return tuple(ms_n)

    def flash_body(c, st):
        seen, ms, mcs = st
        slot = c & 1
        ms = update(c, slot, ms, mcs)
        seen, mcs = logits(c + 1, 1 - slot, seen)
        return seen, ms, mcs

    m0 = tuple(jnp.full((1, nq4), NEG_BIG, F32) for _ in range(N_KV_HEADS))
    seen0, mc0 = logits(0, 0, jnp.zeros((1, tq), F32))
    _, ms, mcs = lax.fori_loop(0, nch - 1, flash_body, (seen0, m0, mc0))
    update(nch - 1, (nch - 1) & 1, ms, mcs)

    for g in range(N_KV_HEADS):
        og = acc_scr[g, 0:HEAD_DIM] / acc_scr[g, HEAD_DIM:HEAD_DIM + 1]
        for r in range(KV_REP):
            hd = g * KV_REP + r
            o_ref[:, hd * HEAD_DIM:(hd + 1) * HEAD_DIM] = og[:, r * tq:(r + 1) * tq].T.astype(o_ref.dtype)


def _attention(qis, sm, kid, q, kb, vt, *, tq, n_q, lead, topk, causal, n_keys):
    tk, sub = ATTN_TK, TIE_BLOCK
    bsz, sq, _ = q.shape
    sk = kb.shape[1]
    assert n_q % tq == 0 and sk % tk == 0 and n_q + lead <= sq
    nq = n_q // tq
    if causal:
        ends = np.minimum(N_META + CHUNK * ((lead + np.arange(nq) * tq + tq - 1 - N_META) // CHUNK + 1), n_keys)
    else:
        ends = np.full((nq,), n_keys)
    nch = jnp.asarray(-(-ends // tk), I32)
    tri = jnp.asarray(np.tril(np.ones((sub, sub), np.float32), -1), BF16)
    lb = max(lead, BF16_SUBLANES)
    qrow = lambda n: pl.BlockSpec((None, tq, n), lambda b, i, s: (b, i, 0))
    qnext = lambda n: pl.BlockSpec((None, lb, n), lambda b, i, s: (b, jnp.minimum((i + 1) * (tq // lb), sq // lb - 1), 0))
    krow = lambda n: pl.BlockSpec((None, sk, n), lambda b, i, s: (b, 0, 0), pipeline_mode=pl.Buffered(1))
    kern = functools.partial(_attn_kernel, tq=tq, tk=tk, topk=topk, causal=causal, n_keys=n_keys, lead=lead)
    return pl.pallas_call(
        kern,
        grid_spec=pltpu.PrefetchScalarGridSpec(
            num_scalar_prefetch=1,
            grid=(bsz, nq),
            in_specs=[qrow(qis.shape[2]), qnext(qis.shape[2]), qrow(sm.shape[2]), qnext(sm.shape[2]),
                      krow(kid.shape[2]), qrow(q.shape[2]), qnext(q.shape[2]),
                      krow(kb.shape[2]),
                      pl.BlockSpec((vt.shape[0], sk), lambda b, i, s: (0, b), pipeline_mode=pl.Buffered(1)),
                      pl.BlockSpec((sub, sub), lambda b, i, s: (0, 0))],
            out_specs=qrow(q.shape[2]),
            scratch_shapes=[pltpu.VMEM((sk, tq), F32),
                            pltpu.VMEM((N_KV_HEADS, HEAD_DIM + BF16_SUBLANES, KV_REP * tq), F32),
                            pltpu.VMEM((2, N_KV_HEADS, tk, KV_REP * tq), F32)]),
        out_shape=jax.ShapeDtypeStruct((bsz, n_q, q.shape[2]), BF16),
        compiler_params=pltpu.CompilerParams(dimension_semantics=("arbitrary", "arbitrary"),
                                             vmem_limit_bytes=VMEM_LIMIT_BYTES),
        name="attn",
    )(nch, qis, qis, sm, sm, kid, q, q, kb, vt, tri)


def _keys_kernel(ck_ref, ckt_ref, cv_ref, cvt_ref, kn_ref, vn_ref, k_ref, vt_ref, *, full_tiles):
    t = pl.program_id(1)
    tk = k_ref.shape[0]
    flat = lambda ref: jnp.concatenate(
        [ref[pl.ds(g, ref.shape[0] // N_KV_HEADS, stride=N_KV_HEADS), :] for g in range(N_KV_HEADS)], axis=1)

    def emit(k, v):
        k_ref[...] = k.astype(BF16)
        vt_ref[...] = v.T.astype(BF16)

    @pl.when(t < full_tiles)
    def _():
        emit(flat(ck_ref), flat(cv_ref))

    @pl.when(t == full_tiles)
    def _():
        def tile(tail_ref, new_ref):
            tail, new = flat(tail_ref), new_ref[...].astype(F32)
            rest = jnp.zeros((tk - tail.shape[0] - new.shape[0], tail.shape[1]), F32)
            return jnp.concatenate([tail, new, rest], axis=0)
        emit(tile(ckt_ref, kn_ref), tile(cvt_ref, vn_ref))


def _sample_keys(cache_k, cache_v, k_new, v_new, *, tk):
    bsz, nc, nh, hd = cache_k.shape
    dec = k_new.shape[1]
    full, tail = nc // tk, nc % tk
    assert tail % BF16_SUBLANES == 0 and tail > 0 and tk % tail == 0 and tail + dec <= tk
    nt = full + 1
    kvw = nh * hd
    cache_k, cache_v = (c.reshape(bsz, nc * nh, hd) for c in (cache_k, cache_v))
    main = pl.BlockSpec((None, tk * nh, hd), lambda b, t: (b, jnp.minimum(t, full - 1), 0))
    last = pl.BlockSpec((None, tail * nh, hd), lambda b, t: (b, full * tk // tail, 0))
    new = pl.BlockSpec((None, dec, kvw), lambda b, t: (b, 0, 0))
    return pl.pallas_call(
        functools.partial(_keys_kernel, full_tiles=full),
        grid=(bsz, nt),
        in_specs=[main, last, main, last, new, new],
        out_specs=[pl.BlockSpec((None, tk, kvw), lambda b, t: (b, t, 0)),
                   pl.BlockSpec((kvw, tk), lambda b, t: (0, b * nt + t))],
        out_shape=[jax.ShapeDtypeStruct((bsz, nt * tk, kvw), BF16), jax.ShapeDtypeStruct((kvw, bsz * nt * tk), BF16)],
        compiler_params=pltpu.CompilerParams(dimension_semantics=("arbitrary", "arbitrary"),
                                             vmem_limit_bytes=VMEM_LIMIT_BYTES),
        name="keys",
    )(cache_k, cache_k, cache_v, cache_v, k_new, v_new)


def _mix_kernel(tb_ref, o_ref, oprev_ref, z_ref, halo_ref, sga_ref, sgb_ref, x_ref, wpool_ref, ps_ref, wao_ref,
                wout_ref, x1_ref, zc_scr, *, tm, lead, tiles):
    if lead:
        head = jnp.where(pl.program_id(0) % tiles == 0, jnp.zeros_like(oprev_ref[...]), oprev_ref[...])
        o = jnp.concatenate([head, o_ref[0:tm - lead, :]], axis=0)
    else:
        o = o_ref[...]
    hs = halo_ref.shape[0]
    zc_scr[0:hs, :] = halo_ref[...]
    zc_scr[hs:hs + tm, :] = z_ref[...]
    avail = tb_ref[pl.program_id(0)] + lax.broadcasted_iota(I32, (tm, 1), 0) + 1
    gw = z_ref.shape[1] // len(POOL_WINDOWS)
    ys = []
    for g, win in enumerate(POOL_WINDOWS):
        cols = slice(g * gw, (g + 1) * gw)
        zt = zc_scr[hs:hs + tm, cols]
        s = zt
        for j in range(1, win):
            s = s + zc_scr[hs - j:hs - j + tm, cols]
        cnt = jnp.minimum(avail, win).astype(F32)
        dlt = s / cnt - zt
        ys.append(_dot(dlt.astype(BF16), wpool_ref[g]))
    a = jnp.concatenate(ys, axis=1) * ps_ref[...]
    b = _dot(o, wao_ref[...])
    mrg = sga_ref[...] * a + sgb_ref[...] * b
    x1_ref[...] = x_ref[...] + _dot(mrg.astype(BF16), wout_ref[...])


def _mix(tbase, o, z, halo, sga, sgb, x, w, *, tm, lead):
    rows, d = x.shape
    bsz, so, _ = o.shape
    nt = rows // tm
    tiles = nt // bsz
    ot = so // tm
    hs = halo.shape[1]
    lb = max(lead, BF16_SUBLANES)
    row = lambda n: pl.BlockSpec((tm, n), lambda i, s: (i, 0))
    const = lambda a: pl.BlockSpec(a.shape, lambda i, s: (0,) * a.ndim)
    o_cur = pl.BlockSpec((None, tm, d), lambda i, s: (i // tiles, jnp.minimum(i % tiles, ot - 1), 0))
    o_prev = pl.BlockSpec((None, lb, d), lambda i, s: (i // tiles, jnp.maximum((i % tiles) * (tm // lb) - 1, 0), 0))
    return pl.pallas_call(
        functools.partial(_mix_kernel, tm=tm, lead=lead, tiles=tiles),
        grid_spec=pltpu.PrefetchScalarGridSpec(
            num_scalar_prefetch=1,
            grid=(nt,),
            in_specs=[o_cur, o_prev, row(d), pl.BlockSpec((None, hs, d), lambda i, s: (i, 0, 0)), row(d), row(d), row(d),
                      const(w["wpool"]), const(w["pscale"]), const(w["wao"]), const(w["wout"])],
            out_specs=row(d),
            scratch_shapes=[pltpu.VMEM((hs + tm, d), F32)]),
        out_shape=jax.ShapeDtypeStruct((rows, d), F32),
        compiler_params=pltpu.CompilerParams(dimension_semantics=("arbitrary",), vmem_limit_bytes=VMEM_LIMIT_BYTES),
        name="mix",
    )(tbase, o, o, z, halo, sga, sgb, x, w["wpool"], w["pscale"], w["wao"], w["wout"])


ROUTE_GROUP_LANE0 = N_EXPERTS


def _moe_kernel(x_ref, nxt_ref, gf_ref, gl_ref, wr_ref, br_ref, wg_ref, wu_ref, wd_ref, y_ref,
                x_scr, h_scr, route_scr, acc_scr, *, ec, lead):
    j = pl.program_id(1)
    tm = x_ref.shape[0]

    @pl.when(j == 0)
    def _():
        x_scr[...] = jnp.concatenate([x_ref[lead:, :], nxt_ref[...]], axis=0) if lead else x_ref[...]
        hf = _rms(x_scr[...], gf_ref[...])
        h_hi = hf.astype(BF16)
        h_lo = (hf - h_hi.astype(F32)).astype(BF16)
        h_scr[...] = h_hi
        hh = _dot(h_hi, wr_ref[...])
        logits = hh[:, :LANES] + hh[:, LANES:] + _dot(h_lo, wr_ref[:, :LANES]) + br_ref[...]
        lane = lax.broadcasted_iota(I32, (tm, LANES), 1)
        lanef = lane.astype(F32)
        ninf = -jnp.inf
        isg = (lane >= ROUTE_GROUP_LANE0) & (lane < ROUTE_GROUP_LANE0 + N_EXPERT_GROUPS)
        gl = jnp.where(isg, logits, ninf)
        gmax = jnp.max(gl, axis=1, keepdims=True)
        gsel = jnp.min(jnp.where(gl == gmax, lanef, 1e9), axis=1, keepdims=True)
        gprob = 1.0 / jnp.sum(jnp.exp(gl - gmax), axis=1, keepdims=True)
        e0 = (gsel - ROUTE_GROUP_LANE0) * EXPERTS_PER_GROUP
        el = jnp.where((lanef >= e0) & (lanef < e0 + EXPERTS_PER_GROUP), logits, ninf)
        v1 = jnp.max(el, axis=1, keepdims=True)
        i1 = jnp.min(jnp.where(el == v1, lanef, 1e9), axis=1, keepdims=True)
        el2 = jnp.where(lanef == i1, ninf, el)
        v2 = jnp.max(el2, axis=1, keepdims=True)
        i2 = jnp.min(jnp.where(el2 == v2, lanef, 1e9), axis=1, keepdims=True)
        e = jnp.exp(v2 - v1)
        w1 = (1.0 / (1.0 + e)) * gprob
        w2 = (e / (1.0 + e)) * gprob
        for n, val in enumerate((i1, i2, w1, w2)):
            route_scr[n] = jnp.broadcast_to(val, (tm, LANES))
        acc_scr[...] = jnp.zeros(acc_scr.shape, F32)

    h = h_scr[...]
    i1, i2, w1, w2 = route_scr[0], route_scr[1], route_scr[2], route_scr[3]
    gate = _dot(h, wg_ref[...])
    up = _dot(h, wu_ref[...])
    act = gate * jax.nn.sigmoid(gate) * up
    parts = []
    for q in range(ec):
        ef = jnp.full((tm, LANES), j * ec + q, I32).astype(F32)
        comb = jnp.where(i1 == ef, w1, 0.0) + jnp.where(i2 == ef, w2, 0.0)
        parts.append(act[:, q * D_EXPERT:(q + 1) * D_EXPERT] * comb)
    acc_scr[...] += _dot(jnp.concatenate(parts, axis=1).astype(BF16), wd_ref[...])

    @pl.when(j == pl.num_programs(1) - 1)
    def _():
        y_ref[...] = _rms(x_scr[...] + acc_scr[...], gl_ref[...])


def _moe(x, w, *, tm, n_out, lead, ec=EXPERTS_PER_GROUP):
    bsz, s, d = x.shape
    assert n_out % tm == 0 and n_out + lead <= s
    nt = n_out // tm
    ne = N_EXPERTS // ec
    cw = ec * D_EXPERT
    lb = max(lead, SUBLANES)
    const = lambda a: pl.BlockSpec(a.shape, lambda i, j: (0,) * a.ndim)
    tile = pl.BlockSpec((None, tm, d), lambda i, j: (i // nt, i % nt, 0))
    after = pl.BlockSpec((None, lb, d), lambda i, j: (i // nt, jnp.minimum((i % nt + 1) * (tm // lb), s // lb - 1), 0))
    return pl.pallas_call(
        functools.partial(_moe_kernel, ec=ec, lead=lead),
        grid=(bsz * nt, ne),
        in_specs=[tile, after, const(w["gffn"]), const(w["gfinal"]), const(w["wr"]), const(w["br"]),
                  pl.BlockSpec((d, cw), lambda i, j: (0, j)), pl.BlockSpec((d, cw), lambda i, j: (0, j)),
                  pl.BlockSpec((cw, d), lambda i, j: (j, 0))],
        out_specs=tile,
        out_shape=jax.ShapeDtypeStruct((bsz, n_out, d), F32),
        scratch_shapes=[pltpu.VMEM((tm, d), F32), pltpu.VMEM((tm, d), BF16), pltpu.VMEM((4, tm, LANES), F32),
                        pltpu.VMEM((tm, d), F32)],
        compiler_params=pltpu.CompilerParams(dimension_semantics=("arbitrary", "arbitrary"),
                                             vmem_limit_bytes=VMEM_LIMIT_BYTES),
        name="moe",
    )(x, x, w["gffn"], w["gfinal"], w["wr"], w["br"], w["wg"], w["wu"], w["wd"])


def _rope_freqs():
    inv = lambda half: jnp.exp(-math.log(ROPE_THETA) * jnp.arange(half, dtype=F32) * (1.0 / half))
    used = 2 * QK_HALF + 2 * IX_HALF
    return jnp.concatenate([inv(QK_HALF), inv(QK_HALF), inv(IX_HALF), inv(IX_HALF), jnp.zeros((LANES - used,), F32)])[None]


def _prep_weights(l, norm_mix_g, norm_ffn_g, norm_final_g, w_in, w_pool, pool_scale, w_attn_out, w_out,
                  w_router_group, b_router_group, w_router_expert, b_router_expert,
                  w_expert_gate, w_expert_up, w_expert_down):
    d = w_in.shape[1]
    pw = w_pool.shape[1] * w_pool.shape[2]
    qw = N_HEADS * HEAD_DIM
    kvw = N_KV_HEADS * HEAD_DIM
    widths = (pw, qw, kvw, kvw, N_IDX_HEADS * IDX_DIM, IDX_DIM, N_IDX_HEADS, d, d)
    offs = np.concatenate([[0], np.cumsum(widths)])
    wi = w_in[l]
    piece = lambda a, b: wi[:, offs[a]:offs[b]].astype(BF16)
    wsm = jnp.concatenate([wi[:, offs[5]:offs[7]], jnp.zeros((d, LANES - IDX_DIM - N_IDX_HEADS), F32)], axis=1)
    wr = jnp.concatenate([w_router_expert[l], w_router_group[l],
                          jnp.zeros((d, LANES - N_EXPERTS - N_EXPERT_GROUPS), F32)], axis=1)
    wr_hi = wr.astype(BF16)
    br = jnp.concatenate([b_router_expert[l], b_router_group[l],
                          jnp.zeros((LANES - N_EXPERTS - N_EXPERT_GROUPS,), F32)])[None, :]
    flat_e = lambda a: a.transpose(1, 0, 2).reshape(d, N_EXPERTS * D_EXPERT).astype(BF16)
    return dict(
        gmix=norm_mix_g[l][None, :], gffn=norm_ffn_g[l][None, :], gfinal=norm_final_g[None, :],
        wz=piece(0, 1), wq=piece(1, 2), wkv=piece(2, 4), wqi=piece(4, 5), wsm=wsm.astype(BF16), wgab=piece(7, 9),
        wpool=w_pool[l].astype(BF16), pscale=pool_scale[l][None, :],
        wao=w_attn_out[l].astype(BF16), wout=w_out[l].astype(BF16),
        wr=jnp.concatenate([wr_hi, (wr - wr_hi.astype(F32)).astype(BF16)], axis=1), br=br,
        wg=flat_e(w_expert_gate[l]), wu=flat_e(w_expert_up[l]),
        wd=w_expert_down[l].reshape(N_EXPERTS * D_EXPERT, d).astype(BF16),
    )


def _dup_lanes(ki):
    return jnp.concatenate([ki, ki], axis=-1)


def kernel(x_prompt, x_sample, cache_k, cache_v, cache_idx_k, state_pool, meta_tokens, norm_mix_g, norm_ffn_g,
           norm_final_g, w_in, w_pool, pool_scale, w_attn_out, w_out, w_router_group, b_router_group,
           w_router_expert, b_router_expert, w_expert_gate, w_expert_up, w_expert_down):
    bsz, seq, d = x_prompt.shape
    dbsz, dec, _ = x_sample.shape
    assert w_in.shape[0] == 1, "single-layer step"
    l = 0
    past = cache_k.shape[2] - N_META
    n = N_META + seq
    topk_p = min(IDX_TOPK, seq // 4)
    topk_s = min(IDX_TOPK, (past + dec) // 4)
    kvw = N_KV_HEADS * HEAD_DIM

    tk = ATTN_TK
    tm = tk
    assert seq % tm == 0 and N_META <= tm
    sp = seq + tm
    nt = sp // tm
    tm_moe = _pick_tile(seq, 1100)
    w = _prep_weights(l, norm_mix_g, norm_ffn_g, norm_final_g, w_in, w_pool, pool_scale, w_attn_out, w_out,
                      w_router_group, b_router_group, w_router_expert, b_router_expert,
                      w_expert_gate, w_expert_up, w_expert_down)

    freq = _rope_freqs()
    xp, z, q, k_c, v_c, kb, vt, qis, sm, kid, sga, sgb = _inproj(
        x_prompt, meta_tokens.astype(F32), w["gmix"], freq, w, tm=tm, n_out=n, lead=N_META, pos_base=0, pos_period=0)
    b3 = lambda a: a.reshape(bsz, sp, a.shape[-1])
    o = _attention(b3(qis), b3(sm), b3(kid), b3(q), b3(kb), vt, tq=ATTN_TQ_PROMPT, n_q=seq, lead=N_META,
                   topk=topk_p, causal=True, n_keys=n)
    hs = POOL_STATE + 1
    z4 = z.reshape(bsz, nt, tm, d)
    halo = jnp.concatenate([jnp.zeros((bsz, 1, hs, d), F32), z4[:, :-1, tm - hs:, :]], axis=1).reshape(bsz * nt, hs, d)
    tbase = jnp.asarray(np.tile(np.arange(nt) * tm, bsz), I32)
    x1 = _mix(tbase, o, z, halo, sga, sgb, xp, w, tm=tm, lead=N_META)
    y_prompt = _moe(b3(x1), w, tm=tm_moe, n_out=seq, lead=N_META)
    k_prompt = k_c[None]
    v_prompt = v_c[None]
    idx_k_prompt = b3(sm)[:, :n, :IDX_DIM][None]
    pool_prompt = b3(z)[:, n - POOL_STATE:n][None]

    rows_s = dbsz * dec
    xs = x_sample.reshape(rows_s, d)
    _, zs, qs, ks_c, vs_c, kbs, _, qiss, sms, kids, sgas, sgbs = _inproj(
        xs[None], jnp.zeros((SUBLANES, d), F32), w["gmix"], freq, w, tm=rows_s, n_out=rows_s, lead=0,
        pos_base=N_META + past, pos_period=dec)
    vs = vs_c.reshape(rows_s, kvw)
    nk = N_META + past + dec
    sk = _round_up(nk, tk)
    sq_s = _round_up(dec, ATTN_TQ)
    s3 = lambda a: a.reshape(dbsz, dec, a.shape[-1])
    padr = lambda a, rows: jnp.concatenate([a, jnp.zeros((dbsz, rows - a.shape[1], a.shape[-1]), a.dtype)], axis=1)
    k_all, vt_all = _sample_keys(cache_k[l], cache_v[l], s3(kbs), s3(vs), tk=tk)
    assert k_all.shape[1] == sk
    ki_all = padr(jnp.concatenate([_dup_lanes(cache_idx_k[l]).astype(BF16), s3(kids)], axis=1), sk)
    os_ = _attention(padr(s3(qiss), sq_s), padr(s3(sms), sq_s), ki_all, padr(s3(qs), sq_s), k_all, vt_all,
                     tq=ATTN_TQ, n_q=sq_s, lead=0, topk=topk_s, causal=False, n_keys=nk)[:, :dec]
    zcat = jnp.concatenate([state_pool[l].astype(F32), s3(zs)], axis=1)
    halo_s = jnp.concatenate([jnp.zeros((dbsz, 1, d), F32), zcat[:, :POOL_STATE]], axis=1)
    tbase_s = jnp.full((dbsz,), POOL_STATE, I32)
    x1s = _mix(tbase_s, os_, zs, halo_s, sgas, sgbs, xs, w, tm=dec, lead=0)
    y_sample = _moe(x1s[None], w, tm=rows_s, n_out=rows_s, lead=0).reshape(dbsz, dec, d)
    k_sample = ks_c.reshape(1, dbsz, dec, N_KV_HEADS, HEAD_DIM)
    v_sample = vs_c.reshape(1, dbsz, dec, N_KV_HEADS, HEAD_DIM)
    idx_k_sample = s3(sms)[:, :, :IDX_DIM][None]
    pool_sample = zcat[:, zcat.shape[1] - POOL_STATE:][None]

    return (y_prompt, y_sample, k_prompt, v_prompt, idx_k_prompt, pool_prompt,
            k_sample, v_sample, idx_k_sample, pool_sample)
```

```python
import functools
import math

import numpy as np
import jax
import jax.numpy as jnp
from jax import lax
from jax.experimental import pallas as pl
from jax.experimental.pallas import tpu as pltpu

CHUNK = 64
N_META = 16
EPS = 1e-6
POOL_WINDOWS = (2, 4, 8, 16)
POOL_STATE = max(POOL_WINDOWS) - 1
N_HEADS = 8
N_KV_HEADS = 2
HEAD_DIM = 128
KV_REP = N_HEADS // N_KV_HEADS
N_IDX_HEADS = 4
IDX_DIM = 64
IDX_TOPK = 256
IDX_W_SCALE = (N_IDX_HEADS * IDX_DIM) ** -0.5
ATTN_SCALE = HEAD_DIM ** -0.5
Q_PRESCALE = ATTN_SCALE * math.log2(math.e)
ROPE_THETA = 500000.0
ROT_FRACTION = 4
N_EXPERT_GROUPS = 4
EXPERTS_PER_GROUP = 8
N_EXPERTS = N_EXPERT_GROUPS * EXPERTS_PER_GROUP
D_EXPERT = 128

LANES = 128
SUBLANES = 8
BF16_SUBLANES = 16
VMEM_LIMIT_BYTES = 56 * 1024 * 1024

QK_HALF = HEAD_DIM // ROT_FRACTION // 2
IX_HALF = IDX_DIM // ROT_FRACTION // 2
TAB_COS_QK = 0
TAB_SIN_QK = QK_HALF
TAB_COS_IX = 2 * QK_HALF
TAB_SIN_IX = 2 * QK_HALF + IX_HALF

ATTN_TQ = LANES
ATTN_TQ_PROMPT = 2 * LANES
ATTN_TK = 512
TIE_BLOCK = 256
F32_TINY = float(np.finfo(np.float32).tiny)
MAX_PROBES = 64
NEG_BIG = -1e30

F32 = jnp.float32
BF16 = jnp.bfloat16
I32 = jnp.int32


def _round_up(x, m):
    return -(-x // m) * m


def _pick_tile(rows, cap):
    return max(t for t in range(BF16_SUBLANES, cap + 1, BF16_SUBLANES) if rows % t == 0)


def _rms(x, g):
    return x * lax.rsqrt(jnp.mean(x * x, axis=-1, keepdims=True) + EPS) * g


def _dot(a, b):
    return jnp.dot(a, b, preferred_element_type=F32)


def _dot_nt(a, b):
    return lax.dot_general(a, b, (((1,), (1,)), ((), ())), preferred_element_type=F32)


def _rope(y, c, a, b, shift_a, shift_b):
    outs = []
    for j in range(y.shape[1] // LANES):
        ys = y[:, j * LANES:(j + 1) * LANES]
        outs.append(ys * c + pltpu.roll(ys, shift_a, 1) * a + pltpu.roll(ys, shift_b, 1) * b)
    return outs[0] if len(outs) == 1 else jnp.concatenate(outs, axis=1)


def _inproj_kernel(x_ref, prev_ref, meta_ref, g_ref, freq_ref, wz_ref, wq_ref, wkv_ref, wqi_ref, wsm_ref, wgab_ref,
                   xs_ref, z_ref, q_ref, k4_ref, v4_ref, kb_ref, vt_ref, qis_ref, sm_ref, kid_ref, sga_ref, sgb_ref,
                   *, lead, frame_tiles, pos_base, pos_period):
    tm, d = x_ref.shape
    ti = pl.program_id(0) % (frame_tiles + 1 if lead else frame_tiles)
    if lead:
        head = jnp.where(ti == 0, meta_ref[...], prev_ref[...])
        body = jnp.where(ti < frame_tiles, x_ref[0:tm - lead, :], 0.0)
        x = jnp.concatenate([head, body], axis=0)
    else:
        x = x_ref[...]
    xs_ref[...] = x
    h = _rms(x, g_ref[...]).astype(BF16)
    lane = lax.broadcasted_iota(I32, (tm, LANES), 1)
    row = lax.broadcasted_iota(I32, (tm, LANES), 0)
    pos = pos_base + (lax.rem(row, pos_period) if pos_period else ti * tm + row)
    ang = pos.astype(F32) * freq_ref[...]
    is_cos = (lane < TAB_SIN_QK) | ((lane >= TAB_COS_IX) & (lane < TAB_SIN_IX))
    t = jnp.where(is_cos, jnp.cos(ang), jnp.sin(ang))

    def rl(s):
        return pltpu.roll(t, s % LANES, 1) if s % LANES else t

    c_qk = jnp.where(lane < QK_HALF, rl(-TAB_COS_QK), jnp.where(lane < 2 * QK_HALF, rl(QK_HALF - TAB_COS_QK), 1.0))
    a_qk = jnp.where(lane < QK_HALF, -rl(-TAB_SIN_QK), 0.0)
    b_qk = jnp.where((lane >= QK_HALF) & (lane < 2 * QK_HALF), rl(QK_HALF - TAB_SIN_QK), 0.0)
    l64 = lane & (IDX_DIM - 1)
    hi = lane >= IDX_DIM
    cos_lo = jnp.where(hi, rl(IDX_DIM - TAB_COS_IX), rl(-TAB_COS_IX))
    cos_hi = jnp.where(hi, rl(IDX_DIM + IX_HALF - TAB_COS_IX), rl(IX_HALF - TAB_COS_IX))
    sin_lo = jnp.where(hi, rl(IDX_DIM - TAB_SIN_IX), rl(-TAB_SIN_IX))
    sin_hi = jnp.where(hi, rl(IDX_DIM + IX_HALF - TAB_SIN_IX), rl(IX_HALF - TAB_SIN_IX))
    c_ix = jnp.where(l64 < IX_HALF, cos_lo, jnp.where(l64 < 2 * IX_HALF, cos_hi, 1.0))
    a_ix = jnp.where(l64 < IX_HALF, -sin_lo, 0.0)
    b_ix = jnp.where((l64 >= IX_HALF) & (l64 < 2 * IX_HALF), sin_hi, 0.0)

    z_ref[...] = _dot(h, wz_ref[...])

    q = _rope(_dot(h, wq_ref[...]), c_qk, a_qk, b_qk, LANES - QK_HALF, QK_HALF)
    q_ref[...] = (q * Q_PRESCALE).astype(BF16)

    kv = _dot(h, wkv_ref[...])
    kvw = kv.shape[1] // 2
    k = _rope(kv[:, :kvw], c_qk, a_qk, b_qk, LANES - QK_HALF, QK_HALF)
    v = kv[:, kvw:]
    for g in range(N_KV_HEADS):
        k4_ref[:, g, :] = k[:, g * HEAD_DIM:(g + 1) * HEAD_DIM]
        v4_ref[:, g, :] = v[:, g * HEAD_DIM:(g + 1) * HEAD_DIM]
    kb_ref[...] = k.astype(BF16)
    vt_ref[...] = v.T.astype(BF16)

    qi = _rope(_dot(h, wqi_ref[...]), c_ix, a_ix, b_ix, LANES - IX_HALF, IX_HALF)
    pieces = []
    for hd in range(N_IDX_HEADS):
        col = qi[:, (hd // 2) * LANES:(hd // 2 + 1) * LANES]
        keep = hi if hd % 2 else jnp.logical_not(hi)
        pieces.append(jnp.where(keep, col, 0.0))
    qis_ref[...] = jnp.concatenate(pieces, axis=1).astype(BF16)

    c_sm = jnp.where(hi, IDX_W_SCALE, c_ix)
    a_sm = jnp.where(hi, 0.0, a_ix)
    b_sm = jnp.where(hi, 0.0, b_ix)
    sm = _rope(_dot(h, wsm_ref[...]), c_sm, a_sm, b_sm, LANES - IX_HALF, IX_HALF)
    sm_ref[...] = sm
    kid_ref[...] = jnp.where(hi, pltpu.roll(sm, IDX_DIM, 1), sm).astype(BF16)

    gab = _dot(h, wgab_ref[...])
    sga_ref[...] = jax.nn.sigmoid(gab[:, :d])
    sgb_ref[...] = jax.nn.sigmoid(gab[:, d:])


def _inproj(x, meta, g, freq, w, *, tm, n_out, lead, pos_base, pos_period):
    bsz, s, d = x.shape
    assert s % tm == 0
    ft = s // tm
    nt = ft + 1 if lead else ft
    rows = bsz * nt * tm
    qw = w["wq"].shape[1]
    kvw = w["wkv"].shape[1] // 2
    lb = max(lead, SUBLANES)
    row_spec = lambda n: pl.BlockSpec((tm, n), lambda i: (i, 0))
    const = lambda a: pl.BlockSpec(a.shape, lambda i: (0,) * a.ndim)
    cache_spec = pl.BlockSpec((None, tm, N_KV_HEADS, HEAD_DIM), lambda i: (i // nt, i % nt, 0, 0))
    cache_shape = jax.ShapeDtypeStruct((bsz, n_out, N_KV_HEADS, HEAD_DIM), F32)
    outs = [
        (d, F32),
        (d, F32),
        (qw, BF16),
        "cache", "cache",
        (kvw, BF16),
        "vt",
        (N_IDX_HEADS * LANES, BF16),
        (LANES, F32),
        (LANES, BF16),
        (d, F32), (d, F32),
    ]
    spec_of = lambda o: (cache_spec if o == "cache" else pl.BlockSpec((kvw, tm), lambda i: (0, i)) if o == "vt"
                         else row_spec(o[0]))
    shape_of = lambda o: (cache_shape if o == "cache" else jax.ShapeDtypeStruct((kvw, rows), BF16) if o == "vt"
                          else jax.ShapeDtypeStruct((rows, o[0]), o[1]))
    return pl.pallas_call(
        functools.partial(_inproj_kernel, lead=lead, frame_tiles=ft, pos_base=pos_base, pos_period=pos_period),
        grid=(bsz * nt,),
        in_specs=[pl.BlockSpec((None, tm, d), lambda i: (i // nt, jnp.minimum(i % nt, ft - 1), 0)),
                  pl.BlockSpec((None, lb, d), lambda i: (i // nt, jnp.maximum((i % nt) * (tm // lb) - 1, 0), 0)),
                  const(meta), const(g), const(freq),
                  const(w["wz"]), const(w["wq"]), const(w["wkv"]), const(w["wqi"]), const(w["wsm"]), const(w["wgab"])],
        out_specs=[spec_of(o) for o in outs],
        out_shape=[shape_of(o) for o in outs],
        compiler_params=pltpu.CompilerParams(dimension_semantics=("arbitrary",), vmem_limit_bytes=VMEM_LIMIT_BYTES),
        name="inproj",
    )(x, x, meta, g, freq, w["wz"], w["wq"], w["wkv"], w["wqi"], w["wsm"], w["wgab"])


def _row_fold(x, op, group=SUBLANES):
    ways = 4
    accs = []
    for j in range(x.shape[0] // group):
        part = x[j * group:(j + 1) * group]
        if j < ways:
            accs.append(part)
        else:
            accs[j % ways] = op(accs[j % ways], part)
    while len(accs) > 1:
        accs = [op(accs[j], accs[j + 1]) for j in range(0, len(accs) - 1, 2)] + accs[len(accs) & ~1:]
    return accs[0]


def _attn_kernel(nch_ref, qis_ref, qis_nx_ref, sm_ref, sm_nx_ref, kid_ref, q_ref, q_nx_ref, k_ref, vt_ref, tri_ref,
                 o_ref, sc_scr, acc_scr, s_scr, *, tq, tk, topk, causal, n_keys, lead):
    i = pl.program_id(1)
    nch = nch_ref[i]
    sub = tri_ref.shape[0]
    nq4 = KV_REP * tq
    rows = lambda ref, nx: jnp.concatenate([ref[lead:, :], nx[...]], axis=0) if lead else ref[...]
    qis, sm, q = rows(qis_ref, qis_nx_ref), rows(sm_ref, sm_nx_ref), rows(q_ref, q_nx_ref)
    qpos = lead + i * tq + lax.broadcasted_iota(I32, (1, tq), 1)
    if causal:
        lim = N_META + CHUNK * (jnp.right_shift(qpos - N_META, int(math.log2(CHUNK))) + 1)
        lim = jnp.minimum(lim, n_keys)
    else:
        lim = jnp.full((1, tq), n_keys, I32)

    qstack = jnp.concatenate([qis[:, hd * LANES:(hd + 1) * LANES] for hd in range(N_IDX_HEADS)], axis=0)
    smt = sm.T
    wlane = jnp.concatenate([smt[IDX_DIM + hd:IDX_DIM + hd + 1, :] for hd in range(N_IDX_HEADS)], axis=1)

    key_row = lax.broadcasted_iota(I32, (tk, tq), 0)

    def score_body(c2, st):
        return score_chunk(jnp.minimum(2 * c2 + 1, nch - 1), score_chunk(2 * c2, st))

    def score_chunk(c, st):
        smax, smin = st
        off = pl.multiple_of(c * tk, tk)
        sc = _dot_nt(kid_ref[pl.ds(off, tk), :], qstack)
        sc = jnp.maximum(sc, 0.0) * wlane
        s = sc[:, 0:tq]
        for hd in range(1, N_IDX_HEADS):
            s = s + sc[:, hd * tq:(hd + 1) * tq]
        ok = key_row < lim - off
        masked = jnp.where(ok, s, -jnp.inf)
        sc_scr[pl.ds(off, tk), :] = masked
        return (jnp.maximum(smax, _row_fold(masked, jnp.maximum)),
                jnp.minimum(smin, _row_fold(jnp.where(ok, s, jnp.inf), jnp.minimum)))

    smax, smin = lax.fori_loop(0, (nch + 1) // 2, score_body, (jnp.full((SUBLANES, tq), -jnp.inf, F32),
                                                    jnp.full((SUBLANES, tq), jnp.inf, F32)))
    smax = jnp.max(smax, axis=0, keepdims=True)
    smin = jnp.min(smin, axis=0, keepdims=True)

    def count_ge(cand):
        def hits(c, cnd):
            kb = sc_scr[pl.ds(pl.multiple_of(c * tk, tk), tk), :]
            return _row_fold(jnp.where(kb >= cnd, 1.0, 0.0), jnp.add)

        def body(c2, acc):
            second = 2 * c2 + 1
            cnd2 = jnp.where(second < nch, cand, jnp.inf)
            return acc + (hits(2 * c2, cand) + hits(jnp.minimum(second, nch - 1), cnd2))
        acc = lax.fori_loop(0, (nch + 1) // 2, body, jnp.zeros((SUBLANES, tq), F32))
        return jnp.sum(acc, axis=0, keepdims=True)

    take_all = lim <= topk
    kf = float(topk)
    lo0 = jnp.where(take_all, NEG_BIG, smin)
    hi0 = jnp.where(take_all, NEG_BIG, smax + jnp.abs(smax) * 2.0 ** -10 + F32_TINY * 2.0 ** 24)
    c_lo0 = lim.astype(F32)
    c_hi0 = jnp.where(take_all, kf, 0.0)
    closed0 = jnp.where(take_all, 1.0, 0.0)

    def search_cond(st):
        return (st[0] < MAX_PROBES) & (jnp.max(1.0 - st[5]) > 0.5)

    def search_body(st):
        return search_step(search_step(st))

    def search_step(st):
        it, lo, hi, c_lo, c_hi, closed = st
        mid = lo + 0.5 * (hi - lo)
        interp = hi - (hi - lo) * ((kf - c_hi) / jnp.maximum(c_lo - c_hi, 1.0))
        inside = lambda v: (v > lo) & (v < hi)
        cand = jnp.where((it & 1) == 0, interp, mid)
        cand = jnp.where((it == 0) & inside(0.0), 0.0, cand)
        cand = jnp.where((it == 1) & inside(F32_TINY), F32_TINY, cand)
        cand = jnp.where(inside(cand), cand, mid)
        cnt = count_ge(cand)
        is_open = closed < 0.5
        up = is_open & (cnt > kf)
        down = is_open & (cnt <= kf)
        lo = jnp.where(up, cand, lo)
        c_lo = jnp.where(up, cnt, c_lo)
        hi = jnp.where(down, cand, hi)
        c_hi = jnp.where(down, cnt, c_hi)
        done = (c_hi == kf) | jnp.logical_not(inside(lo + 0.5 * (hi - lo))) | ((lo == 0.0) & (hi == F32_TINY))
        return it + 1, lo, hi, c_lo, c_hi, jnp.where(done, 1.0, closed)

    _, thr_lo, thr_hi, _, c_hi, _ = lax.while_loop(search_cond, search_body,
                                                   (jnp.int32(0), lo0, hi0, c_lo0, c_hi0, closed0))
    n_tie = kf - c_hi

    acc_scr[...] = jnp.zeros(acc_scr.shape, F32)
    halves = tq // LANES
    eye = jnp.where((lax.broadcasted_iota(I32, (KV_REP * LANES, LANES), 0) & (LANES - 1))
                    == lax.broadcasted_iota(I32, (KV_REP * LANES, LANES), 1), 1.0, 0.0).astype(BF16)
    qgs = [[jnp.concatenate(
        [jnp.concatenate([q[hf * LANES:(hf + 1) * LANES, (g * KV_REP + r) * HEAD_DIM:(g * KV_REP + r + 1) * HEAD_DIM]
                          for r in range(KV_REP)], axis=0), eye], axis=1)
        for hf in range(halves)] for g in range(N_KV_HEADS)]

    ones_rows = jnp.ones((BF16_SUBLANES, tk), BF16)

    def logits(c, slot, seen):
        off = pl.multiple_of(c * tk, tk)
        kb = sc_scr[pl.ds(off, tk), :]
        above = kb >= thr_hi
        at_least = kb >= thr_lo
        eqf = jnp.where(at_least, 1.0, 0.0) - jnp.where(above, 1.0, 0.0)
        ranks = []
        for hf in range(tk // sub):
            e = eqf[hf * sub:(hf + 1) * sub]
            ranks.append(seen + _dot(tri_ref[...], e.astype(BF16)))
            seen = seen + jnp.sum(_row_fold(e, jnp.add), axis=0, keepdims=True)
        rank = jnp.concatenate(ranks, axis=0)
        maskf = jnp.where(above, 0.0, jnp.where(at_least, jnp.where(rank < n_tie, 0.0, NEG_BIG), NEG_BIG))
        maskb = maskf.astype(BF16)
        mcs = []
        for g in range(N_KV_HEADS):
            kc = k_ref[pl.ds(off, tk), g * HEAD_DIM:(g + 1) * HEAD_DIM]
            s = jnp.concatenate(
                [_dot_nt(jnp.concatenate([kc, maskb[:, hf * LANES:(hf + 1) * LANES]], axis=1), qgs[g][hf])
                 for hf in range(halves)], axis=1)
            s_scr[slot, g] = s
            mcs.append(jnp.max(_row_fold(s, jnp.maximum), axis=0, keepdims=True))
        return seen, tuple(mcs)

    def update(c, slot, ms, mcs):
        off = pl.multiple_of(c * tk, tk)
        ms_n = []
        for g in range(N_KV_HEADS):
            vtc = vt_ref[g * HEAD_DIM:(g + 1) * HEAD_DIM, pl.ds(off, tk)]
            m_new = jnp.maximum(ms[g], mcs[g])
            alpha = jnp.exp2(ms[g] - m_new)
            p = jnp.exp2(s_scr[slot, g] - m_new).astype(BF16)
            acc_scr[g] = alpha * acc_scr[g] + _dot(jnp.concatenate([vtc, ones_rows], axis=0), p)
            ms_n.append(m_new)
        return tuple(ms_n)

    def flash_body(c, st):
        seen, ms, mcs = st
        slot = c & 1
        ms = update(c, slot, ms, mcs)
        seen, mcs = logits(c + 1, 1 - slot, seen)
        return seen, ms, mcs

    m0 = tuple(jnp.full((1, nq4), NEG_BIG, F32) for _ in range(N_KV_HEADS))
    seen0, mc0 = logits(0, 0, jnp.zeros((1, tq), F32))
    _, ms, mcs = lax.fori_loop(0, nch - 1, flash_body, (seen0, m0, mc0))
    update(nch - 1, (nch - 1) & 1, ms, mcs)

    for g in range(N_KV_HEADS):
        og = acc_scr[g, 0:HEAD_DIM] / acc_scr[g, HEAD_DIM:HEAD_DIM + 1]
        for hf in range(halves):
            for r in range(KV_REP):
                hd = g * KV_REP + r
                col = (hf * KV_REP + r) * LANES
                o_ref[hf * LANES:(hf + 1) * LANES, hd * HEAD_DIM:(hd + 1) * HEAD_DIM] = (
                    og[:, col:col + LANES].T.astype(o_ref.dtype))


def _attention(qis, sm, kid, q, kb, vt, *, tq, n_q, lead, topk, causal, n_keys):
    tk, sub = ATTN_TK, TIE_BLOCK
    bsz, sq, _ = q.shape
    sk = kb.shape[1]
    assert n_q % tq == 0 and sk % tk == 0 and n_q + lead <= sq
    nq = n_q // tq
    if causal:
        ends = np.minimum(N_META + CHUNK * ((lead + np.arange(nq) * tq + tq - 1 - N_META) // CHUNK + 1), n_keys)
    else:
        ends = np.full((nq,), n_keys)
    nch = jnp.asarray(-(-ends // tk), I32)
    tri = jnp.asarray(np.tril(np.ones((sub, sub), np.float32), -1), BF16)
    lb = max(lead, BF16_SUBLANES)
    qrow = lambda n: pl.BlockSpec((None, tq, n), lambda b, i, s: (b, i, 0))
    qnext = lambda n: pl.BlockSpec((None, lb, n), lambda b, i, s: (b, jnp.minimum((i + 1) * (tq // lb), sq // lb - 1), 0))
    krow = lambda n: pl.BlockSpec((None, sk, n), lambda b, i, s: (b, 0, 0), pipeline_mode=pl.Buffered(1))
    kern = functools.partial(_attn_kernel, tq=tq, tk=tk, topk=topk, causal=causal, n_keys=n_keys, lead=lead)
    return pl.pallas_call(
        kern,
        grid_spec=pltpu.PrefetchScalarGridSpec(
            num_scalar_prefetch=1,
            grid=(bsz, nq),
            in_specs=[qrow(qis.shape[2]), qnext(qis.shape[2]), qrow(sm.shape[2]), qnext(sm.shape[2]),
                      krow(kid.shape[2]), qrow(q.shape[2]), qnext(q.shape[2]),
                      krow(kb.shape[2]),
                      pl.BlockSpec((vt.shape[0], sk), lambda b, i, s: (0, b), pipeline_mode=pl.Buffered(1)),
                      pl.BlockSpec((sub, sub), lambda b, i, s: (0, 0))],
            out_specs=qrow(q.shape[2]),
            scratch_shapes=[pltpu.VMEM((sk, tq), F32),
                            pltpu.VMEM((N_KV_HEADS, HEAD_DIM + BF16_SUBLANES, KV_REP * tq), F32),
                            pltpu.VMEM((2, N_KV_HEADS, tk, KV_REP * tq), F32)]),
        out_shape=jax.ShapeDtypeStruct((bsz, n_q, q.shape[2]), BF16),
        compiler_params=pltpu.CompilerParams(dimension_semantics=("arbitrary", "arbitrary"),
                                             vmem_limit_bytes=VMEM_LIMIT_BYTES),
        name="attn",
    )(nch, qis, qis, sm, sm, kid, q, q, kb, vt, tri)


def _keys_kernel(ck_ref, ckt_ref, cv_ref, cvt_ref, kn_ref, vn_ref, k_ref, vt_ref, *, full_tiles):
    t = pl.program_id(1)
    tk = k_ref.shape[0]
    flat = lambda ref: jnp.concatenate(
        [ref[pl.ds(g, ref.shape[0] // N_KV_HEADS, stride=N_KV_HEADS), :] for g in range(N_KV_HEADS)], axis=1)

    def emit(k, v):
        k_ref[...] = k.astype(BF16)
        vt_ref[...] = v.T.astype(BF16)

    @pl.when(t < full_tiles)
    def _():
        emit(flat(ck_ref), flat(cv_ref))

    @pl.when(t == full_tiles)
    def _():
        def tile(tail_ref, new_ref):
            tail, new = flat(tail_ref), new_ref[...].astype(F32)
            rest = jnp.zeros((tk - tail.shape[0] - new.shape[0], tail.shape[1]), F32)
            return jnp.concatenate([tail, new, rest], axis=0)
        emit(tile(ckt_ref, kn_ref), tile(cvt_ref, vn_ref))


def _sample_keys(cache_k, cache_v, k_new, v_new, *, tk):
    bsz, nc, nh, hd = cache_k.shape
    dec = k_new.shape[1]
    full, tail = nc // tk, nc % tk
    assert tail % BF16_SUBLANES == 0 and tail > 0 and tk % tail == 0 and tail + dec <= tk
    nt = full + 1
    kvw = nh * hd
    cache_k, cache_v = (c.reshape(bsz, nc * nh, hd) for c in (cache_k, cache_v))
    main = pl.BlockSpec((None, tk * nh, hd), lambda b, t: (b, jnp.minimum(t, full - 1), 0))
    last = pl.BlockSpec((None, tail * nh, hd), lambda b, t: (b, full * tk // tail, 0))
    new = pl.BlockSpec((None, dec, kvw), lambda b, t: (b, 0, 0))
    return pl.pallas_call(
        functools.partial(_keys_kernel, full_tiles=full),
        grid=(bsz, nt),
        in_specs=[main, last, main, last, new, new],
        out_specs=[pl.BlockSpec((None, tk, kvw), lambda b, t: (b, t, 0)),
                   pl.BlockSpec((kvw, tk), lambda b, t: (0, b * nt + t))],
        out_shape=[jax.ShapeDtypeStruct((bsz, nt * tk, kvw), BF16), jax.ShapeDtypeStruct((kvw, bsz * nt * tk), BF16)],
        compiler_params=pltpu.CompilerParams(dimension_semantics=("arbitrary", "arbitrary"),
                                             vmem_limit_bytes=VMEM_LIMIT_BYTES),
        name="keys",
    )(cache_k, cache_k, cache_v, cache_v, k_new, v_new)


def _mix_kernel(tb_ref, o_ref, oprev_ref, z_ref, halo_ref, sga_ref, sgb_ref, x_ref, wpool_ref, ps_ref, wao_ref,
                wout_ref, x1_ref, zc_scr, *, tm, lead, tiles):
    if lead:
        head = jnp.where(pl.program_id(0) % tiles == 0, jnp.zeros_like(oprev_ref[...]), oprev_ref[...])
        o = jnp.concatenate([head, o_ref[0:tm - lead, :]], axis=0)
    else:
        o = o_ref[...]
    hs = halo_ref.shape[0]
    zc_scr[0:hs, :] = halo_ref[...]
    zc_scr[hs:hs + tm, :] = z_ref[...]
    avail = tb_ref[pl.program_id(0)] + lax.broadcasted_iota(I32, (tm, 1), 0) + 1
    gw = z_ref.shape[1] // len(POOL_WINDOWS)
    ys = []
    for g, win in enumerate(POOL_WINDOWS):
        cols = slice(g * gw, (g + 1) * gw)
        zt = zc_scr[hs:hs + tm, cols]
        s = zt
        for j in range(1, win):
            s = s + zc_scr[hs - j:hs - j + tm, cols]
        cnt = jnp.minimum(avail, win).astype(F32)
        dlt = s / cnt - zt
        ys.append(_dot(dlt.astype(BF16), wpool_ref[g]))
    a = jnp.concatenate(ys, axis=1) * ps_ref[...]
    b = _dot(o, wao_ref[...])
    mrg = sga_ref[...] * a + sgb_ref[...] * b
    x1_ref[...] = x_ref[...] + _dot(mrg.astype(BF16), wout_ref[...])


def _mix(tbase, o, z, halo, sga, sgb, x, w, *, tm, lead):
    rows, d = x.shape
    bsz, so, _ = o.shape
    nt = rows // tm
    tiles = nt // bsz
    ot = so // tm
    hs = halo.shape[1]
    lb = max(lead, BF16_SUBLANES)
    row = lambda n: pl.BlockSpec((tm, n), lambda i, s: (i, 0))
    const = lambda a: pl.BlockSpec(a.shape, lambda i, s: (0,) * a.ndim)
    o_cur = pl.BlockSpec((None, tm, d), lambda i, s: (i // tiles, jnp.minimum(i % tiles, ot - 1), 0))
    o_prev = pl.BlockSpec((None, lb, d), lambda i, s: (i // tiles, jnp.maximum((i % tiles) * (tm // lb) - 1, 0), 0))
    return pl.pallas_call(
        functools.partial(_mix_kernel, tm=tm, lead=lead, tiles=tiles),
        grid_spec=pltpu.PrefetchScalarGridSpec(
            num_scalar_prefetch=1,
            grid=(nt,),
            in_specs=[o_cur, o_prev, row(d), pl.BlockSpec((None, hs, d), lambda i, s: (i, 0, 0)), row(d), row(d), row(d),
                      const(w["wpool"]), const(w["pscale"]), const(w["wao"]), const(w["wout"])],
            out_specs=row(d),
            scratch_shapes=[pltpu.VMEM((hs + tm, d), F32)]),
        out_shape=jax.ShapeDtypeStruct((rows, d), F32),
        compiler_params=pltpu.CompilerParams(dimension_semantics=("arbitrary",), vmem_limit_bytes=VMEM_LIMIT_BYTES),
        name="mix",
    )(tbase, o, o, z, halo, sga, sgb, x, w["wpool"], w["pscale"], w["wao"], w["wout"])


ROUTE_GROUP_LANE0 = N_EXPERTS


def _moe_kernel(x_ref, nxt_ref, gf_ref, gl_ref, wr_ref, br_ref, wg_ref, wu_ref, wd_ref, y_ref,
                x_scr, h_scr, route_scr, acc_scr, *, ec, lead):
    j = pl.program_id(1)
    tm = x_ref.shape[0]

    @pl.when(j == 0)
    def _():
        x_scr[...] = jnp.concatenate([x_ref[lead:, :], nxt_ref[...]], axis=0) if lead else x_ref[...]
        hf = _rms(x_scr[...], gf_ref[...])
        h_hi = hf.astype(BF16)
        h_lo = (hf - h_hi.astype(F32)).astype(BF16)
        h_scr[...] = h_hi
        hh = _dot(h_hi, wr_ref[...])
        logits = hh[:, :LANES] + hh[:, LANES:] + _dot(h_lo, wr_ref[:, :LANES]) + br_ref[...]
        lane = lax.broadcasted_iota(I32, (tm, LANES), 1)
        lanef = lane.astype(F32)
        ninf = -jnp.inf
        isg = (lane >= ROUTE_GROUP_LANE0) & (lane < ROUTE_GROUP_LANE0 + N_EXPERT_GROUPS)
        gl = jnp.where(isg, logits, ninf)
        gmax = jnp.max(gl, axis=1, keepdims=True)
        gsel = jnp.min(jnp.where(gl == gmax, lanef, 1e9), axis=1, keepdims=True)
        gprob = 1.0 / jnp.sum(jnp.exp(gl - gmax), axis=1, keepdims=True)
        e0 = (gsel - ROUTE_GROUP_LANE0) * EXPERTS_PER_GROUP
        el = jnp.where((lanef >= e0) & (lanef < e0 + EXPERTS_PER_GROUP), logits, ninf)
        v1 = jnp.max(el, axis=1, keepdims=True)
        i1 = jnp.min(jnp.where(el == v1, lanef, 1e9), axis=1, keepdims=True)
        el2 = jnp.where(lanef == i1, ninf, el)
        v2 = jnp.max(el2, axis=1, keepdims=True)
        i2 = jnp.min(jnp.where(el2 == v2, lanef, 1e9), axis=1, keepdims=True)
        e = jnp.exp(v2 - v1)
        w1 = (1.0 / (1.0 + e)) * gprob
        w2 = (e / (1.0 + e)) * gprob
        for n, val in enumerate((i1, i2, w1, w2)):
            route_scr[n] = jnp.broadcast_to(val, (tm, LANES))
        acc_scr[...] = jnp.zeros(acc_scr.shape, F32)

    h = h_scr[...]
    i1, i2, w1, w2 = route_scr[0], route_scr[1], route_scr[2], route_scr[3]
    gate = _dot(h, wg_ref[...])
    up = _dot(h, wu_ref[...])
    act = gate * jax.nn.sigmoid(gate) * up
    parts = []
    for q in range(ec):
        ef = jnp.full((tm, LANES), j * ec + q, I32).astype(F32)
        comb = jnp.where(i1 == ef, w1, 0.0) + jnp.where(i2 == ef, w2, 0.0)
        parts.append(act[:, q * D_EXPERT:(q + 1) * D_EXPERT] * comb)
    acc_scr[...] += _dot(jnp.concatenate(parts, axis=1).astype(BF16), wd_ref[...])

    @pl.when(j == pl.num_programs(1) - 1)
    def _():
        y_ref[...] = _rms(x_scr[...] + acc_scr[...], gl_ref[...])


def _moe(x, w, *, tm, n_out, lead, ec=EXPERTS_PER_GROUP):
    bsz, s, d = x.shape
    assert n_out % tm == 0 and n_out + lead <= s
    nt = n_out // tm
    ne = N_EXPERTS // ec
    cw = ec * D_EXPERT
    lb = max(lead, SUBLANES)
    const = lambda a: pl.BlockSpec(a.shape, lambda i, j: (0,) * a.ndim)
    tile = pl.BlockSpec((None, tm, d), lambda i, j: (i // nt, i % nt, 0))
    after = pl.BlockSpec((None, lb, d), lambda i, j: (i // nt, jnp.minimum((i % nt + 1) * (tm // lb), s // lb - 1), 0))
    return pl.pallas_call(
        functools.partial(_moe_kernel, ec=ec, lead=lead),
        grid=(bsz * nt, ne),
        in_specs=[tile, after, const(w["gffn"]), const(w["gfinal"]), const(w["wr"]), const(w["br"]),
                  pl.BlockSpec((d, cw), lambda i, j: (0, j)), pl.BlockSpec((d, cw), lambda i, j: (0, j)),
                  pl.BlockSpec((cw, d), lambda i, j: (j, 0))],
        out_specs=tile,
        out_shape=jax.ShapeDtypeStruct((bsz, n_out, d), F32),
        scratch_shapes=[pltpu.VMEM((tm, d), F32), pltpu.VMEM((tm, d), BF16), pltpu.VMEM((4, tm, LANES), F32),
                        pltpu.VMEM((tm, d), F32)],
        compiler_params=pltpu.CompilerParams(dimension_semantics=("arbitrary", "arbitrary"),
                                             vmem_limit_bytes=VMEM_LIMIT_BYTES),
        name="moe",
    )(x, x, w["gffn"], w["gfinal"], w["wr"], w["br"], w["wg"], w["wu"], w["wd"])


def _rope_freqs():
    inv = lambda half: jnp.exp(-math.log(ROPE_THETA) * jnp.arange(half, dtype=F32) * (1.0 / half))
    used = 2 * QK_HALF + 2 * IX_HALF
    return jnp.concatenate([inv(QK_HALF), inv(QK_HALF), inv(IX_HALF), inv(IX_HALF), jnp.zeros((LANES - used,), F32)])[None]


def _prep_weights(l, norm_mix_g, norm_ffn_g, norm_final_g, w_in, w_pool, pool_scale, w_attn_out, w_out,
                  w_router_group, b_router_group, w_router_expert, b_router_expert,
                  w_expert_gate, w_expert_up, w_expert_down):
    d = w_in.shape[1]
    pw = w_pool.shape[1] * w_pool.shape[2]
    qw = N_HEADS * HEAD_DIM
    kvw = N_KV_HEADS * HEAD_DIM
    widths = (pw, qw, kvw, kvw, N_IDX_HEADS * IDX_DIM, IDX_DIM, N_IDX_HEADS, d, d)
    offs = np.concatenate([[0], np.cumsum(widths)])
    wi = w_in[l]
    piece = lambda a, b: wi[:, offs[a]:offs[b]].astype(BF16)
    wsm = jnp.concatenate([wi[:, offs[5]:offs[7]], jnp.zeros((d, LANES - IDX_DIM - N_IDX_HEADS), F32)], axis=1)
    wr = jnp.concatenate([w_router_expert[l], w_router_group[l],
                          jnp.zeros((d, LANES - N_EXPERTS - N_EXPERT_GROUPS), F32)], axis=1)
    wr_hi = wr.astype(BF16)
    br = jnp.concatenate([b_router_expert[l], b_router_group[l],
                          jnp.zeros((LANES - N_EXPERTS - N_EXPERT_GROUPS,), F32)])[None, :]
    flat_e = lambda a: a.transpose(1, 0, 2).reshape(d, N_EXPERTS * D_EXPERT).astype(BF16)
    return dict(
        gmix=norm_mix_g[l][None, :], gffn=norm_ffn_g[l][None, :], gfinal=norm_final_g[None, :],
        wz=piece(0, 1), wq=piece(1, 2), wkv=piece(2, 4), wqi=piece(4, 5), wsm=wsm.astype(BF16), wgab=piece(7, 9),
        wpool=w_pool[l].astype(BF16), pscale=pool_scale[l][None, :],
        wao=w_attn_out[l].astype(BF16), wout=w_out[l].astype(BF16),
        wr=jnp.concatenate([wr_hi, (wr - wr_hi.astype(F32)).astype(BF16)], axis=1), br=br,
        wg=flat_e(w_expert_gate[l]), wu=flat_e(w_expert_up[l]),
        wd=w_expert_down[l].reshape(N_EXPERTS * D_EXPERT, d).astype(BF16),
    )


def _dup_lanes(ki):
    return jnp.concatenate([ki, ki], axis=-1)


def kernel(x_prompt, x_sample, cache_k, cache_v, cache_idx_k, state_pool, meta_tokens, norm_mix_g, norm_ffn_g,
           norm_final_g, w_in, w_pool, pool_scale, w_attn_out, w_out, w_router_group, b_router_group,
           w_router_expert, b_router_expert, w_expert_gate, w_expert_up, w_expert_down):
    bsz, seq, d = x_prompt.shape
    dbsz, dec, _ = x_sample.shape
    assert w_in.shape[0] == 1, "single-layer step"
    l = 0
    past = cache_k.shape[2] - N_META
    n = N_META + seq
    topk_p = min(IDX_TOPK, seq // 4)
    topk_s = min(IDX_TOPK, (past + dec) // 4)
    kvw = N_KV_HEADS * HEAD_DIM

    tk = ATTN_TK
    tm = tk
    assert seq % tm == 0 and N_META <= tm
    sp = seq + tm
    nt = sp // tm
    tm_moe = _pick_tile(seq, 1100)
    w = _prep_weights(l, norm_mix_g, norm_ffn_g, norm_final_g, w_in, w_pool, pool_scale, w_attn_out, w_out,
                      w_router_group, b_router_group, w_router_expert, b_router_expert,
                      w_expert_gate, w_expert_up, w_expert_down)

    freq = _rope_freqs()
    xp, z, q, k_c, v_c, kb, vt, qis, sm, kid, sga, sgb = _inproj(
        x_prompt, meta_tokens.astype(F32), w["gmix"], freq, w, tm=tm, n_out=n, lead=N_META, pos_base=0, pos_period=0)
    b3 = lambda a: a.reshape(bsz, sp, a.shape[-1])
    o = _attention(b3(qis), b3(sm), b3(kid), b3(q), b3(kb), vt, tq=ATTN_TQ_PROMPT, n_q=seq, lead=N_META,
                   topk=topk_p, causal=True, n_keys=n)
    hs = POOL_STATE + 1
    z4 = z.reshape(bsz, nt, tm, d)
    halo = jnp.concatenate([jnp.zeros((bsz, 1, hs, d), F32), z4[:, :-1, tm - hs:, :]], axis=1).reshape(bsz * nt, hs, d)
    tbase = jnp.asarray(np.tile(np.arange(nt) * tm, bsz), I32)
    x1 = _mix(tbase, o, z, halo, sga, sgb, xp, w, tm=tm, lead=N_META)
    y_prompt = _moe(b3(x1), w, tm=tm_moe, n_out=seq, lead=N_META)
    k_prompt = k_c[None]
    v_prompt = v_c[None]
    idx_k_prompt = b3(sm)[:, :n, :IDX_DIM][None]
    pool_prompt = b3(z)[:, n - POOL_STATE:n][None]

    rows_s = dbsz * dec
    xs = x_sample.reshape(rows_s, d)
    _, zs, qs, ks_c, vs_c, kbs, _, qiss, sms, kids, sgas, sgbs = _inproj(
        xs[None], jnp.zeros((SUBLANES, d), F32), w["gmix"], freq, w, tm=rows_s, n_out=rows_s, lead=0,
        pos_base=N_META + past, pos_period=dec)
    vs = vs_c.reshape(rows_s, kvw)
    nk = N_META + past + dec
    sk = _round_up(nk, tk)
    sq_s = _round_up(dec, ATTN_TQ)
    s3 = lambda a: a.reshape(dbsz, dec, a.shape[-1])
    padr = lambda a, rows: jnp.concatenate([a, jnp.zeros((dbsz, rows - a.shape[1], a.shape[-1]), a.dtype)], axis=1)
    k_all, vt_all = _sample_keys(cache_k[l], cache_v[l], s3(kbs), s3(vs), tk=tk)
    assert k_all.shape[1] == sk
    ki_all = padr(jnp.concatenate([_dup_lanes(cache_idx_k[l]).astype(BF16), s3(kids)], axis=1), sk)
    os_ = _attention(padr(s3(qiss), sq_s), padr(s3(sms), sq_s), ki_all, padr(s3(qs), sq_s), k_all, vt_all,
                     tq=ATTN_TQ, n_q=sq_s, lead=0, topk=topk_s, causal=False, n_keys=nk)[:, :dec]
    zcat = jnp.concatenate([state_pool[l].astype(F32), s3(zs)], axis=1)
    halo_s = jnp.concatenate([jnp.zeros((dbsz, 1, d), F32), zcat[:, :POOL_STATE]], axis=1)
    tbase_s = jnp.full((dbsz,), POOL_STATE, I32)
    x1s = _mix(tbase_s, os_, zs, halo_s, sgas, sgbs, xs, w, tm=dec, lead=0)
    y_sample = _moe(x1s[None], w, tm=rows_s, n_out=rows_s, lead=0).reshape(dbsz, dec, d)
    k_sample = ks_c.reshape(1, dbsz, dec, N_KV_HEADS, HEAD_DIM)
    v_sample = vs_c.reshape(1, dbsz, dec, N_KV_HEADS, HEAD_DIM)
    idx_k_sample = s3(sms)[:, :, :IDX_DIM][None]
    pool_sample = zcat[:, zcat.shape[1] - POOL_STATE:][None]

    return (y_prompt, y_sample, k_prompt, v_prompt, idx_k_prompt, pool_prompt,
            k_sample, v_sample, idx_k_sample, pool_sample)
```
